```python
import jax, jax.numpy as jnp
from jax import lax
import numpy as np

D_MODEL = 4096
BATCH = 16
SEQ = 2048
DEPTH = 1

N_META = 16
MIX_WIDTH = D_MODEL
CONV_CH = MIX_WIDTH // 2
POOL_CH = MIX_WIDTH - CONV_CH
CONV_HEADS = 16
CONV_K = 3
POOL_WINDOWS = (2, 4, 8, 16)
N_POOL_GROUPS = len(POOL_WINDOWS)
POOL_GROUP = POOL_CH // N_POOL_GROUPS
IN_COLS = 3 * CONV_CH + POOL_CH
D_FF = 256 * ((8 * D_MODEL // 3 + 255) // 256)
LN_EPS = 1e-5
ALPHA = (2.0 * DEPTH) ** 0.25
BETA = (8.0 * DEPTH) ** -0.25

kernel_name = "hybrid_conv_pool_macaron_deepnorm"


def layer_norm(x, g, b):
    xf = x.astype(jnp.float32)
    mu = jnp.mean(xf, axis=-1, keepdims=True)
    xc = xf - mu
    var = jnp.mean(jnp.square(xc), axis=-1, keepdims=True)
    y = xc * lax.rsqrt(var + LN_EPS) * g.astype(jnp.float32) + b.astype(jnp.float32)
    return y.astype(x.dtype)


def swiglu_ffn(x, w_gu, w_down):
    gu = jnp.einsum('bld,df->blf', x, w_gu)
    gate, up = jnp.split(gu, 2, axis=-1)
    return jnp.einsum('blf,fd->bld', jax.nn.silu(gate) * up, w_down)


def causal_short_conv(z, w):
    L = z.shape[1]
    zp = jnp.pad(z, ((0, 0), (CONV_K - 1, 0), (0, 0)))
    y = zp[:, 0:L] * w[0]
    for k in range(1, CONV_K):
        y = y + zp[:, k:k + L] * w[k]
    return y


def causal_window_mean(z, window):
    L = z.shape[1]
    cs = jnp.cumsum(z, axis=1)
    prev = jnp.pad(cs, ((0, 0), (window, 0), (0, 0)))[:, :L]
    count = jnp.minimum(jnp.arange(1, L + 1), window).astype(jnp.float32)
    return (cs - prev) / count[None, :, None]


def pooling_mixer(z, pool_w, pool_scale):
    b, L, _ = z.shape
    zg = z.reshape(b, L, N_POOL_GROUPS, POOL_GROUP).astype(jnp.float32)
    pooled = jnp.stack([causal_window_mean(zg[:, :, g], POOL_WINDOWS[g])
                        for g in range(N_POOL_GROUPS)], axis=2)
    d = (pooled - zg).astype(z.dtype)
    y = jnp.einsum('blgc,gcd->blgd', d, pool_w).reshape(b, L, POOL_CH)
    return y * pool_scale


def hybrid_mixer(h, w_in, conv_w, pool_w, pool_scale, w_out):
    u = jnp.einsum('bld,dc->blc', h, w_in)
    gate_b = u[..., 0:CONV_CH]
    gate_c = u[..., CONV_CH:2 * CONV_CH]
    x_in = u[..., 2 * CONV_CH:3 * CONV_CH]
    z_pool = u[..., 3 * CONV_CH:]
    y_conv = gate_b * causal_short_conv(gate_c * x_in, conv_w)
    y_pool = pooling_mixer(z_pool, pool_w, pool_scale)
    y = jnp.concatenate([y_conv, y_pool], axis=-1)
    return jnp.einsum('blc,cd->bld', y, w_out)


def _fwd_setup_inputs(seed: int = 0) -> dict:
    key = jax.random.key(seed)
    ks = jax.random.split(key, 20)
    f32 = jnp.float32
    D, F = D_MODEL, D_FF

    def nrm(k, shape, scale):
        return jax.random.normal(k, shape, f32) * scale

    def gain(k):
        return 1.0 + 0.05 * jax.random.normal(k, (DEPTH, D), f32)

    def bias(k):
        return 0.02 * jax.random.normal(k, (DEPTH, D), f32)

    return {
        "x": jax.random.normal(ks[0], (BATCH, SEQ, D), f32),
        "meta_tokens": nrm(ks[1], (N_META, D), 1.0),
        "ffn1_w_gu": nrm(ks[2], (DEPTH, D, 2 * F), D ** -0.5),
        "ffn1_w_down": nrm(ks[3], (DEPTH, F, D), BETA * F ** -0.5),
        "ln1_g": gain(ks[4]),
        "ln1_b": bias(ks[5]),
        "w_in": nrm(ks[6], (DEPTH, D, IN_COLS), D ** -0.5),
        "conv_w": nrm(ks[7], (DEPTH, CONV_K, CONV_CH), CONV_K ** -0.5),
        "pool_w": nrm(ks[8], (DEPTH, N_POOL_GROUPS, POOL_GROUP, POOL_GROUP), POOL_GROUP ** -0.5),
        "pool_scale": 1.0 + 0.1 * jax.random.normal(ks[9], (DEPTH, POOL_CH), f32),
        "w_out": nrm(ks[10], (DEPTH, MIX_WIDTH, D), BETA * MIX_WIDTH ** -0.5),
        "ln2_g": gain(ks[11]),
        "ln2_b": bias(ks[12]),
        "ffn2_w_gu": nrm(ks[13], (DEPTH, D, 2 * F), D ** -0.5),
        "ffn2_w_down": nrm(ks[14], (DEPTH, F, D), BETA * F ** -0.5),
        "ln3_g": gain(ks[15]),
        "ln3_b": bias(ks[16]),
    }


def _fwd_reference(x, meta_tokens, ffn1_w_gu, ffn1_w_down, ln1_g, ln1_b, w_in, conv_w, pool_w,
              pool_scale, w_out, ln2_g, ln2_b, ffn2_w_gu, ffn2_w_down, ln3_g, ln3_b):
    b = x.shape[0]
    meta = jnp.broadcast_to(meta_tokens.astype(x.dtype)[None], (b, N_META, D_MODEL))
    h = jnp.concatenate([meta, x], axis=1)
    for i in range(DEPTH):
        h = layer_norm(ALPHA * h + 0.5 * swiglu_ffn(h, ffn1_w_gu[i], ffn1_w_down[i]),
                       ln1_g[i], ln1_b[i])
        h = layer_norm(ALPHA * h + hybrid_mixer(h, w_in[i], conv_w[i], pool_w[i],
                                                 pool_scale[i], w_out[i]),
                       ln2_g[i], ln2_b[i])
        h = layer_norm(ALPHA * h + 0.5 * swiglu_ffn(h, ffn2_w_gu[i], ffn2_w_down[i]),
                       ln3_g[i], ln3_b[i])
    return h[:, N_META:]


import jax as _jax
import jax.numpy as _jnp

TWIN_FORMAT = 'train_step'
FWD_PARAMS = ['x', 'meta_tokens', 'ffn1_w_gu', 'ffn1_w_down', 'ln1_g', 'ln1_b', 'w_in', 'conv_w', 'pool_w', 'pool_scale', 'w_out', 'ln2_g', 'ln2_b', 'ffn2_w_gu', 'ffn2_w_down', 'ln3_g', 'ln3_b']
TWIN_WEIGHTS = ['meta_tokens', 'ffn1_w_gu', 'ffn1_w_down', 'ln1_g', 'ln1_b', 'w_in', 'conv_w', 'pool_w', 'pool_scale', 'w_out', 'ln2_g', 'ln2_b', 'ffn2_w_gu', 'ffn2_w_down', 'ln3_g', 'ln3_b']
TWIN_DIFF_INPUT = 'x'
TWIN_INPUTS = ['x', 'meta_tokens', 'ffn1_w_gu', 'ffn1_w_down', 'ln1_g', 'ln1_b', 'w_in', 'conv_w', 'pool_w', 'pool_scale', 'w_out', 'ln2_g', 'ln2_b', 'ffn2_w_gu', 'ffn2_w_down', 'ln3_g', 'ln3_b', 'loss_target', 'm_meta_tokens', 'm_ffn1_w_gu', 'm_ffn1_w_down', 'm_ln1_g', 'm_ln1_b', 'm_w_in', 'm_conv_w', 'm_pool_w', 'm_pool_scale', 'm_w_out', 'm_ln2_g', 'm_ln2_b', 'm_ffn2_w_gu', 'm_ffn2_w_down', 'm_ln3_g', 'm_ln3_b', 'v_meta_tokens', 'v_ffn1_w_gu', 'v_ffn1_w_down', 'v_ln1_g', 'v_ln1_b', 'v_w_in', 'v_conv_w', 'v_pool_w', 'v_pool_scale', 'v_w_out', 'v_ln2_g', 'v_ln2_b', 'v_ffn2_w_gu', 'v_ffn2_w_down', 'v_ln3_g', 'v_ln3_b']
TWIN_OUTPUTS = ['loss', 'grad_x', 'grad_meta_tokens', 'grad_ffn1_w_gu', 'grad_ffn1_w_down', 'grad_ln1_g', 'grad_ln1_b', 'grad_w_in', 'grad_conv_w', 'grad_pool_w', 'grad_pool_scale', 'grad_w_out', 'grad_ln2_g', 'grad_ln2_b', 'grad_ffn2_w_gu', 'grad_ffn2_w_down', 'grad_ln3_g', 'grad_ln3_b', 'delta_meta_tokens', 'delta_ffn1_w_gu', 'delta_ffn1_w_down', 'delta_ln1_g', 'delta_ln1_b', 'delta_w_in', 'delta_conv_w', 'delta_pool_w', 'delta_pool_scale', 'delta_w_out', 'delta_ln2_g', 'delta_ln2_b', 'delta_ffn2_w_gu', 'delta_ffn2_w_down', 'delta_ln3_g', 'delta_ln3_b', 'new_m_meta_tokens', 'new_m_ffn1_w_gu', 'new_m_ffn1_w_down', 'new_m_ln1_g', 'new_m_ln1_b', 'new_m_w_in', 'new_m_conv_w', 'new_m_pool_w', 'new_m_pool_scale', 'new_m_w_out', 'new_m_ln2_g', 'new_m_ln2_b', 'new_m_ffn2_w_gu', 'new_m_ffn2_w_down', 'new_m_ln3_g', 'new_m_ln3_b', 'new_v_meta_tokens', 'new_v_ffn1_w_gu', 'new_v_ffn1_w_down', 'new_v_ln1_g', 'new_v_ln1_b', 'new_v_w_in', 'new_v_conv_w', 'new_v_pool_w', 'new_v_pool_scale', 'new_v_w_out', 'new_v_ln2_g', 'new_v_ln2_b', 'new_v_ffn2_w_gu', 'new_v_ffn2_w_down', 'new_v_ln3_g', 'new_v_ln3_b']
TWIN_LEAF_KINDS = {'loss': 'loss', 'grad_x': 'grad_x', 'grad_meta_tokens': 'grad_w', 'grad_ffn1_w_gu': 'grad_w', 'grad_ffn1_w_down': 'grad_w', 'grad_ln1_g': 'grad_w', 'grad_ln1_b': 'grad_w', 'grad_w_in': 'grad_w', 'grad_conv_w': 'grad_w', 'grad_pool_w': 'grad_w', 'grad_pool_scale': 'grad_w', 'grad_w_out': 'grad_w', 'grad_ln2_g': 'grad_w', 'grad_ln2_b': 'grad_w', 'grad_ffn2_w_gu': 'grad_w', 'grad_ffn2_w_down': 'grad_w', 'grad_ln3_g': 'grad_w', 'grad_ln3_b': 'grad_w', 'delta_meta_tokens': 'delta_w', 'delta_ffn1_w_gu': 'delta_w', 'delta_ffn1_w_down': 'delta_w', 'delta_ln1_g': 'delta_w', 'delta_ln1_b': 'delta_w', 'delta_w_in': 'delta_w', 'delta_conv_w': 'delta_w', 'delta_pool_w': 'delta_w', 'delta_pool_scale': 'delta_w', 'delta_w_out': 'delta_w', 'delta_ln2_g': 'delta_w', 'delta_ln2_b': 'delta_w', 'delta_ffn2_w_gu': 'delta_w', 'delta_ffn2_w_down': 'delta_w', 'delta_ln3_g': 'delta_w', 'delta_ln3_b': 'delta_w', 'new_m_meta_tokens': 'new_m', 'new_m_ffn1_w_gu': 'new_m', 'new_m_ffn1_w_down': 'new_m', 'new_m_ln1_g': 'new_m', 'new_m_ln1_b': 'new_m', 'new_m_w_in': 'new_m', 'new_m_conv_w': 'new_m', 'new_m_pool_w': 'new_m', 'new_m_pool_scale': 'new_m', 'new_m_w_out': 'new_m', 'new_m_ln2_g': 'new_m', 'new_m_ln2_b': 'new_m', 'new_m_ffn2_w_gu': 'new_m', 'new_m_ffn2_w_down': 'new_m', 'new_m_ln3_g': 'new_m', 'new_m_ln3_b': 'new_m', 'new_v_meta_tokens': 'new_v', 'new_v_ffn1_w_gu': 'new_v', 'new_v_ffn1_w_down': 'new_v', 'new_v_ln1_g': 'new_v', 'new_v_ln1_b': 'new_v', 'new_v_w_in': 'new_v', 'new_v_conv_w': 'new_v', 'new_v_pool_w': 'new_v', 'new_v_pool_scale': 'new_v', 'new_v_w_out': 'new_v', 'new_v_ln2_g': 'new_v', 'new_v_ln2_b': 'new_v', 'new_v_ffn2_w_gu': 'new_v', 'new_v_ffn2_w_down': 'new_v', 'new_v_ln3_g': 'new_v', 'new_v_ln3_b': 'new_v'}


def _forward(args):
    return _fwd_reference(*[args[k] for k in FWD_PARAMS])


def _output_shape():
    def fwd():
        inp = _fwd_setup_inputs(0)
        return _fwd_reference(*[inp[k] for k in FWD_PARAMS])
    out = _jax.eval_shape(fwd)
    return out.shape, out.dtype

N_MICROBATCH = 1
ADAM_LR = 0.001
ADAM_B1 = 0.9
ADAM_B2 = 0.999
ADAM_EPS = 1e-08
ADAM_WD = 0.01
ADAM_STEP = 10
PER_EXAMPLE_BATCH_AXIS = {'x': 0, 'loss_target': 0}
SHARED_INPUTS = []
_WEIGHT_DTYPES = {'meta_tokens': _jnp.float32, 'ffn1_w_gu': _jnp.float32, 'ffn1_w_down': _jnp.float32, 'ln1_g': _jnp.float32, 'ln1_b': _jnp.float32, 'w_in': _jnp.float32, 'conv_w': _jnp.float32, 'pool_w': _jnp.float32, 'pool_scale': _jnp.float32, 'w_out': _jnp.float32, 'ln2_g': _jnp.float32, 'ln2_b': _jnp.float32, 'ffn2_w_gu': _jnp.float32, 'ffn2_w_down': _jnp.float32, 'ln3_g': _jnp.float32, 'ln3_b': _jnp.float32}
MOMENT_SCALE = {'meta_tokens': 3.484105e-04, 'ffn1_w_gu': 4.515311e-03, 'ffn1_w_down': 1.225531e-02, 'ln1_g': 6.241680e-01, 'ln1_b': 1.276223e-01, 'w_in': 1.976494e-02, 'conv_w': 2.060433e-02, 'pool_w': 1.810085e-02, 'pool_scale': 1.801578e-02, 'w_out': 3.242705e-02, 'ln2_g': 7.674088e-01, 'ln2_b': 1.432996e-01, 'ffn2_w_gu': 4.078821e-03, 'ffn2_w_down': 1.108172e-02, 'ln3_g': 8.067984e+00, 'ln3_b': 2.294101e-01}


def _to_microbatches(a, axis):
    t = _jnp.moveaxis(a, axis, 0)
    t = t.reshape((N_MICROBATCH, t.shape[0] // N_MICROBATCH) + t.shape[1:])
    return _jnp.moveaxis(t, 1, axis + 1)


def setup_inputs(seed: int = 0) -> dict:
    inp = _fwd_setup_inputs(seed)
    key = _jax.random.fold_in(_jax.random.key(seed), 7919)
    shape, _ = _output_shape()
    out = dict(inp)
    out["loss_target"] = _jax.random.normal(_jax.random.fold_in(key, 0), shape, _jnp.float32)
    for i, name in enumerate(TWIN_WEIGHTS):
        w = inp[name].astype(_jnp.float32)
        if MOMENT_SCALE is None:
            s = _jnp.sqrt(_jnp.mean(_jnp.square(w)) + 1e-30)
        else:
            s = MOMENT_SCALE[name]
        km, kv = _jax.random.split(_jax.random.fold_in(key, i + 1))
        out[name] = w
        out["m_" + name] = s * _jax.random.normal(km, w.shape, _jnp.float32)
        out["v_" + name] = (s * s) * _jax.random.uniform(kv, w.shape, _jnp.float32, 0.5, 1.5)
    if N_MICROBATCH > 1:
        for name, axis in PER_EXAMPLE_BATCH_AXIS.items():
            out[name] = _to_microbatches(out[name], axis)
    return {'x': out['x'], 'meta_tokens': out['meta_tokens'], 'ffn1_w_gu': out['ffn1_w_gu'], 'ffn1_w_down': out['ffn1_w_down'], 'ln1_g': out['ln1_g'], 'ln1_b': out['ln1_b'], 'w_in': out['w_in'], 'conv_w': out['conv_w'], 'pool_w': out['pool_w'], 'pool_scale': out['pool_scale'], 'w_out': out['w_out'], 'ln2_g': out['ln2_g'], 'ln2_b': out['ln2_b'], 'ffn2_w_gu': out['ffn2_w_gu'], 'ffn2_w_down': out['ffn2_w_down'], 'ln3_g': out['ln3_g'], 'ln3_b': out['ln3_b'], 'loss_target': out['loss_target'], 'm_meta_tokens': out['m_meta_tokens'], 'm_ffn1_w_gu': out['m_ffn1_w_gu'], 'm_ffn1_w_down': out['m_ffn1_w_down'], 'm_ln1_g': out['m_ln1_g'], 'm_ln1_b': out['m_ln1_b'], 'm_w_in': out['m_w_in'], 'm_conv_w': out['m_conv_w'], 'm_pool_w': out['m_pool_w'], 'm_pool_scale': out['m_pool_scale'], 'm_w_out': out['m_w_out'], 'm_ln2_g': out['m_ln2_g'], 'm_ln2_b': out['m_ln2_b'], 'm_ffn2_w_gu': out['m_ffn2_w_gu'], 'm_ffn2_w_down': out['m_ffn2_w_down'], 'm_ln3_g': out['m_ln3_g'], 'm_ln3_b': out['m_ln3_b'], 'v_meta_tokens': out['v_meta_tokens'], 'v_ffn1_w_gu': out['v_ffn1_w_gu'], 'v_ffn1_w_down': out['v_ffn1_w_down'], 'v_ln1_g': out['v_ln1_g'], 'v_ln1_b': out['v_ln1_b'], 'v_w_in': out['v_w_in'], 'v_conv_w': out['v_conv_w'], 'v_pool_w': out['v_pool_w'], 'v_pool_scale': out['v_pool_scale'], 'v_w_out': out['v_w_out'], 'v_ln2_g': out['v_ln2_g'], 'v_ln2_b': out['v_ln2_b'], 'v_ffn2_w_gu': out['v_ffn2_w_gu'], 'v_ffn2_w_down': out['v_ffn2_w_down'], 'v_ln3_g': out['v_ln3_g'], 'v_ln3_b': out['v_ln3_b']}


def _loss(weights, diff, rest, loss_target):
    with _jax.named_scope("forward"):
        args = {**rest, TWIN_DIFF_INPUT: diff, **{k: w.astype(_WEIGHT_DTYPES[k]) for k, w in weights.items()}}
        y = _forward(args)
    with _jax.named_scope("loss_head"):
        err = _jnp.square(y.astype(_jnp.float32) - loss_target)
        return 0.5 * _jnp.sum(_jnp.mean(err, axis=-1)) if err.ndim else 0.5 * err


def _adamw(w, g, m, v):
    m = ADAM_B1 * m + (1.0 - ADAM_B1) * g
    v = ADAM_B2 * v + (1.0 - ADAM_B2) * _jnp.square(g)
    m_hat = m / (1.0 - ADAM_B1 ** ADAM_STEP)
    v_hat = v / (1.0 - ADAM_B2 ** ADAM_STEP)
    delta = -ADAM_LR * (m_hat / (_jnp.sqrt(v_hat) + ADAM_EPS) + ADAM_WD * w)
    return delta, m, v


def reference(x, meta_tokens, ffn1_w_gu, ffn1_w_down, ln1_g, ln1_b, w_in, conv_w, pool_w, pool_scale, w_out, ln2_g, ln2_b, ffn2_w_gu, ffn2_w_down, ln3_g, ln3_b, loss_target, m_meta_tokens, m_ffn1_w_gu, m_ffn1_w_down, m_ln1_g, m_ln1_b, m_w_in, m_conv_w, m_pool_w, m_pool_scale, m_w_out, m_ln2_g, m_ln2_b, m_ffn2_w_gu, m_ffn2_w_down, m_ln3_g, m_ln3_b, v_meta_tokens, v_ffn1_w_gu, v_ffn1_w_down, v_ln1_g, v_ln1_b, v_w_in, v_conv_w, v_pool_w, v_pool_scale, v_w_out, v_ln2_g, v_ln2_b, v_ffn2_w_gu, v_ffn2_w_down, v_ln3_g, v_ln3_b):
    given = dict(x=x, meta_tokens=meta_tokens, ffn1_w_gu=ffn1_w_gu, ffn1_w_down=ffn1_w_down, ln1_g=ln1_g, ln1_b=ln1_b, w_in=w_in, conv_w=conv_w, pool_w=pool_w, pool_scale=pool_scale, w_out=w_out, ln2_g=ln2_g, ln2_b=ln2_b, ffn2_w_gu=ffn2_w_gu, ffn2_w_down=ffn2_w_down, ln3_g=ln3_g, ln3_b=ln3_b, loss_target=loss_target, m_meta_tokens=m_meta_tokens, m_ffn1_w_gu=m_ffn1_w_gu, m_ffn1_w_down=m_ffn1_w_down, m_ln1_g=m_ln1_g, m_ln1_b=m_ln1_b, m_w_in=m_w_in, m_conv_w=m_conv_w, m_pool_w=m_pool_w, m_pool_scale=m_pool_scale, m_w_out=m_w_out, m_ln2_g=m_ln2_g, m_ln2_b=m_ln2_b, m_ffn2_w_gu=m_ffn2_w_gu, m_ffn2_w_down=m_ffn2_w_down, m_ln3_g=m_ln3_g, m_ln3_b=m_ln3_b, v_meta_tokens=v_meta_tokens, v_ffn1_w_gu=v_ffn1_w_gu, v_ffn1_w_down=v_ffn1_w_down, v_ln1_g=v_ln1_g, v_ln1_b=v_ln1_b, v_w_in=v_w_in, v_conv_w=v_conv_w, v_pool_w=v_pool_w, v_pool_scale=v_pool_scale, v_w_out=v_w_out, v_ln2_g=v_ln2_g, v_ln2_b=v_ln2_b, v_ffn2_w_gu=v_ffn2_w_gu, v_ffn2_w_down=v_ffn2_w_down, v_ln3_g=v_ln3_g, v_ln3_b=v_ln3_b)
    weights = {n: given[n] for n in TWIN_WEIGHTS}
    shared = {n: given[n] for n in SHARED_INPUTS}
    per_example = {n: given[n] for n in ['x']}
    grad_fn = _jax.value_and_grad(_loss, argnums=(0, 1))

    def one_microbatch(ex, loss_target):
        ex = dict(ex)
        diff = ex.pop(TWIN_DIFF_INPUT)
        return grad_fn(weights, diff, {**shared, **ex}, loss_target)

    if N_MICROBATCH == 1:
        loss, (grad_w, grad_x) = one_microbatch(per_example, given["loss_target"])
    else:
        def body(carry, xs):
            loss_sum, grad_sum = carry
            l_k, (gw_k, gx_k) = one_microbatch(xs[0], xs[1])
            with _jax.named_scope("update"):
                return (loss_sum + l_k, _jax.tree.map(_jnp.add, grad_sum, gw_k)), gx_k

        init = (_jnp.zeros((), _jnp.float32), _jax.tree.map(_jnp.zeros_like, weights))
        (loss, grad_w), grad_x = _jax.lax.scan(body, init, (per_example, given["loss_target"]))
    with _jax.named_scope("update"):
        delta_w, new_m, new_v = {}, {}, {}
        for n in TWIN_WEIGHTS:
            delta_w[n], new_m[n], new_v[n] = _adamw(weights[n], grad_w[n], given["m_" + n], given["v_" + n])
    return (loss, grad_x, *[grad_w[n] for n in TWIN_WEIGHTS], *[delta_w[n] for n in TWIN_WEIGHTS],
            *[new_m[n] for n in TWIN_WEIGHTS], *[new_v[n] for n in TWIN_WEIGHTS])
```

```python
import functools

import jax
import jax.numpy as jnp
from jax import lax
from jax.experimental import pallas as pl
from jax.experimental.pallas import tpu as pltpu

N_DEV = 8
N_CHIP = 4
N_META = 16
CONV_K = 3
POOL_WINDOWS = (2, 4, 8, 16)
N_POOL_GROUPS = len(POOL_WINDOWS)
LN_EPS = 1e-5
DEPTH = 1
ALPHA = (2.0 * DEPTH) ** 0.25
ADAM_LR = 0.001
ADAM_B1 = 0.9
ADAM_B2 = 0.999
ADAM_EPS = 1e-08
ADAM_WD = 0.01
ADAM_STEP = 10

V7X_VMEM_BYTES = 64 * 1024 * 1024
VMEM_LIMIT = V7X_VMEM_BYTES - 6 * 1024 * 1024
LANE = 128
ROW_ALIGN = 3 * LANE
TM_BIG = 1408
TM_WIDE = 704
TK = 512
TN = 1024
TR_LN = 128
ELEM_BLOCK_BYTES = 1 << 20
TC_MIX = LANE
EPILOGUE_ROWS = 64

NN = (((1,), (0,)), ((), ()))
NT = (((1,), (1,)), ((), ()))
TN_DIMS = (((0,), (0,)), ((), ()))
MESH = pl.DeviceIdType.MESH
BF16 = jnp.bfloat16
F32 = jnp.float32


def _tile(n, target, mult):
    best = None
    for t in range(mult, min(n, target) + 1, mult):
        if n % t == 0:
            best = t
    return n if best is None else best


def _params(sem):
    return pltpu.CompilerParams(dimension_semantics=sem, vmem_limit_bytes=VMEM_LIMIT)


def _sds(shape, dtype):
    return jax.ShapeDtypeStruct(shape, dtype)


def _row_chunks(n_rows, fn):
    ch = _tile(n_rows, EPILOGUE_ROWS, 16)

    def step(i, carry):
        fn(pl.ds(pl.multiple_of(i * ch, ch), ch))
        return carry

    lax.fori_loop(0, n_rows // ch, step, 0)


def _mm(name, grid, dims, ab, ab_specs, extras, extra_specs, out_shape, out_specs, acc_shape, epilogue):
    nk = grid[-1]
    n_extra = len(extras)
    n_out = len(out_shape)
    kax = len(grid) - 1

    def body(*refs):
        a_ref, b_ref = refs[0], refs[1]
        ex = refs[2:2 + n_extra]
        outs = refs[2 + n_extra:2 + n_extra + n_out]
        if nk == 1:
            epilogue(lax.dot_general(a_ref[...], b_ref[...], dims, preferred_element_type=F32), ex, outs, slice(None))
            return
        acc = refs[-1]
        k = pl.program_id(kax)

        @pl.when(k == 0)
        def _():
            acc[...] = jnp.zeros_like(acc)

        acc[...] += lax.dot_general(a_ref[...], b_ref[...], dims, preferred_element_type=F32)

        @pl.when(k == nk - 1)
        def _():
            _row_chunks(acc_shape[0], lambda rows: epilogue(acc[rows, :], ex, outs, rows))

    scratch = [] if nk == 1 else [pltpu.VMEM(acc_shape, F32)]
    sem = ("parallel",) * kax + ("arbitrary",)
    return pl.pallas_call(
        body, name=name, grid=grid, in_specs=list(ab_specs) + list(extra_specs), out_specs=list(out_specs),
        out_shape=list(out_shape), scratch_shapes=scratch, compiler_params=_params(sem),
    )(*ab, *extras)


def _silu_parts(g):
    s = 1.0 / (1.0 + jnp.exp(-g))
    return s, g * s


def _ffn_gu(name, h_b, wgu_all):
    tp, d = h_b.shape
    ns, _, ng = wgu_all.shape
    half = ns // 2
    tm, tk = _tile(tp, TM_WIDE, 16), _tile(d, TK, LANE)
    grid = (tp // tm, half, d // tk)
    nk = grid[-1]

    def body(h_ref, wg_ref, wu_ref, gu_ref, act_ref, acc_g, acc_u):
        k = pl.program_id(2)

        @pl.when(k == 0)
        def _():
            acc_g[...] = jnp.zeros_like(acc_g)
            acc_u[...] = jnp.zeros_like(acc_u)

        acc_g[...] += jnp.dot(h_ref[...], wg_ref[...], preferred_element_type=F32)
        acc_u[...] += jnp.dot(h_ref[...], wu_ref[...], preferred_element_type=F32)

        @pl.when(k == nk - 1)
        def _():
            def finish(rows):
                g = acc_g[rows, :]
                u = acc_u[rows, :]
                _, silu = _silu_parts(g)
                gu_ref[0, rows, :] = g.astype(BF16)
                gu_ref[1, rows, :] = u.astype(BF16)
                act_ref[rows, :] = (silu * u).astype(BF16)

            _row_chunks(tm, finish)

    return pl.pallas_call(
        body, name=name, grid=grid,
        in_specs=[pl.BlockSpec((tm, tk), lambda m, s, k: (m, k)),
                  pl.BlockSpec((None, tk, ng), lambda m, s, k: (s, k, 0)),
                  pl.BlockSpec((None, tk, ng), lambda m, s, k: (s + half, k, 0))],
        out_specs=[pl.BlockSpec((None, 2, tm, ng), lambda m, s, k: (s, 0, m, 0)),
                   pl.BlockSpec((None, tm, ng), lambda m, s, k: (s, m, 0))],
        out_shape=[_sds((half, 2, tp, ng), BF16), _sds((half, tp, ng), BF16)],
        scratch_shapes=[pltpu.VMEM((tm, ng), F32), pltpu.VMEM((tm, ng), F32)],
        compiler_params=_params(("parallel", "parallel", "arbitrary")),
    )(h_b, wgu_all, wgu_all)


def _ffn_down(name, act, wd4, h):
    ns, tp, ng = act.shape
    d = wd4.shape[2]
    tm, tn = _tile(tp, TM_WIDE, 16), _tile(d, TN, LANE)

    def epi(acc, ex, outs, rows):
        outs[0][rows, :] = ALPHA * ex[0][rows, :] + 0.5 * acc

    return _mm(name, (tp // tm, d // tn, ns), NN, (act, wd4),
               [pl.BlockSpec((None, tm, ng), lambda m, n, s: (s, m, 0)),
                pl.BlockSpec((None, ng, tn), lambda m, n, s: (s, 0, n))],
               (h,), [pl.BlockSpec((tm, tn), lambda m, n, s: (m, n))],
               [_sds((tp, d), F32)], [pl.BlockSpec((tm, tn), lambda m, n, s: (m, n))], (tm, tn), epi)[0]


def _ffn_bwd_dgu(name, dp_b, wd4, gu):
    tp, d = dp_b.shape
    ns, ng, _ = wd4.shape
    tm, tk = _tile(tp, TM_WIDE, 16), _tile(d, TK, LANE)

    def epi(acc, ex, outs, rows):
        g = ex[0][0, rows, :].astype(F32)
        u = ex[0][1, rows, :].astype(F32)
        da = 0.5 * acc
        s, silu = _silu_parts(g)
        outs[0][0, rows, :] = (da * u * (s + silu * (1.0 - s))).astype(BF16)
        outs[0][1, rows, :] = (da * silu).astype(BF16)

    return _mm(name, (tp // tm, ns, d // tk), NT, (dp_b, wd4),
               [pl.BlockSpec((tm, tk), lambda m, s, k: (m, k)),
                pl.BlockSpec((None, ng, tk), lambda m, s, k: (s, 0, k))],
               (gu,), [pl.BlockSpec((None, 2, tm, ng), lambda m, s, k: (s, 0, m, 0))],
               [_sds((ns, 2, tp, ng), BF16)], [pl.BlockSpec((None, 2, tm, ng), lambda m, s, k: (s, 0, m, 0))],
               (tm, ng), epi)[0]


def _ffn_bwd_wd(name, act, dp_b):
    ns, tp, ng = act.shape
    d = dp_b.shape[1]
    tkt, tn = _tile(tp, TM_WIDE, LANE), _tile(d, TN, LANE)

    def epi(acc, ex, outs, rows):
        outs[0][rows, :] = (0.5 * acc).astype(BF16)

    return _mm(name, (ns, d // tn, tp // tkt), TN_DIMS, (act, dp_b),
               [pl.BlockSpec((None, tkt, ng), lambda s, n, t: (s, t, 0)),
                pl.BlockSpec((tkt, tn), lambda s, n, t: (t, n))],
               (), [], [_sds((ns, ng, d), BF16)], [pl.BlockSpec((None, ng, tn), lambda s, n, t: (s, 0, n))],
               (ng, tn), epi)[0]


def _ffn_bwd_wgu(name, h_b, dgu):
    tp, d = h_b.shape
    ns, _, _, ng = dgu.shape
    tkt, tmd = _tile(tp, TM_WIDE, LANE), _tile(d, TN, LANE)

    def epi(acc, ex, outs, rows):
        outs[0][rows, :] = acc.astype(BF16)

    return _mm(name, (ns, 2, d // tmd, tp // tkt), TN_DIMS, (h_b, dgu),
               [pl.BlockSpec((tkt, tmd), lambda s, j, i, t: (t, i)),
                pl.BlockSpec((None, None, tkt, ng), lambda s, j, i, t: (s, j, t, 0))],
               (), [], [_sds((2 * ns, d, ng), BF16)],
               [pl.BlockSpec((None, tmd, ng), lambda s, j, i, t: (j * ns + s, i, 0))], (tmd, ng), epi)[0]


def _ffn_bwd_dh(name, dgu, wgu_all, dp):
    ns, _, tp, ng = dgu.shape
    d = wgu_all.shape[1]
    tm, tn = _tile(tp, TM_WIDE, 16), _tile(d, TN, LANE)

    def epi(acc, ex, outs, rows):
        outs[0][rows, :] = ALPHA * ex[0][rows, :] + acc

    return _mm(name, (tp // tm, d // tn, 2 * ns), NT, (dgu, wgu_all),
               [pl.BlockSpec((None, None, tm, ng), lambda m, n, j: (j % ns, j // ns, m, 0)),
                pl.BlockSpec((None, tn, ng), lambda m, n, j: (j, n, 0))],
               (dp,), [pl.BlockSpec((tm, tn), lambda m, n, j: (m, n))],
               [_sds((tp, d), F32)], [pl.BlockSpec((tm, tn), lambda m, n, j: (m, n))], (tm, tn), epi)[0]


def _ln_stats(x):
    mu = jnp.mean(x, axis=-1, keepdims=True)
    xc = x - mu
    var = jnp.mean(xc * xc, axis=-1, keepdims=True)
    rstd = lax.rsqrt(var + LN_EPS)
    return xc * rstd, rstd


def _ln_bwd_rows(dy, xhat, rstd, g):
    dxh = dy * g
    m1 = jnp.mean(dxh, axis=-1, keepdims=True)
    m2 = jnp.mean(dxh * xhat, axis=-1, keepdims=True)
    return rstd * (dxh - m1 - xhat * m2)


def _ln_fwd(name, pre, g, b):
    tp, d = pre.shape
    tr = _tile(tp, TR_LN, 16)

    def body(x_ref, g_ref, b_ref, y_ref, yb_ref):
        xhat, _ = _ln_stats(x_ref[...])
        y = xhat * g_ref[...] + b_ref[...]
        y_ref[...] = y
        yb_ref[...] = y.astype(BF16)

    row = pl.BlockSpec((tr, d), lambda i: (i, 0))
    vec = pl.BlockSpec((1, d), lambda i: (0, 0))
    return pl.pallas_call(body, name=name, grid=(tp // tr,), in_specs=[row, vec, vec], out_specs=[row, row],
                          out_shape=[_sds((tp, d), F32), _sds((tp, d), BF16)],
                          compiler_params=_params(("parallel",)))(pre, g, b)


def _accumulate(i, ref, val):
    @pl.when(i == 0)
    def _():
        ref[...] = val

    @pl.when(i > 0)
    def _():
        ref[...] += val


def _ln_bwd(name, pre, dy, g):
    tp, d = pre.shape
    tr = _tile(tp, TR_LN, 16)

    def body(x_ref, dy_ref, g_ref, dx_ref, dxb_ref, dg_ref, db_ref):
        i = pl.program_id(0)
        xhat, rstd = _ln_stats(x_ref[...])
        dy = dy_ref[...]
        dx = _ln_bwd_rows(dy, xhat, rstd, g_ref[...])
        dx_ref[...] = dx
        dxb_ref[...] = dx.astype(BF16)
        _accumulate(i, dg_ref, jnp.sum(dy * xhat, axis=0, keepdims=True))
        _accumulate(i, db_ref, jnp.sum(dy, axis=0, keepdims=True))

    row = pl.BlockSpec((tr, d), lambda i: (i, 0))
    vec = pl.BlockSpec((1, d), lambda i: (0, 0))
    return pl.pallas_call(body, name=name, grid=(tp // tr,), in_specs=[row, row, vec], out_specs=[row, row, vec, vec],
                          out_shape=[_sds((tp, d), F32), _sds((tp, d), BF16), _sds((1, d), F32), _sds((1, d), F32)],
                          compiler_params=_params(("arbitrary",)))(pre, dy, g)


def _ln_loss_bwd(name, pre, tgt, g, b, n_rows, seq_len):
    tp, d = pre.shape
    tr = _tile(tp, TR_LN, 16)
    n_seq = n_rows // seq_len

    def body(x_ref, t_ref, g_ref, b_ref, dx_ref, dxb_ref, dg_ref, db_ref, sq_ref):
        i = pl.program_id(0)
        xhat, rstd = _ln_stats(x_ref[...])
        gain = g_ref[...]
        y = xhat * gain + b_ref[...]
        r = i * tr + lax.broadcasted_iota(jnp.int32, (tr, 1), 0)
        pos = r
        for s in range(1, n_seq):
            pos = jnp.where(r >= s * seq_len, r - s * seq_len, pos)
        live = jnp.logical_and(r < n_rows, pos >= N_META)
        err = jnp.where(live, y - t_ref[...], 0.0)
        dy = err * (1.0 / d)
        dx = _ln_bwd_rows(dy, xhat, rstd, gain)
        dx_ref[...] = dx
        dxb_ref[...] = dx.astype(BF16)
        _accumulate(i, dg_ref, jnp.sum(dy * xhat, axis=0, keepdims=True))
        _accumulate(i, db_ref, jnp.sum(dy, axis=0, keepdims=True))
        _accumulate(i, sq_ref, jnp.sum(err * err, axis=0, keepdims=True))

    row = pl.BlockSpec((tr, d), lambda i: (i, 0))
    vec = pl.BlockSpec((1, d), lambda i: (0, 0))
    return pl.pallas_call(
        body, name=name, grid=(tp // tr,), in_specs=[row, row, vec, vec], out_specs=[row, row, vec, vec, vec],
        out_shape=[_sds((tp, d), F32), _sds((tp, d), BF16), _sds((1, d), F32), _sds((1, d), F32), _sds((1, d), F32)],
        compiler_params=_params(("arbitrary",)))(pre, tgt, g, b)


def _proj_in(name, h_b, win_all):
    tp, d = h_b.shape
    ns, _, ni = win_all.shape
    tm, tk = _tile(tp, TM_BIG, 16), _tile(d, TK, LANE)

    def epi(acc, ex, outs, rows):
        outs[0][rows, :] = acc

    return _mm(name, (tp // tm, ns, d // tk), NN, (h_b, win_all),
               [pl.BlockSpec((tm, tk), lambda m, j, k: (m, k)),
                pl.BlockSpec((None, tk, ni), lambda m, j, k: (j, k, 0))],
               (), [], [_sds((tp, ns * ni), F32)], [pl.BlockSpec((tm, ni), lambda m, j, k: (m, j))], (tm, ni), epi)[0]


def _positions(tp, n_rows, seq_len):
    r = lax.broadcasted_iota(jnp.int32, (tp, 1), 0)
    pos = r
    for s in range(1, n_rows // seq_len):
        pos = jnp.where(r >= s * seq_len, r - s * seq_len, pos)
    return pos


def _shift_down(x, s, pos):
    return jnp.where(pos >= s, pltpu.roll(x, s, 0), 0.0)


def _shift_up(x, s, pos, seq_len):
    return jnp.where(pos + s < seq_len, pltpu.roll(x, x.shape[0] - s, 0), 0.0)


def _conv_fwd(name, u, conv_w, n_rows, seq_len):
    tp = u.shape[0]
    c = conv_w.shape[1]
    tc = _tile(c, TC_MIX, LANE)
    nb = c // tc

    def body(gb_ref, gc_ref, xi_ref, w_ref, y_ref):
        pos = _positions(tp, n_rows, seq_len)
        v = gc_ref[...] * xi_ref[...]
        w = w_ref[...]
        y = _shift_down(v, 2, pos) * w[0:1]
        y = y + _shift_down(v, 1, pos) * w[1:2]
        y = y + v * w[2:3]
        y_ref[...] = (gb_ref[...] * y).astype(BF16)

    col = lambda off: pl.BlockSpec((tp, tc), lambda i: (0, off + i))
    return pl.pallas_call(body, name=name, grid=(nb,), in_specs=[col(0), col(nb), col(2 * nb), pl.BlockSpec((CONV_K, tc), lambda i: (0, i))],
                          out_specs=pl.BlockSpec((tp, tc), lambda i: (0, i)), out_shape=_sds((tp, c), BF16),
                          compiler_params=_params(("parallel",)))(u, u, u, conv_w)


def _conv_bwd(name, u, conv_w, dy, n_rows, seq_len):
    tp = u.shape[0]
    c = conv_w.shape[1]
    tc = _tile(c, TC_MIX, LANE)
    nb = c // tc

    def body(gb_ref, gc_ref, xi_ref, w_ref, dy_ref, dgb_ref, dgc_ref, dxi_ref, dw_ref):
        pos = _positions(tp, n_rows, seq_len)
        gc, xi = gc_ref[...], xi_ref[...]
        v = gc * xi
        w = w_ref[...]
        v2, v1 = _shift_down(v, 2, pos), _shift_down(v, 1, pos)
        conv = v2 * w[0:1]
        conv = conv + v1 * w[1:2]
        conv = conv + v * w[2:3]
        dyc = dy_ref[...]
        dgb_ref[...] = (dyc * conv).astype(BF16)
        dconv = dyc * gb_ref[...]
        dv = dconv * w[2:3] + _shift_up(dconv, 1, pos, seq_len) * w[1:2] + _shift_up(dconv, 2, pos, seq_len) * w[0:1]
        dgc_ref[...] = (dv * xi).astype(BF16)
        dxi_ref[...] = (dv * gc).astype(BF16)
        dw_ref[0:1, :] = jnp.sum(dconv * v2, axis=0, keepdims=True)
        dw_ref[1:2, :] = jnp.sum(dconv * v1, axis=0, keepdims=True)
        dw_ref[2:3, :] = jnp.sum(dconv * v, axis=0, keepdims=True)

    col = lambda off: pl.BlockSpec((tp, tc), lambda i: (0, off + i))
    wspec = pl.BlockSpec((CONV_K, tc), lambda i: (0, i))
    return pl.pallas_call(body, name=name, grid=(nb,), in_specs=[col(0), col(nb), col(2 * nb), wspec, col(0)],
                          out_specs=[col(0), col(0), col(0), wspec],
                          out_shape=[_sds((tp, c), BF16)] * 3 + [_sds((CONV_K, c), F32)],
                          compiler_params=_params(("parallel",)))(u, u, u, conv_w, dy)


def _window_select(group, parts):
    out = parts[-1]
    for gi in range(len(parts) - 2, -1, -1):
        out = jnp.where(group == gi, parts[gi], out)
    return out


def _pool_fwd(name, u, col0, p, pg, n_rows, seq_len):
    tp = u.shape[0]
    tc = _tile(pg, TC_MIX, LANE)
    per_group = pg // tc

    def body(z_ref, d_ref):
        group = pl.program_id(0) // per_group
        pos = _positions(tp, n_rows, seq_len)
        z = z_ref[...]
        sums, s, w = [], z, 1
        for _ in POOL_WINDOWS:
            s = s + _shift_down(s, w, pos)
            w *= 2
            sums.append(s)
        total = _window_select(group, sums)
        count = jnp.minimum(pos + 1, 2 << group).astype(F32)
        d_ref[...] = (total / count - z).astype(BF16)

    return pl.pallas_call(body, name=name, grid=(p // tc,), in_specs=[pl.BlockSpec((tp, tc), lambda i: (0, col0 // tc + i))],
                          out_specs=pl.BlockSpec((tp, tc), lambda i: (0, i)), out_shape=_sds((tp, p), BF16),
                          compiler_params=_params(("parallel",)))(u)


def _pool_bwd(name, dd, pg, n_rows, seq_len):
    tp, p = dd.shape
    tc = _tile(pg, TC_MIX, LANE)
    per_group = pg // tc

    def body(dd_ref, dz_ref):
        group = pl.program_id(0) // per_group
        pos = _positions(tp, n_rows, seq_len)
        dd_v = dd_ref[...]
        count = jnp.minimum(pos + 1, 2 << group).astype(F32)
        sums, s, w = [], dd_v / count, 1
        for _ in POOL_WINDOWS:
            s = s + _shift_up(s, w, pos, seq_len)
            w *= 2
            sums.append(s)
        dz_ref[...] = (_window_select(group, sums) - dd_v).astype(BF16)

    spec = pl.BlockSpec((tp, tc), lambda i: (0, i))
    return pl.pallas_call(body, name=name, grid=(p // tc,), in_specs=[spec], out_specs=spec, out_shape=_sds((tp, p), BF16),
                          compiler_params=_params(("parallel",)))(dd)


def _pool_mix(name, dpool, pool_w_b, scale):
    tp, p = dpool.shape
    ng, pg, _ = pool_w_b.shape
    tm = _tile(tp, TM_BIG, 16)

    def epi(acc, ex, outs, rows):
        outs[0][rows, :] = acc
        outs[1][rows, :] = (acc * ex[0][...]).astype(BF16)

    blk = pl.BlockSpec((tm, pg), lambda m, g, k: (m, g))
    return _mm(name, (tp // tm, ng, 1), NN, (dpool, pool_w_b), [blk, pl.BlockSpec((None, pg, pg), lambda m, g, k: (g, 0, 0))],
               (scale,), [pl.BlockSpec((1, pg), lambda m, g, k: (0, g))],
               [_sds((tp, p), F32), _sds((tp, p), BF16)], [blk, blk], None, epi)


def _pool_scale_bwd(name, dy, col_block, ypre, scale):
    tp, p = ypre.shape
    tr = _tile(tp, TR_LN, 16)

    def body(dy_ref, yp_ref, s_ref, o_ref, ds_ref):
        i = pl.program_id(0)
        dyp = dy_ref[...]
        o_ref[...] = (dyp * s_ref[...]).astype(BF16)
        _accumulate(i, ds_ref, jnp.sum(dyp * yp_ref[...], axis=0, keepdims=True))

    row = pl.BlockSpec((tr, p), lambda i: (i, 0))
    vec = pl.BlockSpec((1, p), lambda i: (0, 0))
    return pl.pallas_call(body, name=name, grid=(tp // tr,), in_specs=[pl.BlockSpec((tr, p), lambda i: (i, col_block)), row, vec],
                          out_specs=[row, vec], out_shape=[_sds((tp, p), BF16), _sds((1, p), F32)],
                          compiler_params=_params(("arbitrary",)))(dy, ypre, scale)


def _pool_mix_bwd_in(name, dyps, pool_w_b):
    tp, p = dyps.shape
    ng, pg, _ = pool_w_b.shape
    tm = _tile(tp, TM_BIG, 16)

    def epi(acc, ex, outs, rows):
        outs[0][rows, :] = acc

    blk = pl.BlockSpec((tm, pg), lambda m, g, k: (m, g))
    return _mm(name, (tp // tm, ng, 1), NT, (dyps, pool_w_b), [blk, pl.BlockSpec((None, pg, pg), lambda m, g, k: (g, 0, 0))],
               (), [], [_sds((tp, p), F32)], [blk], None, epi)[0]


def _pool_mix_bwd_w(name, dpool, dyps, pg):
    tp, p = dpool.shape
    ng = p // pg
    tkt = _tile(tp, TM_BIG, LANE)

    def epi(acc, ex, outs, rows):
        outs[0][rows, :] = acc

    blk = pl.BlockSpec((tkt, pg), lambda g, t: (t, g))
    return _mm(name, (ng, tp // tkt), TN_DIMS, (dpool, dyps), [blk, blk], (), [],
               [_sds((ng, pg, pg), F32)], [pl.BlockSpec((None, pg, pg), lambda g, t: (g, 0, 0))], (pg, pg), epi)[0]


def _proj_out(name, y, wout_all, h):
    tp = y.shape[0]
    ns, ro, d = wout_all.shape
    tm, tn = _tile(tp, TM_BIG, 16), _tile(d, TN, LANE)

    def epi(acc, ex, outs, rows):
        outs[0][rows, :] = ALPHA * ex[0][rows, :] + acc

    mn = pl.BlockSpec((tm, tn), lambda m, n, j: (m, n))
    return _mm(name, (tp // tm, d // tn, ns), NN, (y, wout_all),
               [pl.BlockSpec((tm, ro), lambda m, n, j: (m, j)), pl.BlockSpec((None, ro, tn), lambda m, n, j: (j, 0, n))],
               (h,), [mn], [_sds((tp, d), F32)], [mn], (tm, tn), epi)[0]


def _proj_out_bwd_y(name, dp_b, wout_all):
    tp, d = dp_b.shape
    ns, ro, _ = wout_all.shape
    tm, tk = _tile(tp, TM_BIG, 16), _tile(d, TK, LANE)

    def epi(acc, ex, outs, rows):
        outs[0][rows, :] = acc

    return _mm(name, (tp // tm, ns, d // tk), NT, (dp_b, wout_all),
               [pl.BlockSpec((tm, tk), lambda m, j, k: (m, k)), pl.BlockSpec((None, ro, tk), lambda m, j, k: (j, 0, k))],
               (), [], [_sds((tp, ns * ro), F32)], [pl.BlockSpec((tm, ro), lambda m, j, k: (m, j))], (tm, ro), epi)[0]


def _proj_out_bwd_w(name, y, dp_b, ns):
    tp, c = y.shape
    d = dp_b.shape[1]
    ro = c // ns
    tkt, tn = _tile(tp, TM_BIG, LANE), _tile(d, 2 * TN, LANE)

    def epi(acc, ex, outs, rows):
        outs[0][rows, :] = acc.astype(BF16)

    return _mm(name, (ns, d // tn, tp // tkt), TN_DIMS, (y, dp_b),
               [pl.BlockSpec((tkt, ro), lambda j, n, t: (t, j)), pl.BlockSpec((tkt, tn), lambda j, n, t: (t, n))],
               (), [], [_sds((ns, ro, d), BF16)], [pl.BlockSpec((None, ro, tn), lambda j, n, t: (j, 0, n))], (ro, tn), epi)[0]


def _proj_in_bwd_w(name, h_b, du, ns):
    tp, d = h_b.shape
    ni = du.shape[1] // ns
    tkt, tmd = _tile(tp, TM_WIDE, LANE), _tile(d, 2 * TN, LANE)

    def epi(acc, ex, outs, rows):
        outs[0][rows, :] = acc.astype(BF16)

    return _mm(name, (ns, d // tmd, tp // tkt), TN_DIMS, (h_b, du),
               [pl.BlockSpec((tkt, tmd), lambda j, i, t: (t, i)), pl.BlockSpec((tkt, ni), lambda j, i, t: (t, j))],
               (), [], [_sds((ns, d, ni), BF16)], [pl.BlockSpec((None, tmd, ni), lambda j, i, t: (j, i, 0))], (tmd, ni), epi)[0]


def _proj_in_bwd_h(name, du, win_all, dp):
    tp = du.shape[0]
    ns, d, ni = win_all.shape
    tm, tn = _tile(tp, TM_BIG, 16), _tile(d, TN, LANE)

    def epi(acc, ex, outs, rows):
        outs[0][rows, :] = ALPHA * ex[0][rows, :] + acc

    mn = pl.BlockSpec((tm, tn), lambda m, n, j: (m, n))
    return _mm(name, (tp // tm, d // tn, ns), NT, (du, win_all),
               [pl.BlockSpec((tm, ni), lambda m, n, j: (m, j)), pl.BlockSpec((None, tn, ni), lambda m, n, j: (j, n, 0))],
               (dp,), [mn], [_sds((tp, d), F32)], [mn], (tm, tn), epi)[0]


ANY = pl.BlockSpec(memory_space=pl.ANY)


def _place():
    return lax.axis_index("x"), lax.axis_index("y"), lax.axis_index("c")


def _all_gather(name, shard):
    def body(x_ref, out_ref, send_sems, recv_sems, local_sem):
        x, y, c = _place()
        me, sibling = (x, y, c), (x, y, 1 - c)
        chips = [(1 - x, y), (x, 1 - y), (1 - x, 1 - y)]

        def slab(px, py, pc):
            return out_ref.at[4 * px + 2 * py + pc]

        def copy(k, block, to, src=None):
            return pltpu.make_async_remote_copy(
                src_ref=slab(*block) if src is None else src, dst_ref=slab(*block),
                send_sem=send_sems.at[k], recv_sem=recv_sems.at[k], device_id=to, device_id_type=MESH)

        mine = pltpu.make_async_copy(x_ref, slab(*me), local_sem)
        mine.start()
        first = [copy(0, me, sibling, src=x_ref)]
        first += [copy(1 + j, me, (*chip, c), src=x_ref) for j, chip in enumerate(chips)]
        for cp in first:
            cp.start()
        passed = [copy(4 + j, (*chip, c), sibling) for j, chip in enumerate(chips)]
        for j, chip in enumerate(chips):
            copy(1 + j, (*chip, c), me).wait_recv()
            passed[j].start()
        copy(0, sibling, me).wait_recv()
        for j, chip in enumerate(chips):
            copy(4 + j, (*chip, 1 - c), me).wait_recv()
        for cp in first + passed:
            cp.wait_send()
        mine.wait()

    return pl.pallas_call(
        body, name=name, out_shape=_sds((N_DEV,) + shard.shape, shard.dtype), in_specs=[ANY], out_specs=ANY,
        scratch_shapes=[pltpu.SemaphoreType.DMA((N_DEV - 1,)), pltpu.SemaphoreType.DMA((N_DEV - 1,)), pltpu.SemaphoreType.DMA(())],
    )(shard)


def _exchange_sibling(name, part):
    def body(p_ref, r_ref, send_sems, recv_sems):
        x, y, c = _place()
        copies = [pltpu.make_async_remote_copy(
            src_ref=p_ref.at[2 * j + (1 - c)], dst_ref=r_ref.at[j], send_sem=send_sems.at[j], recv_sem=recv_sems.at[j],
            device_id=(x, y, 1 - c), device_id_type=MESH) for j in range(N_CHIP)]
        for cp in copies:
            cp.start()
        for cp in copies:
            cp.wait()

    return pl.pallas_call(
        body, name=name, out_shape=_sds((N_CHIP,) + part.shape[1:], part.dtype), in_specs=[ANY], out_specs=ANY,
        scratch_shapes=[pltpu.SemaphoreType.DMA((N_CHIP,)), pltpu.SemaphoreType.DMA((N_CHIP,))],
    )(part)


def _exchange_chips(name, s):
    def body(s_ref, r_ref, send_sems, recv_sems):
        x, y, c = _place()
        chips = [(1 - x, y), (x, 1 - y), (1 - x, 1 - y)]
        copies = [pltpu.make_async_remote_copy(
            src_ref=s_ref.at[2 * px + py], dst_ref=r_ref.at[k], send_sem=send_sems.at[k], recv_sem=recv_sems.at[k],
            device_id=(px, py, c), device_id_type=MESH) for k, (px, py) in enumerate(chips)]
        for cp in copies:
            cp.start()
        for cp in copies:
            cp.wait()

    return pl.pallas_call(
        body, name=name, out_shape=_sds((N_CHIP - 1,) + s.shape[1:], s.dtype), in_specs=[ANY], out_specs=ANY,
        scratch_shapes=[pltpu.SemaphoreType.DMA((N_CHIP - 1,)), pltpu.SemaphoreType.DMA((N_CHIP - 1,))],
    )(s)


def _add_sibling(name, part, recv, core):
    _, r, c = part.shape
    tr = _tile(r, max(16, 2 * ELEM_BLOCK_BYTES // (2 * c)), 16)

    def body(core_ref, p_ref, r_ref, o_ref):
        o_ref[...] = (p_ref[...].astype(F32) + r_ref[...].astype(F32)).astype(BF16)

    blk = pl.BlockSpec((None, tr, c), lambda j, i, core_ref: (j, i, 0))
    return pl.pallas_call(
        body, name=name,
        grid_spec=pltpu.PrefetchScalarGridSpec(
            num_scalar_prefetch=1, grid=(N_CHIP, r // tr),
            in_specs=[pl.BlockSpec((None, tr, c), lambda j, i, core_ref: (2 * j + core_ref[0], i, 0)), blk], out_specs=blk),
        out_shape=_sds((N_CHIP, r, c), BF16), compiler_params=_params(("parallel", "parallel")),
    )(core, part, recv)


def _adamw_math(w, g, m, v):
    m = ADAM_B1 * m + (1.0 - ADAM_B1) * g
    v = ADAM_B2 * v + (1.0 - ADAM_B2) * (g * g)
    m_hat = m / (1.0 - ADAM_B1 ** ADAM_STEP)
    v_hat = v / (1.0 - ADAM_B2 ** ADAM_STEP)
    delta = -ADAM_LR * (m_hat / (jnp.sqrt(v_hat) + ADAM_EPS) + ADAM_WD * w)
    return delta, m, v


def _adamw_big(name, s, recv, chip, w, m, v):
    r, c = w.shape
    tr = _tile(r, max(16, ELEM_BLOCK_BYTES // (4 * c)), 16)

    def body(chip_ref, s_ref, r_ref, w_ref, m_ref, v_ref, g_out, d_out, m_out, v_out):
        g = s_ref[...].astype(F32)
        for k in range(N_CHIP - 1):
            g = g + r_ref[k].astype(F32)
        delta, m_new, v_new = _adamw_math(w_ref[...], g, m_ref[...], v_ref[...])
        g_out[...] = g
        d_out[...] = delta
        m_out[...] = m_new
        v_out[...] = v_new

    blk = pl.BlockSpec((tr, c), lambda i, chip_ref: (i, 0))
    return pl.pallas_call(
        body, name=name,
        grid_spec=pltpu.PrefetchScalarGridSpec(
            num_scalar_prefetch=1, grid=(r // tr,),
            in_specs=[pl.BlockSpec((None, tr, c), lambda i, chip_ref: (chip_ref[0], i, 0)),
                      pl.BlockSpec((N_CHIP - 1, tr, c), lambda i, chip_ref: (0, i, 0)), blk, blk, blk],
            out_specs=[blk, blk, blk, blk]),
        out_shape=[_sds((r, c), F32)] * 4, compiler_params=_params(("parallel",)),
    )(chip, s, recv, w, m, v)


def _sum_parts(name, parts):
    n, r, c = parts.shape
    tr = _tile(r, max(8, ELEM_BLOCK_BYTES // (4 * c)), 8)

    def body(p_ref, o_ref):
        acc = p_ref[0]
        for k in range(1, n):
            acc = acc + p_ref[k]
        o_ref[...] = acc

    return pl.pallas_call(body, name=name, grid=(r // tr,), in_specs=[pl.BlockSpec((n, tr, c), lambda i: (0, i, 0))],
                          out_specs=pl.BlockSpec((tr, c), lambda i: (i, 0)), out_shape=_sds((r, c), F32),
                          compiler_params=_params(("parallel",)))(parts)


def _adamw_small(name, ws, gs, ms, vs):
    n = len(ws)

    def body(*refs):
        ins, outs = refs[:4 * n], refs[4 * n:]
        for i in range(n):
            delta, m_new, v_new = _adamw_math(ins[i][...], ins[n + i][...], ins[2 * n + i][...], ins[3 * n + i][...])
            outs[i][...] = delta
            outs[n + i][...] = m_new
            outs[2 * n + i][...] = v_new

    shapes = [_sds(w.shape, F32) for w in ws]
    return pl.pallas_call(body, name=name, out_shape=shapes * 3)(*ws, *gs, *ms, *vs)


def _reduce_scatter_adamw(tag, part, core, chip, w, m, v):
    from_sibling = _exchange_sibling(f"rs_sibling_{tag}", part)
    summed = _add_sibling(f"rs_add_{tag}", part, from_sibling, core)
    from_chips = _exchange_chips(f"rs_chips_{tag}", summed)
    return _adamw_big(f"adamw_{tag}", summed, from_chips, chip, w, m, v)


def _pad_rows(a, rows):
    return jnp.pad(a, ((0, rows - a.shape[0]), (0, 0)))


def kernel(x, meta_tokens, ffn1_w_gu, ffn1_w_down, ln1_g, ln1_b, w_in, conv_w, pool_w, pool_scale, w_out, ln2_g, ln2_b, ffn2_w_gu, ffn2_w_down, ln3_g, ln3_b, loss_target, m_meta_tokens, m_ffn1_w_gu, m_ffn1_w_down, m_ln1_g, m_ln1_b, m_w_in, m_conv_w, m_pool_w, m_pool_scale, m_w_out, m_ln2_g, m_ln2_b, m_ffn2_w_gu, m_ffn2_w_down, m_ln3_g, m_ln3_b, v_meta_tokens, v_ffn1_w_gu, v_ffn1_w_down, v_ln1_g, v_ln1_b, v_w_in, v_conv_w, v_pool_w, v_pool_scale, v_w_out, v_ln2_g, v_ln2_b, v_ffn2_w_gu, v_ffn2_w_down, v_ln3_g, v_ln3_b):
    names = ["meta_tokens", "ffn1_w_gu", "ffn1_w_down", "ln1_g", "ln1_b", "w_in", "conv_w", "pool_w", "pool_scale", "w_out",
             "ln2_g", "ln2_b", "ffn2_w_gu", "ffn2_w_down", "ln3_g", "ln3_b"]
    w_of = dict(zip(names, [meta_tokens, ffn1_w_gu, ffn1_w_down, ln1_g, ln1_b, w_in, conv_w, pool_w, pool_scale, w_out,
                            ln2_g, ln2_b, ffn2_w_gu, ffn2_w_down, ln3_g, ln3_b]))
    m_of = dict(zip(names, [m_meta_tokens, m_ffn1_w_gu, m_ffn1_w_down, m_ln1_g, m_ln1_b, m_w_in, m_conv_w, m_pool_w, m_pool_scale,
                            m_w_out, m_ln2_g, m_ln2_b, m_ffn2_w_gu, m_ffn2_w_down, m_ln3_g, m_ln3_b]))
    v_of = dict(zip(names, [v_meta_tokens, v_ffn1_w_gu, v_ffn1_w_down, v_ln1_g, v_ln1_b, v_w_in, v_conv_w, v_pool_w, v_pool_scale,
                            v_w_out, v_ln2_g, v_ln2_b, v_ffn2_w_gu, v_ffn2_w_down, v_ln3_g, v_ln3_b]))

    n_seq, seq, d = x.shape
    seq_len = seq + N_META
    n_rows = n_seq * seq_len
    tp = -(-n_rows // ROW_ALIGN) * ROW_ALIGN
    c_conv = conv_w.shape[2] * N_DEV
    p_pool = pool_scale.shape[1]
    pg = pool_w.shape[3]
    assert c_conv == p_pool and p_pool == N_POOL_GROUPS * pg and POOL_WINDOWS == tuple(2 << g for g in range(N_POOL_GROUPS))
    assert (N_POOL_GROUPS * pg * pg) % d == 0 and pg % LANE == 0

    xi, yi, ci = _place()
    dev = 4 * xi + 2 * yi + ci
    core = jnp.reshape(ci, (1,)).astype(jnp.int32)
    chip = jnp.reshape(2 * xi + yi, (1,)).astype(jnp.int32)

    big = ["ffn1_w_gu", "ffn1_w_down", "w_in", "w_out", "ffn2_w_gu", "ffn2_w_down"]
    gathered = {n: _all_gather(f"ag_{n}", w_of[n][0].astype(BF16)) for n in big}
    wgu1, wgu2 = gathered["ffn1_w_gu"], gathered["ffn2_w_gu"]
    wd1 = gathered["ffn1_w_down"].reshape(N_CHIP, -1, d)
    wd2 = gathered["ffn2_w_down"].reshape(N_CHIP, -1, d)
    win_all, wout_all = gathered["w_in"], gathered["w_out"]

    wcol = d // N_DEV
    conv_rows = 8
    small_local = jnp.concatenate([
        meta_tokens,
        pool_w[0].reshape(N_POOL_GROUPS * (pg // N_DEV), pg),
        jnp.pad(conv_w[0], ((0, conv_rows - CONV_K), (0, wcol - conv_w.shape[2]))),
    ], axis=0)
    small_all = _all_gather("ag_small", small_local)
    r0, r1 = N_META, N_META + N_POOL_GROUPS * (pg // N_DEV)
    meta_full = jnp.transpose(small_all[:, :r0], (1, 0, 2)).reshape(N_META, d)
    pool_w_full = jnp.transpose(small_all[:, r0:r1].reshape(N_DEV, N_POOL_GROUPS, pg // N_DEV, pg), (1, 0, 2, 3)).reshape(N_POOL_GROUPS, pg, pg)
    conv_w_full = jnp.transpose(small_all[:, r1:r1 + CONV_K, :conv_w.shape[2]], (1, 0, 2)).reshape(CONV_K, c_conv)
    pool_w_b = pool_w_full.astype(BF16)

    h0 = jnp.concatenate([jnp.broadcast_to(meta_full[None], (n_seq, N_META, d)), x], axis=1).reshape(n_rows, d)
    h0 = _pad_rows(h0, tp)
    h0_b = h0.astype(BF16)
    tgt = _pad_rows(jnp.pad(loss_target, ((0, 0), (N_META, 0), (0, 0))).reshape(n_rows, d), tp)

    gu1, act1 = _ffn_gu("ffn1_gu", h0_b, wgu1)
    pre1 = _ffn_down("ffn1_down", act1, wd1, h0)
    h1, h1_b = _ln_fwd("ln1", pre1, ln1_g, ln1_b)

    u = _proj_in("mix_in", h1_b, win_all)
    y_conv = _conv_fwd("mix_conv", u, conv_w_full, n_rows, seq_len)
    dpool = _pool_fwd("mix_pool", u, 3 * c_conv, p_pool, pg, n_rows, seq_len)
    ypre, y_pool = _pool_mix("mix_pool_w", dpool, pool_w_b, pool_scale)
    y_mix = jnp.concatenate([y_conv, y_pool], axis=1)
    pre2 = _proj_out("mix_out", y_mix, wout_all, h1)
    h2, h2_b = _ln_fwd("ln2", pre2, ln2_g, ln2_b)

    gu2, act2 = _ffn_gu("ffn2_gu", h2_b, wgu2)
    pre3 = _ffn_down("ffn2_down", act2, wd2, h2)

    dpre3, dpre3_b, d_ln3_g, d_ln3_b, sq = _ln_loss_bwd("ln3_loss", pre3, tgt, ln3_g, ln3_b, n_rows, seq_len)
    loss = lax.psum(0.5 * jnp.sum(sq) / d, ("x", "y", "c"))

    dgu2 = _ffn_bwd_dgu("ffn2_bwd_dgu", dpre3_b, wd2, gu2)
    g_wd2 = _ffn_bwd_wd("ffn2_bwd_wd", act2, dpre3_b)
    g_wgu2 = _ffn_bwd_wgu("ffn2_bwd_wgu", h2_b, dgu2)
    dh2 = _ffn_bwd_dh("ffn2_bwd_dh", dgu2, wgu2, dpre3)
    dpre2, dpre2_b, d_ln2_g, d_ln2_b = _ln_bwd("ln2_bwd", pre2, dh2, ln2_g)

    dy_mix = _proj_out_bwd_y("mix_out_bwd_y", dpre2_b, wout_all)
    g_wout = _proj_out_bwd_w("mix_out_bwd_w", y_mix, dpre2_b, N_DEV)
    dyps, d_pool_scale = _pool_scale_bwd("mix_pool_scale_bwd", dy_mix, 1, ypre, pool_scale)
    dd = _pool_mix_bwd_in("mix_pool_w_bwd_in", dyps, pool_w_b)
    d_pool_w = _pool_mix_bwd_w("mix_pool_w_bwd_w", dpool, dyps, pg)
    du_pool = _pool_bwd("mix_pool_bwd", dd, pg, n_rows, seq_len)
    du_b, du_c, du_x, d_conv_w = _conv_bwd("mix_conv_bwd", u, conv_w_full, dy_mix, n_rows, seq_len)
    du = jnp.concatenate([du_b, du_c, du_x, du_pool], axis=1)
    g_win = _proj_in_bwd_w("mix_in_bwd_w", h1_b, du, N_DEV)
    dh1 = _proj_in_bwd_h("mix_in_bwd_h", du, win_all, dpre2)
    dpre1, dpre1_b, d_ln1_g, d_ln1_b = _ln_bwd("ln1_bwd", pre1, dh1, ln1_g)

    dgu1 = _ffn_bwd_dgu("ffn1_bwd_dgu", dpre1_b, wd1, gu1)
    g_wd1 = _ffn_bwd_wd("ffn1_bwd_wd", act1, dpre1_b)
    g_wgu1 = _ffn_bwd_wgu("ffn1_bwd_wgu", h0_b, dgu1)
    dh0 = _ffn_bwd_dh("ffn1_bwd_dh", dgu1, wgu1, dpre1)

    dh0_seq = dh0[:n_rows].reshape(n_seq, seq_len, d)
    grad_x = dh0_seq[:, N_META:]
    d_meta = jnp.sum(dh0_seq[:, :N_META], axis=0)

    parts = {"ffn1_w_gu": g_wgu1, "ffn1_w_down": g_wd1.reshape(N_DEV, -1, d), "w_in": g_win, "w_out": g_wout,
             "ffn2_w_gu": g_wgu2, "ffn2_w_down": g_wd2.reshape(N_DEV, -1, d)}
    grads, deltas, new_m, new_v = {}, {}, {}, {}
    for n in big:
        g, dl, mm, vv = _reduce_scatter_adamw(n, parts[n], core, chip, w_of[n][0], m_of[n][0], v_of[n][0])
        grads[n], deltas[n], new_m[n], new_v[n] = g[None], dl[None], mm[None], vv[None]

    def widen(a):
        return jnp.pad(a, ((0, 0), (0, d - a.shape[1])))

    small_part = jnp.concatenate([
        d_ln1_g, d_ln1_b, d_ln2_g, d_ln2_b, d_ln3_g, d_ln3_b, widen(d_pool_scale), widen(d_conv_w), d_meta,
        d_pool_w.reshape(-1, d)], axis=0)
    n_small_rows = small_part.shape[0]
    small_part = _pad_rows(small_part, -(-n_small_rows // 8) * 8)
    small_sum = _sum_parts("small_sum", _all_gather("ag_small_grads", small_part))
    o = 7 + CONV_K
    g_small = {
        "ln1_g": small_sum[0:1], "ln1_b": small_sum[1:2], "ln2_g": small_sum[2:3], "ln2_b": small_sum[3:4],
        "ln3_g": small_sum[4:5], "ln3_b": small_sum[5:6], "pool_scale": small_sum[6:7, :p_pool],
        "conv_w": lax.dynamic_slice_in_dim(small_sum[7:o, :c_conv], dev * (c_conv // N_DEV), c_conv // N_DEV, axis=1)[None],
        "meta_tokens": lax.dynamic_slice_in_dim(small_sum[o:o + N_META], dev * wcol, wcol, axis=1),
        "pool_w": lax.dynamic_slice_in_dim(small_sum[o + N_META:n_small_rows].reshape(N_POOL_GROUPS, pg, pg),
                                           dev * (pg // N_DEV), pg // N_DEV, axis=1)[None],
    }
    small = ["meta_tokens", "ln1_g", "ln1_b", "conv_w", "pool_w", "pool_scale", "ln2_g", "ln2_b", "ln3_g", "ln3_b"]

    def flat(a):
        return a.reshape(-1, a.shape[-1])

    outs = _adamw_small("adamw_small", [flat(w_of[n]) for n in small], [flat(g_small[n]) for n in small],
                        [flat(m_of[n]) for n in small], [flat(v_of[n]) for n in small])
    ns = len(small)
    for i, n in enumerate(small):
        shape = w_of[n].shape
        grads[n] = g_small[n].reshape(shape)
        deltas[n], new_m[n], new_v[n] = outs[i].reshape(shape), outs[ns + i].reshape(shape), outs[2 * ns + i].reshape(shape)

    return (loss, grad_x, *[grads[n] for n in names], *[deltas[n] for n in names],
            *[new_m[n] for n in names], *[new_v[n] for n in names])
```

```python
import functools

import jax
import jax.numpy as jnp
from jax import lax
from jax.experimental import pallas as pl
from jax.experimental.pallas import tpu as pltpu

N_DEV = 8
N_CHIP = 4
N_META = 16
CONV_K = 3
POOL_WINDOWS = (2, 4, 8, 16)
N_POOL_GROUPS = len(POOL_WINDOWS)
LN_EPS = 1e-5
DEPTH = 1
ALPHA = (2.0 * DEPTH) ** 0.25
ADAM_LR = 0.001
ADAM_B1 = 0.9
ADAM_B2 = 0.999
ADAM_EPS = 1e-08
ADAM_WD = 0.01
ADAM_STEP = 10

V7X_VMEM_BYTES = 64 * 1024 * 1024
VMEM_LIMIT = V7X_VMEM_BYTES - 6 * 1024 * 1024
LANE = 128
ROW_ALIGN = 3 * LANE
TM_BIG = 1408
TM_WIDE = 704
TK = 512
TN = 1024
TR_LN = 128
ELEM_BLOCK_BYTES = 1 << 20
TC_MIX = LANE
EPILOGUE_ROWS = 64

NN = (((1,), (0,)), ((), ()))
NT = (((1,), (1,)), ((), ()))
TN_DIMS = (((0,), (0,)), ((), ()))
MESH = pl.DeviceIdType.MESH
BF16 = jnp.bfloat16
F32 = jnp.float32
ANY = pl.BlockSpec(memory_space=pl.ANY)
HBM = pl.BlockSpec(memory_space=pltpu.HBM)
SEM = pl.BlockSpec(memory_space=pltpu.SEMAPHORE)
EFFECT = pltpu.SideEffectType.DATAFLOW_SIDE_EFFECTING


def _tile(n, target, mult):
    best = None
    for t in range(mult, min(n, target) + 1, mult):
        if n % t == 0:
            best = t
    return n if best is None else best


def _params(sem):
    return pltpu.CompilerParams(dimension_semantics=sem, vmem_limit_bytes=VMEM_LIMIT)


def _sds(shape, dtype):
    return jax.ShapeDtypeStruct(shape, dtype)


def _row_chunks(n_rows, fn):
    ch = _tile(n_rows, EPILOGUE_ROWS, 16)

    def step(i, carry):
        fn(pl.ds(pl.multiple_of(i * ch, ch), ch))
        return carry

    lax.fori_loop(0, n_rows // ch, step, 0)


def _mm(name, grid, dims, ab, ab_specs, extras, extra_specs, out_shape, out_specs, acc_shape, epilogue, deps=()):
    nk = grid[-1]
    n_extra = len(extras)
    n_in = 2 + n_extra + len(deps)
    n_out = len(out_shape)
    kax = len(grid) - 1

    def body(*refs):
        a_ref, b_ref = refs[0], refs[1]
        ex = refs[2:2 + n_extra]
        outs = refs[n_in:n_in + n_out]
        if nk == 1:
            epilogue(lax.dot_general(a_ref[...], b_ref[...], dims, preferred_element_type=F32), ex, outs, slice(None))
            return
        acc = refs[-1]
        k = pl.program_id(kax)

        @pl.when(k == 0)
        def _():
            acc[...] = jnp.zeros_like(acc)

        acc[...] += lax.dot_general(a_ref[...], b_ref[...], dims, preferred_element_type=F32)

        @pl.when(k == nk - 1)
        def _():
            _row_chunks(acc_shape[0], lambda rows: epilogue(acc[rows, :], ex, outs, rows))

    scratch = [] if nk == 1 else [pltpu.VMEM(acc_shape, F32)]
    sem = ("parallel",) * kax + ("arbitrary",)
    return pl.pallas_call(
        body, name=name, grid=grid, in_specs=list(ab_specs) + list(extra_specs) + [ANY] * len(deps), out_specs=list(out_specs),
        out_shape=list(out_shape), scratch_shapes=scratch, compiler_params=_params(sem),
    )(*ab, *extras, *deps)


def _silu_parts(g):
    s = 1.0 / (1.0 + jnp.exp(-g))
    return s, g * s


def _ffn_gu(name, h_b, wgu_all):
    tp, d = h_b.shape
    ns, _, ng = wgu_all.shape
    half = ns // 2
    tm, tk = _tile(tp, TM_WIDE, 16), _tile(d, TK, LANE)
    grid = (tp // tm, half, d // tk)
    nk = grid[-1]

    def body(h_ref, wg_ref, wu_ref, gu_ref, act_ref, acc_g, acc_u):
        k = pl.program_id(2)

        @pl.when(k == 0)
        def _():
            acc_g[...] = jnp.zeros_like(acc_g)
            acc_u[...] = jnp.zeros_like(acc_u)

        acc_g[...] += jnp.dot(h_ref[...], wg_ref[...], preferred_element_type=F32)
        acc_u[...] += jnp.dot(h_ref[...], wu_ref[...], preferred_element_type=F32)

        @pl.when(k == nk - 1)
        def _():
            def finish(rows):
                g = acc_g[rows, :]
                u = acc_u[rows, :]
                _, silu = _silu_parts(g)
                gu_ref[0, rows, :] = g.astype(BF16)
                gu_ref[1, rows, :] = u.astype(BF16)
                act_ref[rows, :] = (silu * u).astype(BF16)

            _row_chunks(tm, finish)

    return pl.pallas_call(
        body, name=name, grid=grid,
        in_specs=[pl.BlockSpec((tm, tk), lambda m, s, k: (m, k)),
                  pl.BlockSpec((None, tk, ng), lambda m, s, k: (s, k, 0)),
                  pl.BlockSpec((None, tk, ng), lambda m, s, k: (s + half, k, 0))],
        out_specs=[pl.BlockSpec((None, 2, tm, ng), lambda m, s, k: (s, 0, m, 0)),
                   pl.BlockSpec((None, tm, ng), lambda m, s, k: (s, m, 0))],
        out_shape=[_sds((half, 2, tp, ng), BF16), _sds((half, tp, ng), BF16)],
        scratch_shapes=[pltpu.VMEM((tm, ng), F32), pltpu.VMEM((tm, ng), F32)],
        compiler_params=_params(("parallel", "parallel", "arbitrary")),
    )(h_b, wgu_all, wgu_all)


def _ffn_down(name, act, wd4, h):
    ns, tp, ng = act.shape
    d = wd4.shape[2]
    tm, tn = _tile(tp, TM_WIDE, 16), _tile(d, TN, LANE)

    def epi(acc, ex, outs, rows):
        outs[0][rows, :] = ALPHA * ex[0][rows, :] + 0.5 * acc

    return _mm(name, (tp // tm, d // tn, ns), NN, (act, wd4),
               [pl.BlockSpec((None, tm, ng), lambda m, n, s: (s, m, 0)),
                pl.BlockSpec((None, ng, tn), lambda m, n, s: (s, 0, n))],
               (h,), [pl.BlockSpec((tm, tn), lambda m, n, s: (m, n))],
               [_sds((tp, d), F32)], [pl.BlockSpec((tm, tn), lambda m, n, s: (m, n))], (tm, tn), epi)[0]


def _ffn_bwd_dgu(name, dp_b, wd4, gu):
    tp, d = dp_b.shape
    ns, ng, _ = wd4.shape
    tm, tk = _tile(tp, TM_WIDE, 16), _tile(d, TK, LANE)

    def epi(acc, ex, outs, rows):
        g = ex[0][0, rows, :].astype(F32)
        u = ex[0][1, rows, :].astype(F32)
        da = 0.5 * acc
        s, silu = _silu_parts(g)
        outs[0][0, rows, :] = (da * u * (s + silu * (1.0 - s))).astype(BF16)
        outs[0][1, rows, :] = (da * silu).astype(BF16)

    return _mm(name, (tp // tm, ns, d // tk), NT, (dp_b, wd4),
               [pl.BlockSpec((tm, tk), lambda m, s, k: (m, k)),
                pl.BlockSpec((None, ng, tk), lambda m, s, k: (s, 0, k))],
               (gu,), [pl.BlockSpec((None, 2, tm, ng), lambda m, s, k: (s, 0, m, 0))],
               [_sds((ns, 2, tp, ng), BF16)], [pl.BlockSpec((None, 2, tm, ng), lambda m, s, k: (s, 0, m, 0))],
               (tm, ng), epi)[0]


def _ffn_bwd_wd(name, act, dp_b):
    ns, tp, ng = act.shape
    d = dp_b.shape[1]
    tkt, tn = _tile(tp, TM_WIDE, LANE), _tile(d, TN, LANE)

    def epi(acc, ex, outs, rows):
        outs[0][rows, :] = (0.5 * acc).astype(BF16)

    return _mm(name, (ns, d // tn, tp // tkt), TN_DIMS, (act, dp_b),
               [pl.BlockSpec((None, tkt, ng), lambda s, n, t: (s, t, 0)),
                pl.BlockSpec((tkt, tn), lambda s, n, t: (t, n))],
               (), [], [_sds((ns, ng, d), BF16)], [pl.BlockSpec((None, ng, tn), lambda s, n, t: (s, 0, n))],
               (ng, tn), epi)[0]


def _ffn_bwd_wgu(name, h_b, dgu, deps=()):
    tp, d = h_b.shape
    ns, _, _, ng = dgu.shape
    tkt, tmd = _tile(tp, TM_WIDE, LANE), _tile(d, TN, LANE)

    def epi(acc, ex, outs, rows):
        outs[0][rows, :] = acc.astype(BF16)

    return _mm(name, (ns, 2, d // tmd, tp // tkt), TN_DIMS, (h_b, dgu),
               [pl.BlockSpec((tkt, tmd), lambda s, j, i, t: (t, i)),
                pl.BlockSpec((None, None, tkt, ng), lambda s, j, i, t: (s, j, t, 0))],
               (), [], [_sds((2 * ns, d, ng), BF16)],
               [pl.BlockSpec((None, tmd, ng), lambda s, j, i, t: (j * ns + s, i, 0))], (tmd, ng), epi, deps)[0]


def _ffn_bwd_dh(name, dgu, wgu_all, dp, deps=()):
    ns, _, tp, ng = dgu.shape
    d = wgu_all.shape[1]
    tm, tn = _tile(tp, TM_WIDE, 16), _tile(d, TN, LANE)

    def epi(acc, ex, outs, rows):
        outs[0][rows, :] = ALPHA * ex[0][rows, :] + acc

    return _mm(name, (tp // tm, d // tn, 2 * ns), NT, (dgu, wgu_all),
               [pl.BlockSpec((None, None, tm, ng), lambda m, n, j: (j % ns, j // ns, m, 0)),
                pl.BlockSpec((None, tn, ng), lambda m, n, j: (j, n, 0))],
               (dp,), [pl.BlockSpec((tm, tn), lambda m, n, j: (m, n))],
               [_sds((tp, d), F32)], [pl.BlockSpec((tm, tn), lambda m, n, j: (m, n))], (tm, tn), epi, deps)[0]


def _ln_stats(x):
    mu = jnp.mean(x, axis=-1, keepdims=True)
    xc = x - mu
    var = jnp.mean(xc * xc, axis=-1, keepdims=True)
    rstd = lax.rsqrt(var + LN_EPS)
    return xc * rstd, rstd


def _ln_bwd_rows(dy, xhat, rstd, g):
    dxh = dy * g
    m1 = jnp.mean(dxh, axis=-1, keepdims=True)
    m2 = jnp.mean(dxh * xhat, axis=-1, keepdims=True)
    return rstd * (dxh - m1 - xhat * m2)


def _ln_fwd(name, pre, g, b):
    tp, d = pre.shape
    tr = _tile(tp, TR_LN, 16)

    def body(x_ref, g_ref, b_ref, y_ref, yb_ref):
        xhat, _ = _ln_stats(x_ref[...])
        y = xhat * g_ref[...] + b_ref[...]
        y_ref[...] = y
        yb_ref[...] = y.astype(BF16)

    row = pl.BlockSpec((tr, d), lambda i: (i, 0))
    vec = pl.BlockSpec((1, d), lambda i: (0, 0))
    return pl.pallas_call(body, name=name, grid=(tp // tr,), in_specs=[row, vec, vec], out_specs=[row, row],
                          out_shape=[_sds((tp, d), F32), _sds((tp, d), BF16)],
                          compiler_params=_params(("parallel",)))(pre, g, b)


def _accumulate(i, ref, val):
    @pl.when(i == 0)
    def _():
        ref[...] = val

    @pl.when(i > 0)
    def _():
        ref[...] += val


def _ln_bwd(name, pre, dy, g):
    tp, d = pre.shape
    tr = _tile(tp, TR_LN, 16)

    def body(x_ref, dy_ref, g_ref, dx_ref, dxb_ref, dg_ref, db_ref):
        i = pl.program_id(0)
        xhat, rstd = _ln_stats(x_ref[...])
        dy = dy_ref[...]
        dx = _ln_bwd_rows(dy, xhat, rstd, g_ref[...])
        dx_ref[...] = dx
        dxb_ref[...] = dx.astype(BF16)
        _accumulate(i, dg_ref, jnp.sum(dy * xhat, axis=0, keepdims=True))
        _accumulate(i, db_ref, jnp.sum(dy, axis=0, keepdims=True))

    row = pl.BlockSpec((tr, d), lambda i: (i, 0))
    vec = pl.BlockSpec((1, d), lambda i: (0, 0))
    return pl.pallas_call(body, name=name, grid=(tp // tr,), in_specs=[row, row, vec], out_specs=[row, row, vec, vec],
                          out_shape=[_sds((tp, d), F32), _sds((tp, d), BF16), _sds((1, d), F32), _sds((1, d), F32)],
                          compiler_params=_params(("arbitrary",)))(pre, dy, g)


def _ln_loss_bwd(name, pre, tgt, g, b, n_rows, seq_len):
    tp, d = pre.shape
    tr = _tile(tp, TR_LN, 16)
    n_seq = n_rows // seq_len

    def body(x_ref, t_ref, g_ref, b_ref, dx_ref, dxb_ref, dg_ref, db_ref, sq_ref):
        i = pl.program_id(0)
        xhat, rstd = _ln_stats(x_ref[...])
        gain = g_ref[...]
        y = xhat * gain + b_ref[...]
        r = i * tr + lax.broadcasted_iota(jnp.int32, (tr, 1), 0)
        pos = r
        for s in range(1, n_seq):
            pos = jnp.where(r >= s * seq_len, r - s * seq_len, pos)
        live = jnp.logical_and(r < n_rows, pos >= N_META)
        err = jnp.where(live, y - t_ref[...], 0.0)
        dy = err * (1.0 / d)
        dx = _ln_bwd_rows(dy, xhat, rstd, gain)
        dx_ref[...] = dx
        dxb_ref[...] = dx.astype(BF16)
        _accumulate(i, dg_ref, jnp.sum(dy * xhat, axis=0, keepdims=True))
        _accumulate(i, db_ref, jnp.sum(dy, axis=0, keepdims=True))
        _accumulate(i, sq_ref, jnp.sum(err * err, axis=0, keepdims=True))

    row = pl.BlockSpec((tr, d), lambda i: (i, 0))
    vec = pl.BlockSpec((1, d), lambda i: (0, 0))
    return pl.pallas_call(
        body, name=name, grid=(tp // tr,), in_specs=[row, row, vec, vec], out_specs=[row, row, vec, vec, vec],
        out_shape=[_sds((tp, d), F32), _sds((tp, d), BF16), _sds((1, d), F32), _sds((1, d), F32), _sds((1, d), F32)],
        compiler_params=_params(("arbitrary",)))(pre, tgt, g, b)


def _proj_in(name, h_b, win_all):
    tp, d = h_b.shape
    ns, _, ni = win_all.shape
    tm, tk = _tile(tp, TM_BIG, 16), _tile(d, TK, LANE)

    def epi(acc, ex, outs, rows):
        outs[0][rows, :] = acc

    return _mm(name, (tp // tm, ns, d // tk), NN, (h_b, win_all),
               [pl.BlockSpec((tm, tk), lambda m, j, k: (m, k)),
                pl.BlockSpec((None, tk, ni), lambda m, j, k: (j, k, 0))],
               (), [], [_sds((tp, ns * ni), F32)], [pl.BlockSpec((tm, ni), lambda m, j, k: (m, j))], (tm, ni), epi)[0]


def _positions(tp, n_rows, seq_len):
    r = lax.broadcasted_iota(jnp.int32, (tp, 1), 0)
    pos = r
    for s in range(1, n_rows // seq_len):
        pos = jnp.where(r >= s * seq_len, r - s * seq_len, pos)
    return pos


def _shift_down(x, s, pos):
    return jnp.where(pos >= s, pltpu.roll(x, s, 0), 0.0)


def _shift_up(x, s, pos, seq_len):
    return jnp.where(pos + s < seq_len, pltpu.roll(x, x.shape[0] - s, 0), 0.0)


def _conv_fwd(name, u, conv_w, n_rows, seq_len):
    tp = u.shape[0]
    c = conv_w.shape[1]
    tc = _tile(c, TC_MIX, LANE)
    nb = c // tc

    def body(gb_ref, gc_ref, xi_ref, w_ref, y_ref):
        pos = _positions(tp, n_rows, seq_len)
        v = gc_ref[...] * xi_ref[...]
        w = w_ref[...]
        y = _shift_down(v, 2, pos) * w[0:1]
        y = y + _shift_down(v, 1, pos) * w[1:2]
        y = y + v * w[2:3]
        y_ref[...] = (gb_ref[...] * y).astype(BF16)

    col = lambda off: pl.BlockSpec((tp, tc), lambda i: (0, off + i))
    return pl.pallas_call(body, name=name, grid=(nb,), in_specs=[col(0), col(nb), col(2 * nb), pl.BlockSpec((CONV_K, tc), lambda i: (0, i))],
                          out_specs=pl.BlockSpec((tp, tc), lambda i: (0, i)), out_shape=_sds((tp, c), BF16),
                          compiler_params=_params(("parallel",)))(u, u, u, conv_w)


def _conv_bwd(name, u, conv_w, dy, n_rows, seq_len):
    tp = u.shape[0]
    c = conv_w.shape[1]
    tc = _tile(c, TC_MIX, LANE)
    nb = c // tc

    def body(gb_ref, gc_ref, xi_ref, w_ref, dy_ref, dgb_ref, dgc_ref, dxi_ref, dw_ref):
        pos = _positions(tp, n_rows, seq_len)
        gc, xi = gc_ref[...], xi_ref[...]
        v = gc * xi
        w = w_ref[...]
        v2, v1 = _shift_down(v, 2, pos), _shift_down(v, 1, pos)
        conv = v2 * w[0:1]
        conv = conv + v1 * w[1:2]
        conv = conv + v * w[2:3]
        dyc = dy_ref[...]
        dgb_ref[...] = (dyc * conv).astype(BF16)
        dconv = dyc * gb_ref[...]
        dv = dconv * w[2:3] + _shift_up(dconv, 1, pos, seq_len) * w[1:2] + _shift_up(dconv, 2, pos, seq_len) * w[0:1]
        dgc_ref[...] = (dv * xi).astype(BF16)
        dxi_ref[...] = (dv * gc).astype(BF16)
        dw_ref[0:1, :] = jnp.sum(dconv * v2, axis=0, keepdims=True)
        dw_ref[1:2, :] = jnp.sum(dconv * v1, axis=0, keepdims=True)
        dw_ref[2:3, :] = jnp.sum(dconv * v, axis=0, keepdims=True)

    col = lambda off: pl.BlockSpec((tp, tc), lambda i: (0, off + i))
    wspec = pl.BlockSpec((CONV_K, tc), lambda i: (0, i))
    return pl.pallas_call(body, name=name, grid=(nb,), in_specs=[col(0), col(nb), col(2 * nb), wspec, col(0)],
                          out_specs=[col(0), col(0), col(0), wspec],
                          out_shape=[_sds((tp, c), BF16)] * 3 + [_sds((CONV_K, c), F32)],
                          compiler_params=_params(("parallel",)))(u, u, u, conv_w, dy)


def _window_select(group, parts):
    out = parts[-1]
    for gi in range(len(parts) - 2, -1, -1):
        out = jnp.where(group == gi, parts[gi], out)
    return out


def _pool_fwd(name, u, col0, p, pg, n_rows, seq_len):
    tp = u.shape[0]
    tc = _tile(pg, TC_MIX, LANE)
    per_group = pg // tc

    def body(z_ref, d_ref):
        group = pl.program_id(0) // per_group
        pos = _positions(tp, n_rows, seq_len)
        z = z_ref[...]
        sums, s, w = [], z, 1
        for _ in POOL_WINDOWS:
            s = s + _shift_down(s, w, pos)
            w *= 2
            sums.append(s)
        total = _window_select(group, sums)
        count = jnp.minimum(pos + 1, 2 << group).astype(F32)
        d_ref[...] = (total / count - z).astype(BF16)

    return pl.pallas_call(body, name=name, grid=(p // tc,), in_specs=[pl.BlockSpec((tp, tc), lambda i: (0, col0 // tc + i))],
                          out_specs=pl.BlockSpec((tp, tc), lambda i: (0, i)), out_shape=_sds((tp, p), BF16),
                          compiler_params=_params(("parallel",)))(u)


def _pool_bwd(name, dd, pg, n_rows, seq_len):
    tp, p = dd.shape
    tc = _tile(pg, TC_MIX, LANE)
    per_group = pg // tc

    def body(dd_ref, dz_ref):
        group = pl.program_id(0) // per_group
        pos = _positions(tp, n_rows, seq_len)
        dd_v = dd_ref[...]
        count = jnp.minimum(pos + 1, 2 << group).astype(F32)
        sums, s, w = [], dd_v / count, 1
        for _ in POOL_WINDOWS:
            s = s + _shift_up(s, w, pos, seq_len)
            w *= 2
            sums.append(s)
        dz_ref[...] = (_window_select(group, sums) - dd_v).astype(BF16)

    spec = pl.BlockSpec((tp, tc), lambda i: (0, i))
    return pl.pallas_call(body, name=name, grid=(p // tc,), in_specs=[spec], out_specs=spec, out_shape=_sds((tp, p), BF16),
                          compiler_params=_params(("parallel",)))(dd)


def _pool_mix(name, dpool, pool_w_b, scale):
    tp, p = dpool.shape
    ng, pg, _ = pool_w_b.shape
    tm = _tile(tp, TM_BIG, 16)

    def epi(acc, ex, outs, rows):
        outs[0][rows, :] = acc
        outs[1][rows, :] = (acc * ex[0][...]).astype(BF16)

    blk = pl.BlockSpec((tm, pg), lambda m, g, k: (m, g))
    return _mm(name, (tp // tm, ng, 1), NN, (dpool, pool_w_b), [blk, pl.BlockSpec((None, pg, pg), lambda m, g, k: (g, 0, 0))],
               (scale,), [pl.BlockSpec((1, pg), lambda m, g, k: (0, g))],
               [_sds((tp, p), F32), _sds((tp, p), BF16)], [blk, blk], None, epi)


def _pool_scale_bwd(name, dy, col_block, ypre, scale):
    tp, p = ypre.shape
    tr = _tile(tp, TR_LN, 16)

    def body(dy_ref, yp_ref, s_ref, o_ref, ds_ref):
        i = pl.program_id(0)
        dyp = dy_ref[...]
        o_ref[...] = (dyp * s_ref[...]).astype(BF16)
        _accumulate(i, ds_ref, jnp.sum(dyp * yp_ref[...], axis=0, keepdims=True))

    row = pl.BlockSpec((tr, p), lambda i: (i, 0))
    vec = pl.BlockSpec((1, p), lambda i: (0, 0))
    return pl.pallas_call(body, name=name, grid=(tp // tr,), in_specs=[pl.BlockSpec((tr, p), lambda i: (i, col_block)), row, vec],
                          out_specs=[row, vec], out_shape=[_sds((tp, p), BF16), _sds((1, p), F32)],
                          compiler_params=_params(("arbitrary",)))(dy, ypre, scale)


def _pool_mix_bwd_in(name, dyps, pool_w_b):
    tp, p = dyps.shape
    ng, pg, _ = pool_w_b.shape
    tm = _tile(tp, TM_BIG, 16)

    def epi(acc, ex, outs, rows):
        outs[0][rows, :] = acc

    blk = pl.BlockSpec((tm, pg), lambda m, g, k: (m, g))
    return _mm(name, (tp // tm, ng, 1), NT, (dyps, pool_w_b), [blk, pl.BlockSpec((None, pg, pg), lambda m, g, k: (g, 0, 0))],
               (), [], [_sds((tp, p), F32)], [blk], None, epi)[0]


def _pool_mix_bwd_w(name, dpool, dyps, pg):
    tp, p = dpool.shape
    ng = p // pg
    tkt = _tile(tp, TM_BIG, LANE)

    def epi(acc, ex, outs, rows):
        outs[0][rows, :] = acc

    blk = pl.BlockSpec((tkt, pg), lambda g, t: (t, g))
    return _mm(name, (ng, tp // tkt), TN_DIMS, (dpool, dyps), [blk, blk], (), [],
               [_sds((ng, pg, pg), F32)], [pl.BlockSpec((None, pg, pg), lambda g, t: (g, 0, 0))], (pg, pg), epi)[0]


def _proj_out(name, y, wout_all, h):
    tp = y.shape[0]
    ns, ro, d = wout_all.shape
    tm, tn = _tile(tp, TM_BIG, 16), _tile(d, TN, LANE)

    def epi(acc, ex, outs, rows):
        outs[0][rows, :] = ALPHA * ex[0][rows, :] + acc

    mn = pl.BlockSpec((tm, tn), lambda m, n, j: (m, n))
    return _mm(name, (tp // tm, d // tn, ns), NN, (y, wout_all),
               [pl.BlockSpec((tm, ro), lambda m, n, j: (m, j)), pl.BlockSpec((None, ro, tn), lambda m, n, j: (j, 0, n))],
               (h,), [mn], [_sds((tp, d), F32)], [mn], (tm, tn), epi)[0]


def _proj_out_bwd_y(name, dp_b, wout_all):
    tp, d = dp_b.shape
    ns, ro, _ = wout_all.shape
    tm, tk = _tile(tp, TM_BIG, 16), _tile(d, TK, LANE)

    def epi(acc, ex, outs, rows):
        outs[0][rows, :] = acc

    return _mm(name, (tp // tm, ns, d // tk), NT, (dp_b, wout_all),
               [pl.BlockSpec((tm, tk), lambda m, j, k: (m, k)), pl.BlockSpec((None, ro, tk), lambda m, j, k: (j, 0, k))],
               (), [], [_sds((tp, ns * ro), F32)], [pl.BlockSpec((tm, ro), lambda m, j, k: (m, j))], (tm, ro), epi)[0]


def _proj_out_bwd_w(name, y, dp_b, ns):
    tp, c = y.shape
    d = dp_b.shape[1]
    ro = c // ns
    tkt, tn = _tile(tp, TM_BIG, LANE), _tile(d, 2 * TN, LANE)

    def epi(acc, ex, outs, rows):
        outs[0][rows, :] = acc.astype(BF16)

    return _mm(name, (ns, d // tn, tp // tkt), TN_DIMS, (y, dp_b),
               [pl.BlockSpec((tkt, ro), lambda j, n, t: (t, j)), pl.BlockSpec((tkt, tn), lambda j, n, t: (t, n))],
               (), [], [_sds((ns, ro, d), BF16)], [pl.BlockSpec((None, ro, tn), lambda j, n, t: (j, 0, n))], (ro, tn), epi)[0]


def _proj_in_bwd_w(name, h_b, du, ns, deps=()):
    tp, d = h_b.shape
    ni = du.shape[1] // ns
    tkt, tmd = _tile(tp, TM_WIDE, LANE), _tile(d, 2 * TN, LANE)

    def epi(acc, ex, outs, rows):
        outs[0][rows, :] = acc.astype(BF16)

    return _mm(name, (ns, d // tmd, tp // tkt), TN_DIMS, (h_b, du),
               [pl.BlockSpec((tkt, tmd), lambda j, i, t: (t, i)), pl.BlockSpec((tkt, ni), lambda j, i, t: (t, j))],
               (), [], [_sds((ns, d, ni), BF16)], [pl.BlockSpec((None, tmd, ni), lambda j, i, t: (j, i, 0))], (tmd, ni), epi, deps)[0]


def _proj_in_bwd_h(name, du, win_all, dp, deps=()):
    tp = du.shape[0]
    ns, d, ni = win_all.shape
    tm, tn = _tile(tp, TM_BIG, 16), _tile(d, TN, LANE)

    def epi(acc, ex, outs, rows):
        outs[0][rows, :] = ALPHA * ex[0][rows, :] + acc

    mn = pl.BlockSpec((tm, tn), lambda m, n, j: (m, n))
    return _mm(name, (tp // tm, d // tn, ns), NT, (du, win_all),
               [pl.BlockSpec((tm, ni), lambda m, n, j: (m, j)), pl.BlockSpec((None, tn, ni), lambda m, n, j: (j, n, 0))],
               (dp,), [mn], [_sds((tp, d), F32)], [mn], (tm, tn), epi, deps)[0]


def _place():
    return lax.axis_index("x"), lax.axis_index("y"), lax.axis_index("c")


def _hbm(a):
    return pltpu.with_memory_space_constraint(a, pltpu.HBM)


def _token_shape():
    return _sds((8, LANE), F32)


def _gather_peers(x, y, c):
    return [(x, y, 1 - c), (1 - x, y, c), (x, 1 - y, c), (1 - x, 1 - y, c)]


def _gather_start(name, shards):
    n = len(shards)
    n_peer = N_CHIP

    def body(*refs):
        x_refs, land_refs = refs[:n], refs[n:2 * n]
        send_sems, recv_sems = refs[2 * n:3 * n], refs[3 * n:4 * n]
        token = refs[-1]
        x, y, c = _place()
        me = 4 * x + 2 * y + c
        for i in range(n):
            for k, peer in enumerate(_gather_peers(x, y, c)):
                pltpu.make_async_remote_copy(
                    src_ref=x_refs[i], dst_ref=land_refs[i].at[me], send_sem=send_sems[i].at[k], recv_sem=recv_sems[i].at[k],
                    device_id=peer, device_id_type=MESH).start()
        token[...] = jnp.zeros_like(token)

    lands = [lax.empty((N_DEV,) + s.shape, s.dtype) for s in shards]
    sem = pltpu.SemaphoreType.DMA((n_peer,))
    out = pl.pallas_call(
        body, name=name,
        out_shape=[sem] * (2 * n) + [pltpu.HBM(s.shape, s.dtype) for s in shards]
        + [pltpu.HBM(l.shape, l.dtype) for l in lands] + [_token_shape()],
        in_specs=[HBM] * (2 * n), out_specs=[SEM] * (2 * n) + [HBM] * (2 * n) + [pl.BlockSpec(memory_space=pltpu.VMEM)],
        input_output_aliases={i: 2 * n + i for i in range(2 * n)},
        compiler_params=pltpu.CompilerParams(has_side_effects=EFFECT),
    )(*[_hbm(s) for s in shards], *[_hbm(l) for l in lands])
    per = [(out[i], out[n + i], out[2 * n + i], out[3 * n + i]) for i in range(n)]
    return per, out[-1]


def _gather_wait(name, started, after):
    send_sems, recv_sems, shard, land = started

    def body(x_ref, land_ref, send_ref, recv_ref, after_ref, x_out, land_out):
        x, y, c = _place()
        for k, (px, py, pc) in enumerate(_gather_peers(x, y, c)):
            cp = pltpu.make_async_remote_copy(
                src_ref=x_ref, dst_ref=land_ref.at[4 * px + 2 * py + pc], send_sem=send_ref.at[k], recv_sem=recv_ref.at[k],
                device_id=(px, py, pc), device_id_type=MESH)
            cp.wait_send()
            cp.wait_recv()

    return pl.pallas_call(
        body, name=name, out_shape=(pltpu.HBM(shard.shape, shard.dtype), pltpu.HBM(land.shape, land.dtype)),
        in_specs=(HBM, HBM, SEM, SEM, ANY), out_specs=(HBM, HBM), input_output_aliases={0: 0, 1: 1},
        compiler_params=pltpu.CompilerParams(has_side_effects=EFFECT),
    )(shard, land, send_sems, recv_sems, after)


def _gather_finish(name, land, shard):
    def body(land_ref, x_ref, out_ref, send_sems, recv_sems, local_sem):
        x, y, c = _place()
        chips = [(1 - x, y), (x, 1 - y), (1 - x, 1 - y)]
        mine = pltpu.make_async_copy(x_ref, out_ref.at[4 * x + 2 * y + c], local_sem)
        mine.start()
        copies = []
        for k, (px, py) in enumerate(chips):
            slab = 4 * px + 2 * py + c
            copies.append(pltpu.make_async_remote_copy(
                src_ref=land_ref.at[slab], dst_ref=out_ref.at[slab], send_sem=send_sems.at[k], recv_sem=recv_sems.at[k],
                device_id=(x, y, 1 - c), device_id_type=MESH))
        for cp in copies:
            cp.start()
        for cp in copies:
            cp.wait()
        mine.wait()

    return pl.pallas_call(
        body, name=name, out_shape=_sds(land.shape, land.dtype), in_specs=[ANY, ANY], out_specs=ANY,
        input_output_aliases={0: 0},
        scratch_shapes=[pltpu.SemaphoreType.DMA((N_CHIP - 1,)), pltpu.SemaphoreType.DMA((N_CHIP - 1,)), pltpu.SemaphoreType.DMA(())],
    )(land, shard)


def _chips_start(name, s):
    n_peer = N_CHIP - 1

    def body(s_ref, land_ref, send_sems, recv_sems, s_out, land_out, token):
        x, y, c = _place()
        for k, (px, py) in enumerate([(1 - x, y), (x, 1 - y), (1 - x, 1 - y)]):
            pltpu.make_async_remote_copy(
                src_ref=s_ref.at[2 * px + py], dst_ref=land_ref.at[k], send_sem=send_sems.at[k], recv_sem=recv_sems.at[k],
                device_id=(px, py, c), device_id_type=MESH).start()
        token[...] = jnp.zeros_like(token)

    land = lax.empty((n_peer,) + s.shape[1:], s.dtype)
    sem = pltpu.SemaphoreType.DMA((n_peer,))
    out = pl.pallas_call(
        body, name=name,
        out_shape=[sem, sem, pltpu.HBM(s.shape, s.dtype), pltpu.HBM(land.shape, land.dtype), _token_shape()],
        in_specs=[HBM, HBM], out_specs=[SEM, SEM, HBM, HBM, pl.BlockSpec(memory_space=pltpu.VMEM)],
        input_output_aliases={0: 2, 1: 3}, compiler_params=pltpu.CompilerParams(has_side_effects=EFFECT),
    )(_hbm(s), _hbm(land))
    return out[:4], out[4]


def _chips_wait(name, started, after):
    send_sems, recv_sems, s, land = started

    def body(s_ref, land_ref, send_ref, recv_ref, after_ref, s_out, land_out):
        x, y, c = _place()
        for k, (px, py) in enumerate([(1 - x, y), (x, 1 - y), (1 - x, 1 - y)]):
            cp = pltpu.make_async_remote_copy(
                src_ref=s_ref.at[2 * px + py], dst_ref=land_ref.at[k], send_sem=send_ref.at[k], recv_sem=recv_ref.at[k],
                device_id=(px, py, c), device_id_type=MESH)
            cp.wait_send()
            cp.wait_recv()

    return pl.pallas_call(
        body, name=name, out_shape=(pltpu.HBM(s.shape, s.dtype), pltpu.HBM(land.shape, land.dtype)),
        in_specs=(HBM, HBM, SEM, SEM, ANY), out_specs=(HBM, HBM), input_output_aliases={0: 0, 1: 1},
        compiler_params=pltpu.CompilerParams(has_side_effects=EFFECT),
    )(s, land, send_sems, recv_sems, after)


def _all_gather(name, shard):
    def body(x_ref, out_ref, send_sems, recv_sems, local_sem):
        x, y, c = _place()
        me, sibling = (x, y, c), (x, y, 1 - c)
        chips = [(1 - x, y), (x, 1 - y), (1 - x, 1 - y)]

        def slab(px, py, pc):
            return out_ref.at[4 * px + 2 * py + pc]

        def copy(k, block, to, src=None):
            return pltpu.make_async_remote_copy(
                src_ref=slab(*block) if src is None else src, dst_ref=slab(*block),
                send_sem=send_sems.at[k], recv_sem=recv_sems.at[k], device_id=to, device_id_type=MESH)

        mine = pltpu.make_async_copy(x_ref, slab(*me), local_sem)
        mine.start()
        first = [copy(0, me, sibling, src=x_ref)]
        first += [copy(1 + j, me, (*chip, c), src=x_ref) for j, chip in enumerate(chips)]
        for cp in first:
            cp.start()
        passed = [copy(4 + j, (*chip, c), sibling) for j, chip in enumerate(chips)]
        for j, chip in enumerate(chips):
            copy(1 + j, (*chip, c), me).wait_recv()
            passed[j].start()
        copy(0, sibling, me).wait_recv()
        for j, chip in enumerate(chips):
            copy(4 + j, (*chip, 1 - c), me).wait_recv()
        for cp in first + passed:
            cp.wait_send()
        mine.wait()

    return pl.pallas_call(
        body, name=name, out_shape=_sds((N_DEV,) + shard.shape, shard.dtype), in_specs=[ANY], out_specs=ANY,
        scratch_shapes=[pltpu.SemaphoreType.DMA((N_DEV - 1,)), pltpu.SemaphoreType.DMA((N_DEV - 1,)), pltpu.SemaphoreType.DMA(())],
    )(shard)


def _exchange_sibling(name, part):
    def body(p_ref, r_ref, send_sems, recv_sems):
        x, y, c = _place()
        copies = [pltpu.make_async_remote_copy(
            src_ref=p_ref.at[2 * j + (1 - c)], dst_ref=r_ref.at[j], send_sem=send_sems.at[j], recv_sem=recv_sems.at[j],
            device_id=(x, y, 1 - c), device_id_type=MESH) for j in range(N_CHIP)]
        for cp in copies:
            cp.start()
        for cp in copies:
            cp.wait()

    return pl.pallas_call(
        body, name=name, out_shape=_sds((N_CHIP,) + part.shape[1:], part.dtype), in_specs=[ANY], out_specs=ANY,
        scratch_shapes=[pltpu.SemaphoreType.DMA((N_CHIP,)), pltpu.SemaphoreType.DMA((N_CHIP,))],
    )(part)


def _add_sibling(name, part, recv, core):
    _, r, c = part.shape
    tr = _tile(r, max(16, 2 * ELEM_BLOCK_BYTES // (2 * c)), 16)

    def body(core_ref, p_ref, r_ref, o_ref):
        o_ref[...] = (p_ref[...].astype(F32) + r_ref[...].astype(F32)).astype(BF16)

    blk = pl.BlockSpec((None, tr, c), lambda j, i, core_ref: (j, i, 0))
    return pl.pallas_call(
        body, name=name,
        grid_spec=pltpu.PrefetchScalarGridSpec(
            num_scalar_prefetch=1, grid=(N_CHIP, r // tr),
            in_specs=[pl.BlockSpec((None, tr, c), lambda j, i, core_ref: (2 * j + core_ref[0], i, 0)), blk], out_specs=blk),
        out_shape=_sds((N_CHIP, r, c), BF16), compiler_params=_params(("parallel", "parallel")),
    )(core, part, recv)


def _adamw_math(w, g, m, v):
    m = ADAM_B1 * m + (1.0 - ADAM_B1) * g
    v = ADAM_B2 * v + (1.0 - ADAM_B2) * (g * g)
    m_hat = m / (1.0 - ADAM_B1 ** ADAM_STEP)
    v_hat = v / (1.0 - ADAM_B2 ** ADAM_STEP)
    delta = -ADAM_LR * (m_hat / (jnp.sqrt(v_hat) + ADAM_EPS) + ADAM_WD * w)
    return delta, m, v


def _adamw_big(name, s, recv, chip, w, m, v):
    r, c = w.shape
    tr = _tile(r, max(16, ELEM_BLOCK_BYTES // (4 * c)), 16)

    def body(chip_ref, s_ref, r_ref, w_ref, m_ref, v_ref, g_out, d_out, m_out, v_out):
        g = s_ref[...].astype(F32)
        for k in range(N_CHIP - 1):
            g = g + r_ref[k].astype(F32)
        delta, m_new, v_new = _adamw_math(w_ref[...], g, m_ref[...], v_ref[...])
        g_out[...] = g
        d_out[...] = delta
        m_out[...] = m_new
        v_out[...] = v_new

    blk = pl.BlockSpec((tr, c), lambda i, chip_ref: (i, 0))
    return pl.pallas_call(
        body, name=name,
        grid_spec=pltpu.PrefetchScalarGridSpec(
            num_scalar_prefetch=1, grid=(r // tr,),
            in_specs=[pl.BlockSpec((None, tr, c), lambda i, chip_ref: (chip_ref[0], i, 0)),
                      pl.BlockSpec((N_CHIP - 1, tr, c), lambda i, chip_ref: (0, i, 0)), blk, blk, blk],
            out_specs=[blk, blk, blk, blk]),
        out_shape=[_sds((r, c), F32)] * 4, compiler_params=_params(("parallel",)),
    )(chip, s, recv, w, m, v)


def _sum_parts(name, parts):
    n, r, c = parts.shape
    tr = _tile(r, max(8, ELEM_BLOCK_BYTES // (4 * c)), 8)

    def body(p_ref, o_ref):
        acc = p_ref[0]
        for k in range(1, n):
            acc = acc + p_ref[k]
        o_ref[...] = acc

    return pl.pallas_call(body, name=name, grid=(r // tr,), in_specs=[pl.BlockSpec((n, tr, c), lambda i: (0, i, 0))],
                          out_specs=pl.BlockSpec((tr, c), lambda i: (i, 0)), out_shape=_sds((r, c), F32),
                          compiler_params=_params(("parallel",)))(parts)


def _adamw_small(name, ws, gs, ms, vs):
    n = len(ws)

    def body(*refs):
        ins, outs = refs[:4 * n], refs[4 * n:]
        for i in range(n):
            delta, m_new, v_new = _adamw_math(ins[i][...], ins[n + i][...], ins[2 * n + i][...], ins[3 * n + i][...])
            outs[i][...] = delta
            outs[n + i][...] = m_new
            outs[2 * n + i][...] = v_new

    shapes = [_sds(w.shape, F32) for w in ws]
    return pl.pallas_call(body, name=name, out_shape=shapes * 3)(*ws, *gs, *ms, *vs)


def _reduce_start(tag, part, core):
    from_sibling = _exchange_sibling(f"rs_sibling_{tag}", part)
    summed = _add_sibling(f"rs_add_{tag}", part, from_sibling, core)
    return _chips_start(f"rs_chips_start_{tag}", summed)


def _pad_rows(a, rows):
    return jnp.pad(a, ((0, rows - a.shape[0]), (0, 0)))


def kernel(x, meta_tokens, ffn1_w_gu, ffn1_w_down, ln1_g, ln1_b, w_in, conv_w, pool_w, pool_scale, w_out, ln2_g, ln2_b, ffn2_w_gu, ffn2_w_down, ln3_g, ln3_b, loss_target, m_meta_tokens, m_ffn1_w_gu, m_ffn1_w_down, m_ln1_g, m_ln1_b, m_w_in, m_conv_w, m_pool_w, m_pool_scale, m_w_out, m_ln2_g, m_ln2_b, m_ffn2_w_gu, m_ffn2_w_down, m_ln3_g, m_ln3_b, v_meta_tokens, v_ffn1_w_gu, v_ffn1_w_down, v_ln1_g, v_ln1_b, v_w_in, v_conv_w, v_pool_w, v_pool_scale, v_w_out, v_ln2_g, v_ln2_b, v_ffn2_w_gu, v_ffn2_w_down, v_ln3_g, v_ln3_b):
    names = ["meta_tokens", "ffn1_w_gu", "ffn1_w_down", "ln1_g", "ln1_b", "w_in", "conv_w", "pool_w", "pool_scale", "w_out",
             "ln2_g", "ln2_b", "ffn2_w_gu", "ffn2_w_down", "ln3_g", "ln3_b"]
    w_of = dict(zip(names, [meta_tokens, ffn1_w_gu, ffn1_w_down, ln1_g, ln1_b, w_in, conv_w, pool_w, pool_scale, w_out,
                            ln2_g, ln2_b, ffn2_w_gu, ffn2_w_down, ln3_g, ln3_b]))
    m_of = dict(zip(names, [m_meta_tokens, m_ffn1_w_gu, m_ffn1_w_down, m_ln1_g, m_ln1_b, m_w_in, m_conv_w, m_pool_w, m_pool_scale,
                            m_w_out, m_ln2_g, m_ln2_b, m_ffn2_w_gu, m_ffn2_w_down, m_ln3_g, m_ln3_b]))
    v_of = dict(zip(names, [v_meta_tokens, v_ffn1_w_gu, v_ffn1_w_down, v_ln1_g, v_ln1_b, v_w_in, v_conv_w, v_pool_w, v_pool_scale,
                            v_w_out, v_ln2_g, v_ln2_b, v_ffn2_w_gu, v_ffn2_w_down, v_ln3_g, v_ln3_b]))

    n_seq, seq, d = x.shape
    seq_len = seq + N_META
    n_rows = n_seq * seq_len
    tp = -(-n_rows // ROW_ALIGN) * ROW_ALIGN
    c_conv = conv_w.shape[2] * N_DEV
    p_pool = pool_scale.shape[1]
    pg = pool_w.shape[3]
    assert c_conv == p_pool and p_pool == N_POOL_GROUPS * pg and POOL_WINDOWS == tuple(2 << g for g in range(N_POOL_GROUPS))
    assert (N_POOL_GROUPS * pg * pg) % d == 0 and pg % LANE == 0

    xi, yi, ci = _place()
    dev = 4 * xi + 2 * yi + ci
    core = jnp.reshape(ci, (1,)).astype(jnp.int32)
    chip = jnp.reshape(2 * xi + yi, (1,)).astype(jnp.int32)

    big = ["ffn1_w_gu", "ffn1_w_down", "w_in", "w_out", "ffn2_w_gu", "ffn2_w_down"]
    wcol = d // N_DEV
    conv_rows = 8
    small_local = jnp.concatenate([
        meta_tokens,
        pool_w[0].reshape(N_POOL_GROUPS * (pg // N_DEV), pg),
        jnp.pad(conv_w[0], ((0, conv_rows - CONV_K), (0, wcol - conv_w.shape[2]))),
    ], axis=0)
    started, gather_token = _gather_start("ag_start", [small_local] + [w_of[n][0].astype(BF16) for n in big])
    started = dict(zip(["small"] + big, started))

    def gathered(n, after):
        shard, land = _gather_wait(f"ag_wait_{n}", started[n], after)
        return _gather_finish(f"ag_finish_{n}", land, shard)

    small_all = gathered("small", gather_token)
    r0, r1 = N_META, N_META + N_POOL_GROUPS * (pg // N_DEV)
    meta_full = jnp.transpose(small_all[:, :r0], (1, 0, 2)).reshape(N_META, d)
    pool_w_full = jnp.transpose(small_all[:, r0:r1].reshape(N_DEV, N_POOL_GROUPS, pg // N_DEV, pg), (1, 0, 2, 3)).reshape(N_POOL_GROUPS, pg, pg)
    conv_w_full = jnp.transpose(small_all[:, r1:r1 + CONV_K, :conv_w.shape[2]], (1, 0, 2)).reshape(CONV_K, c_conv)
    pool_w_b = pool_w_full.astype(BF16)

    h0 = jnp.concatenate([jnp.broadcast_to(meta_full[None], (n_seq, N_META, d)), x], axis=1).reshape(n_rows, d)
    h0 = _pad_rows(h0, tp)
    h0_b = h0.astype(BF16)
    tgt = _pad_rows(jnp.pad(loss_target, ((0, 0), (N_META, 0), (0, 0))).reshape(n_rows, d), tp)

    wgu1 = gathered("ffn1_w_gu", h0_b)
    gu1, act1 = _ffn_gu("ffn1_gu", h0_b, wgu1)
    wd1 = gathered("ffn1_w_down", act1).reshape(N_CHIP, -1, d)
    pre1 = _ffn_down("ffn1_down", act1, wd1, h0)
    win_all = gathered("w_in", pre1)
    h1, h1_b = _ln_fwd("ln1", pre1, ln1_g, ln1_b)

    u = _proj_in("mix_in", h1_b, win_all)
    wout_all = gathered("w_out", u)
    y_conv = _conv_fwd("mix_conv", u, conv_w_full, n_rows, seq_len)
    dpool = _pool_fwd("mix_pool", u, 3 * c_conv, p_pool, pg, n_rows, seq_len)
    ypre, y_pool = _pool_mix("mix_pool_w", dpool, pool_w_b, pool_scale)
    y_mix = jnp.concatenate([y_conv, y_pool], axis=1)
    pre2 = _proj_out("mix_out", y_mix, wout_all, h1)
    wgu2 = gathered("ffn2_w_gu", pre2)
    h2, h2_b = _ln_fwd("ln2", pre2, ln2_g, ln2_b)

    gu2, act2 = _ffn_gu("ffn2_gu", h2_b, wgu2)
    wd2 = gathered("ffn2_w_down", act2).reshape(N_CHIP, -1, d)
    pre3 = _ffn_down("ffn2_down", act2, wd2, h2)

    dpre3, dpre3_b, d_ln3_g, d_ln3_b, sq = _ln_loss_bwd("ln3_loss", pre3, tgt, ln3_g, ln3_b, n_rows, seq_len)
    loss = lax.psum(0.5 * jnp.sum(sq) / d, ("x", "y", "c"))
    reducing = {}

    dgu2 = _ffn_bwd_dgu("ffn2_bwd_dgu", dpre3_b, wd2, gu2)
    g_wd2 = _ffn_bwd_wd("ffn2_bwd_wd", act2, dpre3_b)
    reducing["ffn2_w_down"], token = _reduce_start("ffn2_w_down", g_wd2.reshape(N_DEV, -1, d), core)
    g_wgu2 = _ffn_bwd_wgu("ffn2_bwd_wgu", h2_b, dgu2, (token,))
    reducing["ffn2_w_gu"], token = _reduce_start("ffn2_w_gu", g_wgu2, core)
    dh2 = _ffn_bwd_dh("ffn2_bwd_dh", dgu2, wgu2, dpre3, (token,))
    dpre2, dpre2_b, d_ln2_g, d_ln2_b = _ln_bwd("ln2_bwd", pre2, dh2, ln2_g)

    dy_mix = _proj_out_bwd_y("mix_out_bwd_y", dpre2_b, wout_all)
    g_wout = _proj_out_bwd_w("mix_out_bwd_w", y_mix, dpre2_b, N_DEV)
    reducing["w_out"], token = _reduce_start("w_out", g_wout, core)
    dyps, d_pool_scale = _pool_scale_bwd("mix_pool_scale_bwd", dy_mix, 1, ypre, pool_scale)
    dd = _pool_mix_bwd_in("mix_pool_w_bwd_in", dyps, pool_w_b)
    d_pool_w = _pool_mix_bwd_w("mix_pool_w_bwd_w", dpool, dyps, pg)
    du_pool = _pool_bwd("mix_pool_bwd", dd, pg, n_rows, seq_len)
    du_b, du_c, du_x, d_conv_w = _conv_bwd("mix_conv_bwd", u, conv_w_full, dy_mix, n_rows, seq_len)
    du = jnp.concatenate([du_b, du_c, du_x, du_pool], axis=1)
    g_win = _proj_in_bwd_w("mix_in_bwd_w", h1_b, du, N_DEV, (token,))
    reducing["w_in"], token = _reduce_start("w_in", g_win, core)
    dh1 = _proj_in_bwd_h("mix_in_bwd_h", du, win_all, dpre2, (token,))
    dpre1, dpre1_b, d_ln1_g, d_ln1_b = _ln_bwd("ln1_bwd", pre1, dh1, ln1_g)

    dgu1 = _ffn_bwd_dgu("ffn1_bwd_dgu", dpre1_b, wd1, gu1)
    g_wd1 = _ffn_bwd_wd("ffn1_bwd_wd", act1, dpre1_b)
    reducing["ffn1_w_down"], token = _reduce_start("ffn1_w_down", g_wd1.reshape(N_DEV, -1, d), core)
    g_wgu1 = _ffn_bwd_wgu("ffn1_bwd_wgu", h0_b, dgu1, (token,))
    reducing["ffn1_w_gu"], token = _reduce_start("ffn1_w_gu", g_wgu1, core)
    dh0 = _ffn_bwd_dh("ffn1_bwd_dh", dgu1, wgu1, dpre1, (token,))

    dh0_seq = dh0[:n_rows].reshape(n_seq, seq_len, d)
    grad_x = dh0_seq[:, N_META:]
    d_meta = jnp.sum(dh0_seq[:, :N_META], axis=0)

    grads, deltas, new_m, new_v = {}, {}, {}, {}
    for n in ["ffn2_w_down", "ffn2_w_gu", "w_out", "w_in", "ffn1_w_down", "ffn1_w_gu"]:
        summed, from_chips = _chips_wait(f"rs_chips_wait_{n}", reducing[n], dh0)
        g, dl, mm, vv = _adamw_big(f"adamw_{n}", summed, from_chips, chip, w_of[n][0], m_of[n][0], v_of[n][0])
        grads[n], deltas[n], new_m[n], new_v[n] = g[None], dl[None], mm[None], vv[None]

    def widen(a):
        return jnp.pad(a, ((0, 0), (0, d - a.shape[1])))

    small_part = jnp.concatenate([
        d_ln1_g, d_ln1_b, d_ln2_g, d_ln2_b, d_ln3_g, d_ln3_b, widen(d_pool_scale), widen(d_conv_w), d_meta,
        d_pool_w.reshape(-1, d)], axis=0)
    n_small_rows = small_part.shape[0]
    small_part = _pad_rows(small_part, -(-n_small_rows // 8) * 8)
    small_sum = _sum_parts("small_sum", _all_gather("ag_small_grads", small_part))
    o = 7 + CONV_K
    g_small = {
        "ln1_g": small_sum[0:1], "ln1_b": small_sum[1:2], "ln2_g": small_sum[2:3], "ln2_b": small_sum[3:4],
        "ln3_g": small_sum[4:5], "ln3_b": small_sum[5:6], "pool_scale": small_sum[6:7, :p_pool],
        "conv_w": lax.dynamic_slice_in_dim(small_sum[7:o, :c_conv], dev * (c_conv // N_DEV), c_conv // N_DEV, axis=1)[None],
        "meta_tokens": lax.dynamic_slice_in_dim(small_sum[o:o + N_META], dev * wcol, wcol, axis=1),
        "pool_w": lax.dynamic_slice_in_dim(small_sum[o + N_META:n_small_rows].reshape(N_POOL_GROUPS, pg, pg),
                                           dev * (pg // N_DEV), pg // N_DEV, axis=1)[None],
    }
    small = ["meta_tokens", "ln1_g", "ln1_b", "conv_w", "pool_w", "pool_scale", "ln2_g", "ln2_b", "ln3_g", "ln3_b"]

    def flat(a):
        return a.reshape(-1, a.shape[-1])

    outs = _adamw_small("adamw_small", [flat(w_of[n]) for n in small], [flat(g_small[n]) for n in small],
                        [flat(m_of[n]) for n in small], [flat(v_of[n]) for n in small])
    ns = len(small)
    for i, n in enumerate(small):
        shape = w_of[n].shape
        grads[n] = g_small[n].reshape(shape)
        deltas[n], new_m[n], new_v[n] = outs[i].reshape(shape), outs[ns + i].reshape(shape), outs[2 * ns + i].reshape(shape)

    return (loss, grad_x, *[grads[n] for n in names], *[deltas[n] for n in names],
            *[new_m[n] for n in names], *[new_v[n] for n in names])
```

```python
import functools

import jax
import jax.numpy as jnp
from jax import lax
from jax.experimental import pallas as pl
from jax.experimental.pallas import tpu as pltpu

N_DEV = 8
N_CHIP = 4
N_META = 16
CONV_K = 3
POOL_WINDOWS = (2, 4, 8, 16)
N_POOL_GROUPS = len(POOL_WINDOWS)
LN_EPS = 1e-5
DEPTH = 1
ALPHA = (2.0 * DEPTH) ** 0.25
ADAM_LR = 0.001
ADAM_B1 = 0.9
ADAM_B2 = 0.999
ADAM_EPS = 1e-08
ADAM_WD = 0.01
ADAM_STEP = 10

V7X_VMEM_BYTES = 64 * 1024 * 1024
VMEM_LIMIT = V7X_VMEM_BYTES - 6 * 1024 * 1024
LANE = 128
ROW_ALIGN = 3 * LANE
TM_BIG = 1408
TM_WIDE = 704
TK = 512
TN = 1024
TR_LN = 128
ELEM_BLOCK_BYTES = 1 << 20
TC_MIX = LANE
EPILOGUE_ROWS = 64

NN = (((1,), (0,)), ((), ()))
NT = (((1,), (1,)), ((), ()))
TN_DIMS = (((0,), (0,)), ((), ()))
MESH = pl.DeviceIdType.MESH
BF16 = jnp.bfloat16
F32 = jnp.float32
ANY = pl.BlockSpec(memory_space=pl.ANY)
HBM = pl.BlockSpec(memory_space=pltpu.HBM)
SEM = pl.BlockSpec(memory_space=pltpu.SEMAPHORE)
EFFECT = pltpu.SideEffectType.DATAFLOW_SIDE_EFFECTING


def _tile(n, target, mult):
    best = None
    for t in range(mult, min(n, target) + 1, mult):
        if n % t == 0:
            best = t
    return n if best is None else best


def _params(sem):
    return pltpu.CompilerParams(dimension_semantics=sem, vmem_limit_bytes=VMEM_LIMIT)


def _sds(shape, dtype):
    return jax.ShapeDtypeStruct(shape, dtype)


def _row_chunks(n_rows, fn):
    ch = _tile(n_rows, EPILOGUE_ROWS, 16)

    def step(i, carry):
        fn(pl.ds(pl.multiple_of(i * ch, ch), ch))
        return carry

    lax.fori_loop(0, n_rows // ch, step, 0)


def _mm(name, grid, dims, ab, ab_specs, extras, extra_specs, out_shape, out_specs, acc_shape, epilogue, deps=()):
    nk = grid[-1]
    n_extra = len(extras)
    n_in = 2 + n_extra + len(deps)
    n_out = len(out_shape)
    kax = len(grid) - 1

    def body(*refs):
        a_ref, b_ref = refs[0], refs[1]
        ex = refs[2:2 + n_extra]
        outs = refs[n_in:n_in + n_out]
        if nk == 1:
            epilogue(lax.dot_general(a_ref[...], b_ref[...], dims, preferred_element_type=F32), ex, outs, slice(None))
            return
        acc = refs[-1]
        k = pl.program_id(kax)

        @pl.when(k == 0)
        def _():
            acc[...] = jnp.zeros_like(acc)

        acc[...] += lax.dot_general(a_ref[...], b_ref[...], dims, preferred_element_type=F32)

        @pl.when(k == nk - 1)
        def _():
            _row_chunks(acc_shape[0], lambda rows: epilogue(acc[rows, :], ex, outs, rows))

    scratch = [] if nk == 1 else [pltpu.VMEM(acc_shape, F32)]
    sem = ("parallel",) * kax + ("arbitrary",)
    return pl.pallas_call(
        body, name=name, grid=grid, in_specs=list(ab_specs) + list(extra_specs) + [ANY] * len(deps), out_specs=list(out_specs),
        out_shape=list(out_shape), scratch_shapes=scratch, compiler_params=_params(sem),
    )(*ab, *extras, *deps)


def _silu_parts(g):
    s = 1.0 / (1.0 + jnp.exp(-g))
    return s, g * s


def _ffn_gu(name, h_b, wgu_all):
    tp, d = h_b.shape
    ns, _, ng = wgu_all.shape
    half = ns // 2
    tm, tk = _tile(tp, TM_WIDE, 16), _tile(d, TK, LANE)
    grid = (tp // tm, half, d // tk)
    nk = grid[-1]

    def body(h_ref, wg_ref, wu_ref, gu_ref, act_ref, acc_g, acc_u):
        k = pl.program_id(2)

        @pl.when(k == 0)
        def _():
            acc_g[...] = jnp.zeros_like(acc_g)
            acc_u[...] = jnp.zeros_like(acc_u)

        acc_g[...] += jnp.dot(h_ref[...], wg_ref[...], preferred_element_type=F32)
        acc_u[...] += jnp.dot(h_ref[...], wu_ref[...], preferred_element_type=F32)

        @pl.when(k == nk - 1)
        def _():
            def finish(rows):
                g = acc_g[rows, :]
                u = acc_u[rows, :]
                _, silu = _silu_parts(g)
                gu_ref[0, rows, :] = g.astype(BF16)
                gu_ref[1, rows, :] = u.astype(BF16)
                act_ref[rows, :] = (silu * u).astype(BF16)

            _row_chunks(tm, finish)

    return pl.pallas_call(
        body, name=name, grid=grid,
        in_specs=[pl.BlockSpec((tm, tk), lambda m, s, k: (m, k)),
                  pl.BlockSpec((None, tk, ng), lambda m, s, k: (s, k, 0)),
                  pl.BlockSpec((None, tk, ng), lambda m, s, k: (s + half, k, 0))],
        out_specs=[pl.BlockSpec((None, 2, tm, ng), lambda m, s, k: (s, 0, m, 0)),
                   pl.BlockSpec((None, tm, ng), lambda m, s, k: (s, m, 0))],
        out_shape=[_sds((half, 2, tp, ng), BF16), _sds((half, tp, ng), BF16)],
        scratch_shapes=[pltpu.VMEM((tm, ng), F32), pltpu.VMEM((tm, ng), F32)],
        compiler_params=_params(("parallel", "parallel", "arbitrary")),
    )(h_b, wgu_all, wgu_all)


def _ffn_down(name, act, wd4, h):
    ns, tp, ng = act.shape
    d = wd4.shape[2]
    tm, tn = _tile(tp, TM_WIDE, 16), _tile(d, TN, LANE)

    def epi(acc, ex, outs, rows):
        outs[0][rows, :] = ALPHA * ex[0][rows, :] + 0.5 * acc

    return _mm(name, (tp // tm, d // tn, ns), NN, (act, wd4),
               [pl.BlockSpec((None, tm, ng), lambda m, n, s: (s, m, 0)),
                pl.BlockSpec((None, ng, tn), lambda m, n, s: (s, 0, n))],
               (h,), [pl.BlockSpec((tm, tn), lambda m, n, s: (m, n))],
               [_sds((tp, d), F32)], [pl.BlockSpec((tm, tn), lambda m, n, s: (m, n))], (tm, tn), epi)[0]


def _ffn_bwd_dgu(name, dp_b, wd4, gu):
    tp, d = dp_b.shape
    ns, ng, _ = wd4.shape
    tm, tk = _tile(tp, TM_WIDE, 16), _tile(d, TK, LANE)

    def epi(acc, ex, outs, rows):
        g = ex[0][0, rows, :].astype(F32)
        u = ex[0][1, rows, :].astype(F32)
        da = 0.5 * acc
        s, silu = _silu_parts(g)
        outs[0][0, rows, :] = (da * u * (s + silu * (1.0 - s))).astype(BF16)
        outs[0][1, rows, :] = (da * silu).astype(BF16)

    return _mm(name, (tp // tm, ns, d // tk), NT, (dp_b, wd4),
               [pl.BlockSpec((tm, tk), lambda m, s, k: (m, k)),
                pl.BlockSpec((None, ng, tk), lambda m, s, k: (s, 0, k))],
               (gu,), [pl.BlockSpec((None, 2, tm, ng), lambda m, s, k: (s, 0, m, 0))],
               [_sds((ns, 2, tp, ng), BF16)], [pl.BlockSpec((None, 2, tm, ng), lambda m, s, k: (s, 0, m, 0))],
               (tm, ng), epi)[0]


def _ffn_bwd_wd(name, act, dp_b):
    ns, tp, ng = act.shape
    d = dp_b.shape[1]
    tkt, tn = _tile(tp, TM_WIDE, LANE), _tile(d, TN, LANE)

    def epi(acc, ex, outs, rows):
        outs[0][rows, :] = (0.5 * acc).astype(BF16)

    return _mm(name, (ns, d // tn, tp // tkt), TN_DIMS, (act, dp_b),
               [pl.BlockSpec((None, tkt, ng), lambda s, n, t: (s, t, 0)),
                pl.BlockSpec((tkt, tn), lambda s, n, t: (t, n))],
               (), [], [_sds((ns, ng, d), BF16)], [pl.BlockSpec((None, ng, tn), lambda s, n, t: (s, 0, n))],
               (ng, tn), epi)[0]


def _ffn_bwd_wgu(name, h_b, dgu, deps=()):
    tp, d = h_b.shape
    ns, _, _, ng = dgu.shape
    tkt, tmd = _tile(tp, TM_WIDE, LANE), _tile(d, TN, LANE)

    def epi(acc, ex, outs, rows):
        outs[0][rows, :] = acc.astype(BF16)

    return _mm(name, (ns, 2, d // tmd, tp // tkt), TN_DIMS, (h_b, dgu),
               [pl.BlockSpec((tkt, tmd), lambda s, j, i, t: (t, i)),
                pl.BlockSpec((None, None, tkt, ng), lambda s, j, i, t: (s, j, t, 0))],
               (), [], [_sds((2 * ns, d, ng), BF16)],
               [pl.BlockSpec((None, tmd, ng), lambda s, j, i, t: (j * ns + s, i, 0))], (tmd, ng), epi, deps)[0]


def _ffn_bwd_dh(name, dgu, wgu_all, dp, deps=()):
    ns, _, tp, ng = dgu.shape
    d = wgu_all.shape[1]
    tm, tn = _tile(tp, TM_WIDE, 16), _tile(d, TN, LANE)

    def epi(acc, ex, outs, rows):
        outs[0][rows, :] = ALPHA * ex[0][rows, :] + acc

    return _mm(name, (tp // tm, d // tn, 2 * ns), NT, (dgu, wgu_all),
               [pl.BlockSpec((None, None, tm, ng), lambda m, n, j: (j % ns, j // ns, m, 0)),
                pl.BlockSpec((None, tn, ng), lambda m, n, j: (j, n, 0))],
               (dp,), [pl.BlockSpec((tm, tn), lambda m, n, j: (m, n))],
               [_sds((tp, d), F32)], [pl.BlockSpec((tm, tn), lambda m, n, j: (m, n))], (tm, tn), epi, deps)[0]


def _ln_stats(x):
    mu = jnp.mean(x, axis=-1, keepdims=True)
    xc = x - mu
    var = jnp.mean(xc * xc, axis=-1, keepdims=True)
    rstd = lax.rsqrt(var + LN_EPS)
    return xc * rstd, rstd


def _ln_bwd_rows(dy, xhat, rstd, g):
    dxh = dy * g
    m1 = jnp.mean(dxh, axis=-1, keepdims=True)
    m2 = jnp.mean(dxh * xhat, axis=-1, keepdims=True)
    return rstd * (dxh - m1 - xhat * m2)


def _ln_fwd(name, pre, g, b):
    tp, d = pre.shape
    tr = _tile(tp, TR_LN, 16)

    def body(x_ref, g_ref, b_ref, y_ref, yb_ref):
        xhat, _ = _ln_stats(x_ref[...])
        y = xhat * g_ref[...] + b_ref[...]
        y_ref[...] = y
        yb_ref[...] = y.astype(BF16)

    row = pl.BlockSpec((tr, d), lambda i: (i, 0))
    vec = pl.BlockSpec((1, d), lambda i: (0, 0))
    return pl.pallas_call(body, name=name, grid=(tp // tr,), in_specs=[row, vec, vec], out_specs=[row, row],
                          out_shape=[_sds((tp, d), F32), _sds((tp, d), BF16)],
                          compiler_params=_params(("parallel",)))(pre, g, b)


def _accumulate(i, ref, val):
    @pl.when(i == 0)
    def _():
        ref[...] = val

    @pl.when(i > 0)
    def _():
        ref[...] += val


def _ln_bwd(name, pre, dy, g):
    tp, d = pre.shape
    tr = _tile(tp, TR_LN, 16)

    def body(x_ref, dy_ref, g_ref, dx_ref, dxb_ref, dg_ref, db_ref):
        i = pl.program_id(0)
        xhat, rstd = _ln_stats(x_ref[...])
        dy = dy_ref[...]
        dx = _ln_bwd_rows(dy, xhat, rstd, g_ref[...])
        dx_ref[...] = dx
        dxb_ref[...] = dx.astype(BF16)
        _accumulate(i, dg_ref, jnp.sum(dy * xhat, axis=0, keepdims=True))
        _accumulate(i, db_ref, jnp.sum(dy, axis=0, keepdims=True))

    row = pl.BlockSpec((tr, d), lambda i: (i, 0))
    vec = pl.BlockSpec((1, d), lambda i: (0, 0))
    return pl.pallas_call(body, name=name, grid=(tp // tr,), in_specs=[row, row, vec], out_specs=[row, row, vec, vec],
                          out_shape=[_sds((tp, d), F32), _sds((tp, d), BF16), _sds((1, d), F32), _sds((1, d), F32)],
                          compiler_params=_params(("arbitrary",)))(pre, dy, g)


def _ln_loss_bwd(name, pre, tgt, g, b, n_rows, seq_len):
    tp, d = pre.shape
    tr = _tile(tp, TR_LN, 16)
    n_seq = n_rows // seq_len

    def body(x_ref, t_ref, g_ref, b_ref, dx_ref, dxb_ref, dg_ref, db_ref, sq_ref):
        i = pl.program_id(0)
        xhat, rstd = _ln_stats(x_ref[...])
        gain = g_ref[...]
        y = xhat * gain + b_ref[...]
        r = i * tr + lax.broadcasted_iota(jnp.int32, (tr, 1), 0)
        pos = r
        for s in range(1, n_seq):
            pos = jnp.where(r >= s * seq_len, r - s * seq_len, pos)
        live = jnp.logical_and(r < n_rows, pos >= N_META)
        err = jnp.where(live, y - t_ref[...], 0.0)
        dy = err * (1.0 / d)
        dx = _ln_bwd_rows(dy, xhat, rstd, gain)
        dx_ref[...] = dx
        dxb_ref[...] = dx.astype(BF16)
        _accumulate(i, dg_ref, jnp.sum(dy * xhat, axis=0, keepdims=True))
        _accumulate(i, db_ref, jnp.sum(dy, axis=0, keepdims=True))
        _accumulate(i, sq_ref, jnp.sum(err * err, axis=0, keepdims=True))

    row = pl.BlockSpec((tr, d), lambda i: (i, 0))
    vec = pl.BlockSpec((1, d), lambda i: (0, 0))
    return pl.pallas_call(
        body, name=name, grid=(tp // tr,), in_specs=[row, row, vec, vec], out_specs=[row, row, vec, vec, vec],
        out_shape=[_sds((tp, d), F32), _sds((tp, d), BF16), _sds((1, d), F32), _sds((1, d), F32), _sds((1, d), F32)],
        compiler_params=_params(("arbitrary",)))(pre, tgt, g, b)


def _proj_in(name, h_b, win_all):
    tp, d = h_b.shape
    ns, _, ni = win_all.shape
    tm, tk = _tile(tp, TM_BIG, 16), _tile(d, TK, LANE)

    def epi(acc, ex, outs, rows):
        outs[0][rows, :] = acc

    return _mm(name, (tp // tm, ns, d // tk), NN, (h_b, win_all),
               [pl.BlockSpec((tm, tk), lambda m, j, k: (m, k)),
                pl.BlockSpec((None, tk, ni), lambda m, j, k: (j, k, 0))],
               (), [], [_sds((tp, ns * ni), F32)], [pl.BlockSpec((tm, ni), lambda m, j, k: (m, j))], (tm, ni), epi)[0]


def _positions(tp, n_rows, seq_len):
    r = lax.broadcasted_iota(jnp.int32, (tp, 1), 0)
    pos = r
    for s in range(1, n_rows // seq_len):
        pos = jnp.where(r >= s * seq_len, r - s * seq_len, pos)
    return pos


def _shift_down(x, s, pos):
    return jnp.where(pos >= s, pltpu.roll(x, s, 0), 0.0)


def _shift_up(x, s, pos, seq_len):
    return jnp.where(pos + s < seq_len, pltpu.roll(x, x.shape[0] - s, 0), 0.0)


def _conv_fwd(name, u, conv_w, n_rows, seq_len):
    tp = u.shape[0]
    c = conv_w.shape[1]
    tc = _tile(c, TC_MIX, LANE)
    nb = c // tc

    def body(gb_ref, gc_ref, xi_ref, w_ref, y_ref):
        pos = _positions(tp, n_rows, seq_len)
        v = gc_ref[...] * xi_ref[...]
        w = w_ref[...]
        y = _shift_down(v, 2, pos) * w[0:1]
        y = y + _shift_down(v, 1, pos) * w[1:2]
        y = y + v * w[2:3]
        y_ref[...] = (gb_ref[...] * y).astype(BF16)

    col = lambda off: pl.BlockSpec((tp, tc), lambda i: (0, off + i))
    return pl.pallas_call(body, name=name, grid=(nb,), in_specs=[col(0), col(nb), col(2 * nb), pl.BlockSpec((CONV_K, tc), lambda i: (0, i))],
                          out_specs=pl.BlockSpec((tp, tc), lambda i: (0, i)), out_shape=_sds((tp, c), BF16),
                          compiler_params=_params(("parallel",)))(u, u, u, conv_w)


def _conv_bwd(name, u, conv_w, dy, n_rows, seq_len):
    tp = u.shape[0]
    c = conv_w.shape[1]
    tc = _tile(c, TC_MIX, LANE)
    nb = c // tc

    def body(gb_ref, gc_ref, xi_ref, w_ref, dy_ref, dgb_ref, dgc_ref, dxi_ref, dw_ref):
        pos = _positions(tp, n_rows, seq_len)
        gc, xi = gc_ref[...], xi_ref[...]
        v = gc * xi
        w = w_ref[...]
        v2, v1 = _shift_down(v, 2, pos), _shift_down(v, 1, pos)
        conv = v2 * w[0:1]
        conv = conv + v1 * w[1:2]
        conv = conv + v * w[2:3]
        dyc = dy_ref[...]
        dgb_ref[...] = (dyc * conv).astype(BF16)
        dconv = dyc * gb_ref[...]
        dv = dconv * w[2:3] + _shift_up(dconv, 1, pos, seq_len) * w[1:2] + _shift_up(dconv, 2, pos, seq_len) * w[0:1]
        dgc_ref[...] = (dv * xi).astype(BF16)
        dxi_ref[...] = (dv * gc).astype(BF16)
        dw_ref[0:1, :] = jnp.sum(dconv * v2, axis=0, keepdims=True)
        dw_ref[1:2, :] = jnp.sum(dconv * v1, axis=0, keepdims=True)
        dw_ref[2:3, :] = jnp.sum(dconv * v, axis=0, keepdims=True)

    col = lambda off: pl.BlockSpec((tp, tc), lambda i: (0, off + i))
    wspec = pl.BlockSpec((CONV_K, tc), lambda i: (0, i))
    return pl.pallas_call(body, name=name, grid=(nb,), in_specs=[col(0), col(nb), col(2 * nb), wspec, col(0)],
                          out_specs=[col(0), col(0), col(0), wspec],
                          out_shape=[_sds((tp, c), BF16)] * 3 + [_sds((CONV_K, c), F32)],
                          compiler_params=_params(("parallel",)))(u, u, u, conv_w, dy)


def _window_select(group, parts):
    out = parts[-1]
    for gi in range(len(parts) - 2, -1, -1):
        out = jnp.where(group == gi, parts[gi], out)
    return out


def _pool_fwd(name, u, col0, p, pg, n_rows, seq_len):
    tp = u.shape[0]
    tc = _tile(pg, TC_MIX, LANE)
    per_group = pg // tc

    def body(z_ref, d_ref):
        group = pl.program_id(0) // per_group
        pos = _positions(tp, n_rows, seq_len)
        z = z_ref[...]
        sums, s, w = [], z, 1
        for _ in POOL_WINDOWS:
            s = s + _shift_down(s, w, pos)
            w *= 2
            sums.append(s)
        total = _window_select(group, sums)
        count = jnp.minimum(pos + 1, 2 << group).astype(F32)
        d_ref[...] = (total / count - z).astype(BF16)

    return pl.pallas_call(body, name=name, grid=(p // tc,), in_specs=[pl.BlockSpec((tp, tc), lambda i: (0, col0 // tc + i))],
                          out_specs=pl.BlockSpec((tp, tc), lambda i: (0, i)), out_shape=_sds((tp, p), BF16),
                          compiler_params=_params(("parallel",)))(u)


def _pool_bwd(name, dd, pg, n_rows, seq_len):
    tp, p = dd.shape
    tc = _tile(pg, TC_MIX, LANE)
    per_group = pg // tc

    def body(dd_ref, dz_ref):
        group = pl.program_id(0) // per_group
        pos = _positions(tp, n_rows, seq_len)
        dd_v = dd_ref[...]
        count = jnp.minimum(pos + 1, 2 << group).astype(F32)
        sums, s, w = [], dd_v / count, 1
        for _ in POOL_WINDOWS:
            s = s + _shift_up(s, w, pos, seq_len)
            w *= 2
            sums.append(s)
        dz_ref[...] = (_window_select(group, sums) - dd_v).astype(BF16)

    spec = pl.BlockSpec((tp, tc), lambda i: (0, i))
    return pl.pallas_call(body, name=name, grid=(p // tc,), in_specs=[spec], out_specs=spec, out_shape=_sds((tp, p), BF16),
                          compiler_params=_params(("parallel",)))(dd)


def _pool_mix(name, dpool, pool_w_b, scale):
    tp, p = dpool.shape
    ng, pg, _ = pool_w_b.shape
    tm = _tile(tp, TM_BIG, 16)

    def epi(acc, ex, outs, rows):
        outs[0][rows, :] = acc
        outs[1][rows, :] = (acc * ex[0][...]).astype(BF16)

    blk = pl.BlockSpec((tm, pg), lambda m, g, k: (m, g))
    return _mm(name, (tp // tm, ng, 1), NN, (dpool, pool_w_b), [blk, pl.BlockSpec((None, pg, pg), lambda m, g, k: (g, 0, 0))],
               (scale,), [pl.BlockSpec((1, pg), lambda m, g, k: (0, g))],
               [_sds((tp, p), F32), _sds((tp, p), BF16)], [blk, blk], None, epi)


def _pool_scale_bwd(name, dy, col_block, ypre, scale):
    tp, p = ypre.shape
    tr = _tile(tp, TR_LN, 16)

    def body(dy_ref, yp_ref, s_ref, o_ref, ds_ref):
        i = pl.program_id(0)
        dyp = dy_ref[...]
        o_ref[...] = (dyp * s_ref[...]).astype(BF16)
        _accumulate(i, ds_ref, jnp.sum(dyp * yp_ref[...], axis=0, keepdims=True))

    row = pl.BlockSpec((tr, p), lambda i: (i, 0))
    vec = pl.BlockSpec((1, p), lambda i: (0, 0))
    return pl.pallas_call(body, name=name, grid=(tp // tr,), in_specs=[pl.BlockSpec((tr, p), lambda i: (i, col_block)), row, vec],
                          out_specs=[row, vec], out_shape=[_sds((tp, p), BF16), _sds((1, p), F32)],
                          compiler_params=_params(("arbitrary",)))(dy, ypre, scale)


def _pool_mix_bwd_in(name, dyps, pool_w_b):
    tp, p = dyps.shape
    ng, pg, _ = pool_w_b.shape
    tm = _tile(tp, TM_BIG, 16)

    def epi(acc, ex, outs, rows):
        outs[0][rows, :] = acc

    blk = pl.BlockSpec((tm, pg), lambda m, g, k: (m, g))
    return _mm(name, (tp // tm, ng, 1), NT, (dyps, pool_w_b), [blk, pl.BlockSpec((None, pg, pg), lambda m, g, k: (g, 0, 0))],
               (), [], [_sds((tp, p), F32)], [blk], None, epi)[0]


def _pool_mix_bwd_w(name, dpool, dyps, pg):
    tp, p = dpool.shape
    ng = p // pg
    tkt = _tile(tp, TM_BIG, LANE)

    def epi(acc, ex, outs, rows):
        outs[0][rows, :] = acc

    blk = pl.BlockSpec((tkt, pg), lambda g, t: (t, g))
    return _mm(name, (ng, tp // tkt), TN_DIMS, (dpool, dyps), [blk, blk], (), [],
               [_sds((ng, pg, pg), F32)], [pl.BlockSpec((None, pg, pg), lambda g, t: (g, 0, 0))], (pg, pg), epi)[0]


def _proj_out(name, y, wout_all, h):
    tp = y.shape[0]
    ns, ro, d = wout_all.shape
    tm, tn = _tile(tp, TM_BIG, 16), _tile(d, TN, LANE)

    def epi(acc, ex, outs, rows):
        outs[0][rows, :] = ALPHA * ex[0][rows, :] + acc

    mn = pl.BlockSpec((tm, tn), lambda m, n, j: (m, n))
    return _mm(name, (tp // tm, d // tn, ns), NN, (y, wout_all),
               [pl.BlockSpec((tm, ro), lambda m, n, j: (m, j)), pl.BlockSpec((None, ro, tn), lambda m, n, j: (j, 0, n))],
               (h,), [mn], [_sds((tp, d), F32)], [mn], (tm, tn), epi)[0]


def _proj_out_bwd_y(name, dp_b, wout_all):
    tp, d = dp_b.shape
    ns, ro, _ = wout_all.shape
    tm, tk = _tile(tp, TM_BIG, 16), _tile(d, TK, LANE)

    def epi(acc, ex, outs, rows):
        outs[0][rows, :] = acc

    return _mm(name, (tp // tm, ns, d // tk), NT, (dp_b, wout_all),
               [pl.BlockSpec((tm, tk), lambda m, j, k: (m, k)), pl.BlockSpec((None, ro, tk), lambda m, j, k: (j, 0, k))],
               (), [], [_sds((tp, ns * ro), F32)], [pl.BlockSpec((tm, ro), lambda m, j, k: (m, j))], (tm, ro), epi)[0]


def _proj_out_bwd_w(name, y, dp_b, ns):
    tp, c = y.shape
    d = dp_b.shape[1]
    ro = c // ns
    tkt, tn = _tile(tp, TM_BIG, LANE), _tile(d, 2 * TN, LANE)

    def epi(acc, ex, outs, rows):
        outs[0][rows, :] = acc.astype(BF16)

    return _mm(name, (ns, d // tn, tp // tkt), TN_DIMS, (y, dp_b),
               [pl.BlockSpec((tkt, ro), lambda j, n, t: (t, j)), pl.BlockSpec((tkt, tn), lambda j, n, t: (t, n))],
               (), [], [_sds((ns, ro, d), BF16)], [pl.BlockSpec((None, ro, tn), lambda j, n, t: (j, 0, n))], (ro, tn), epi)[0]


def _proj_in_bwd_w(name, h_b, du, ns, deps=()):
    tp, d = h_b.shape
    ni = du.shape[1] // ns
    tkt, tmd = _tile(tp, TM_WIDE, LANE), _tile(d, 2 * TN, LANE)

    def epi(acc, ex, outs, rows):
        outs[0][rows, :] = acc.astype(BF16)

    return _mm(name, (ns, d // tmd, tp // tkt), TN_DIMS, (h_b, du),
               [pl.BlockSpec((tkt, tmd), lambda j, i, t: (t, i)), pl.BlockSpec((tkt, ni), lambda j, i, t: (t, j))],
               (), [], [_sds((ns, d, ni), BF16)], [pl.BlockSpec((None, tmd, ni), lambda j, i, t: (j, i, 0))], (tmd, ni), epi, deps)[0]


def _proj_in_bwd_h(name, du, win_all, dp, deps=()):
    tp = du.shape[0]
    ns, d, ni = win_all.shape
    tm, tn = _tile(tp, TM_BIG, 16), _tile(d, TN, LANE)

    def epi(acc, ex, outs, rows):
        outs[0][rows, :] = ALPHA * ex[0][rows, :] + acc

    mn = pl.BlockSpec((tm, tn), lambda m, n, j: (m, n))
    return _mm(name, (tp // tm, d // tn, ns), NT, (du, win_all),
               [pl.BlockSpec((tm, ni), lambda m, n, j: (m, j)), pl.BlockSpec((None, tn, ni), lambda m, n, j: (j, n, 0))],
               (dp,), [mn], [_sds((tp, d), F32)], [mn], (tm, tn), epi, deps)[0]


def _place():
    return lax.axis_index("x"), lax.axis_index("y"), lax.axis_index("c")


def _hbm(a):
    return pltpu.with_memory_space_constraint(a, pltpu.HBM)


def _token_shape():
    return _sds((8, LANE), F32)


def _gather_peers(x, y, c):
    return [(x, y, 1 - c), (1 - x, y, c), (x, 1 - y, c), (1 - x, 1 - y, c)]


def _gather_start(name, shards, after=()):
    n = len(shards)
    n_peer = N_CHIP
    n_in = 2 * n + len(after)

    def body(*refs):
        x_refs, land_refs = refs[:n], refs[n:2 * n]
        send_sems, recv_sems = refs[n_in:n_in + n], refs[n_in + n:n_in + 2 * n]
        token = refs[-1]
        x, y, c = _place()
        me = 4 * x + 2 * y + c
        for i in range(n):
            for k, peer in enumerate(_gather_peers(x, y, c)):
                pltpu.make_async_remote_copy(
                    src_ref=x_refs[i], dst_ref=land_refs[i].at[me], send_sem=send_sems[i].at[k], recv_sem=recv_sems[i].at[k],
                    device_id=peer, device_id_type=MESH).start()
        token[...] = jnp.zeros_like(token)

    lands = [lax.empty((N_DEV,) + s.shape, s.dtype) for s in shards]
    sem = pltpu.SemaphoreType.DMA((n_peer,))
    out = pl.pallas_call(
        body, name=name,
        out_shape=[sem] * (2 * n) + [pltpu.HBM(s.shape, s.dtype) for s in shards]
        + [pltpu.HBM(l.shape, l.dtype) for l in lands] + [_token_shape()],
        in_specs=[HBM] * (2 * n) + [ANY] * len(after),
        out_specs=[SEM] * (2 * n) + [HBM] * (2 * n) + [pl.BlockSpec(memory_space=pltpu.VMEM)],
        input_output_aliases={i: 2 * n + i for i in range(2 * n)},
        compiler_params=pltpu.CompilerParams(has_side_effects=EFFECT),
    )(*[_hbm(s) for s in shards], *[_hbm(l) for l in lands], *after)
    per = [(out[i], out[n + i], out[2 * n + i], out[3 * n + i]) for i in range(n)]
    return per, out[-1]


def _gather_wait(name, started, after):
    send_sems, recv_sems, shard, land = started

    def body(x_ref, land_ref, send_ref, recv_ref, after_ref, x_out, land_out):
        x, y, c = _place()
        for k, (px, py, pc) in enumerate(_gather_peers(x, y, c)):
            cp = pltpu.make_async_remote_copy(
                src_ref=x_ref, dst_ref=land_ref.at[4 * px + 2 * py + pc], send_sem=send_ref.at[k], recv_sem=recv_ref.at[k],
                device_id=(px, py, pc), device_id_type=MESH)
            cp.wait_send()
            cp.wait_recv()

    return pl.pallas_call(
        body, name=name, out_shape=(pltpu.HBM(shard.shape, shard.dtype), pltpu.HBM(land.shape, land.dtype)),
        in_specs=(HBM, HBM, SEM, SEM, ANY), out_specs=(HBM, HBM), input_output_aliases={0: 0, 1: 1},
        compiler_params=pltpu.CompilerParams(has_side_effects=EFFECT),
    )(shard, land, send_sems, recv_sems, after)


def _gather_finish(name, land, shard, deps=()):
    def body(land_ref, x_ref, *rest):
        out_ref, send_sems, recv_sems, local_sem = rest[len(deps):]
        x, y, c = _place()
        chips = [(1 - x, y), (x, 1 - y), (1 - x, 1 - y)]
        mine = pltpu.make_async_copy(x_ref, out_ref.at[4 * x + 2 * y + c], local_sem)
        mine.start()
        copies = []
        for k, (px, py) in enumerate(chips):
            slab = 4 * px + 2 * py + c
            copies.append(pltpu.make_async_remote_copy(
                src_ref=land_ref.at[slab], dst_ref=out_ref.at[slab], send_sem=send_sems.at[k], recv_sem=recv_sems.at[k],
                device_id=(x, y, 1 - c), device_id_type=MESH))
        for cp in copies:
            cp.start()
        for cp in copies:
            cp.wait()
        mine.wait()

    return pl.pallas_call(
        body, name=name, out_shape=_sds(land.shape, land.dtype), in_specs=[ANY, ANY] + [ANY] * len(deps), out_specs=ANY,
        input_output_aliases={0: 0},
        scratch_shapes=[pltpu.SemaphoreType.DMA((N_CHIP - 1,)), pltpu.SemaphoreType.DMA((N_CHIP - 1,)), pltpu.SemaphoreType.DMA(())],
    )(land, shard, *deps)


def _chips_start(name, s):
    n_peer = N_CHIP - 1

    def body(s_ref, land_ref, send_sems, recv_sems, s_out, land_out, token):
        x, y, c = _place()
        for k, (px, py) in enumerate([(1 - x, y), (x, 1 - y), (1 - x, 1 - y)]):
            pltpu.make_async_remote_copy(
                src_ref=s_ref.at[2 * px + py], dst_ref=land_ref.at[k], send_sem=send_sems.at[k], recv_sem=recv_sems.at[k],
                device_id=(px, py, c), device_id_type=MESH).start()
        token[...] = jnp.zeros_like(token)

    land = lax.empty((n_peer,) + s.shape[1:], s.dtype)
    sem = pltpu.SemaphoreType.DMA((n_peer,))
    out = pl.pallas_call(
        body, name=name,
        out_shape=[sem, sem, pltpu.HBM(s.shape, s.dtype), pltpu.HBM(land.shape, land.dtype), _token_shape()],
        in_specs=[HBM, HBM], out_specs=[SEM, SEM, HBM, HBM, pl.BlockSpec(memory_space=pltpu.VMEM)],
        input_output_aliases={0: 2, 1: 3}, compiler_params=pltpu.CompilerParams(has_side_effects=EFFECT),
    )(_hbm(s), _hbm(land))
    return out[:4], out[4]


def _chips_wait(name, started, after):
    send_sems, recv_sems, s, land = started

    def body(s_ref, land_ref, send_ref, recv_ref, after_ref, s_out, land_out):
        x, y, c = _place()
        for k, (px, py) in enumerate([(1 - x, y), (x, 1 - y), (1 - x, 1 - y)]):
            cp = pltpu.make_async_remote_copy(
                src_ref=s_ref.at[2 * px + py], dst_ref=land_ref.at[k], send_sem=send_ref.at[k], recv_sem=recv_ref.at[k],
                device_id=(px, py, c), device_id_type=MESH)
            cp.wait_send()
            cp.wait_recv()

    return pl.pallas_call(
        body, name=name, out_shape=(pltpu.HBM(s.shape, s.dtype), pltpu.HBM(land.shape, land.dtype)),
        in_specs=(HBM, HBM, SEM, SEM, ANY), out_specs=(HBM, HBM), input_output_aliases={0: 0, 1: 1},
        compiler_params=pltpu.CompilerParams(has_side_effects=EFFECT),
    )(s, land, send_sems, recv_sems, after)


def _all_gather(name, shard):
    def body(x_ref, out_ref, send_sems, recv_sems, local_sem):
        x, y, c = _place()
        me, sibling = (x, y, c), (x, y, 1 - c)
        chips = [(1 - x, y), (x, 1 - y), (1 - x, 1 - y)]

        def slab(px, py, pc):
            return out_ref.at[4 * px + 2 * py + pc]

        def copy(k, block, to, src=None):
            return pltpu.make_async_remote_copy(
                src_ref=slab(*block) if src is None else src, dst_ref=slab(*block),
                send_sem=send_sems.at[k], recv_sem=recv_sems.at[k], device_id=to, device_id_type=MESH)

        mine = pltpu.make_async_copy(x_ref, slab(*me), local_sem)
        mine.start()
        first = [copy(0, me, sibling, src=x_ref)]
        first += [copy(1 + j, me, (*chip, c), src=x_ref) for j, chip in enumerate(chips)]
        for cp in first:
            cp.start()
        passed = [copy(4 + j, (*chip, c), sibling) for j, chip in enumerate(chips)]
        for j, chip in enumerate(chips):
            copy(1 + j, (*chip, c), me).wait_recv()
            passed[j].start()
        copy(0, sibling, me).wait_recv()
        for j, chip in enumerate(chips):
            copy(4 + j, (*chip, 1 - c), me).wait_recv()
        for cp in first + passed:
            cp.wait_send()
        mine.wait()

    return pl.pallas_call(
        body, name=name, out_shape=_sds((N_DEV,) + shard.shape, shard.dtype), in_specs=[ANY], out_specs=ANY,
        scratch_shapes=[pltpu.SemaphoreType.DMA((N_DEV - 1,)), pltpu.SemaphoreType.DMA((N_DEV - 1,)), pltpu.SemaphoreType.DMA(())],
    )(shard)


def _exchange_sibling(name, part):
    def body(p_ref, r_ref, send_sems, recv_sems):
        x, y, c = _place()
        copies = [pltpu.make_async_remote_copy(
            src_ref=p_ref.at[2 * j + (1 - c)], dst_ref=r_ref.at[j], send_sem=send_sems.at[j], recv_sem=recv_sems.at[j],
            device_id=(x, y, 1 - c), device_id_type=MESH) for j in range(N_CHIP)]
        for cp in copies:
            cp.start()
        for cp in copies:
            cp.wait()

    return pl.pallas_call(
        body, name=name, out_shape=_sds((N_CHIP,) + part.shape[1:], part.dtype), in_specs=[ANY], out_specs=ANY,
        scratch_shapes=[pltpu.SemaphoreType.DMA((N_CHIP,)), pltpu.SemaphoreType.DMA((N_CHIP,))],
    )(part)


def _add_sibling(name, part, recv, core):
    _, r, c = part.shape
    tr = _tile(r, max(16, 2 * ELEM_BLOCK_BYTES // (2 * c)), 16)

    def body(core_ref, p_ref, r_ref, o_ref):
        o_ref[...] = (p_ref[...].astype(F32) + r_ref[...].astype(F32)).astype(BF16)

    blk = pl.BlockSpec((None, tr, c), lambda j, i, core_ref: (j, i, 0))
    return pl.pallas_call(
        body, name=name,
        grid_spec=pltpu.PrefetchScalarGridSpec(
            num_scalar_prefetch=1, grid=(N_CHIP, r // tr),
            in_specs=[pl.BlockSpec((None, tr, c), lambda j, i, core_ref: (2 * j + core_ref[0], i, 0)), blk], out_specs=blk),
        out_shape=_sds((N_CHIP, r, c), BF16), compiler_params=_params(("parallel", "parallel")),
    )(core, part, recv)


def _adamw_math(w, g, m, v):
    m = ADAM_B1 * m + (1.0 - ADAM_B1) * g
    v = ADAM_B2 * v + (1.0 - ADAM_B2) * (g * g)
    m_hat = m / (1.0 - ADAM_B1 ** ADAM_STEP)
    v_hat = v / (1.0 - ADAM_B2 ** ADAM_STEP)
    delta = -ADAM_LR * (m_hat / (jnp.sqrt(v_hat) + ADAM_EPS) + ADAM_WD * w)
    return delta, m, v


def _adamw_big(name, s, recv, chip, w, m, v):
    r, c = w.shape
    tr = _tile(r, max(16, ELEM_BLOCK_BYTES // (4 * c)), 16)

    def body(chip_ref, s_ref, r_ref, w_ref, m_ref, v_ref, g_out, d_out, m_out, v_out):
        g = s_ref[...].astype(F32)
        for k in range(N_CHIP - 1):
            g = g + r_ref[k].astype(F32)
        delta, m_new, v_new = _adamw_math(w_ref[...], g, m_ref[...], v_ref[...])
        g_out[...] = g
        d_out[...] = delta
        m_out[...] = m_new
        v_out[...] = v_new

    blk = pl.BlockSpec((tr, c), lambda i, chip_ref: (i, 0))
    return pl.pallas_call(
        body, name=name,
        grid_spec=pltpu.PrefetchScalarGridSpec(
            num_scalar_prefetch=1, grid=(r // tr,),
            in_specs=[pl.BlockSpec((None, tr, c), lambda i, chip_ref: (chip_ref[0], i, 0)),
                      pl.BlockSpec((N_CHIP - 1, tr, c), lambda i, chip_ref: (0, i, 0)), blk, blk, blk],
            out_specs=[blk, blk, blk, blk]),
        out_shape=[_sds((r, c), F32)] * 4, compiler_params=_params(("parallel",)),
    )(chip, s, recv, w, m, v)


def _sum_parts(name, parts):
    n, r, c = parts.shape
    tr = _tile(r, max(8, ELEM_BLOCK_BYTES // (4 * c)), 8)

    def body(p_ref, o_ref):
        acc = p_ref[0]
        for k in range(1, n):
            acc = acc + p_ref[k]
        o_ref[...] = acc

    return pl.pallas_call(body, name=name, grid=(r // tr,), in_specs=[pl.BlockSpec((n, tr, c), lambda i: (0, i, 0))],
                          out_specs=pl.BlockSpec((tr, c), lambda i: (i, 0)), out_shape=_sds((r, c), F32),
                          compiler_params=_params(("parallel",)))(parts)


def _adamw_small(name, ws, gs, ms, vs):
    n = len(ws)

    def body(*refs):
        ins, outs = refs[:4 * n], refs[4 * n:]
        for i in range(n):
            delta, m_new, v_new = _adamw_math(ins[i][...], ins[n + i][...], ins[2 * n + i][...], ins[3 * n + i][...])
            outs[i][...] = delta
            outs[n + i][...] = m_new
            outs[2 * n + i][...] = v_new

    shapes = [_sds(w.shape, F32) for w in ws]
    return pl.pallas_call(body, name=name, out_shape=shapes * 3)(*ws, *gs, *ms, *vs)


def _reduce_start(tag, part, core):
    from_sibling = _exchange_sibling(f"rs_sibling_{tag}", part)
    summed = _add_sibling(f"rs_add_{tag}", part, from_sibling, core)
    return _chips_start(f"rs_chips_start_{tag}", summed)


def _pad_rows(a, rows):
    return jnp.pad(a, ((0, rows - a.shape[0]), (0, 0)))


def kernel(x, meta_tokens, ffn1_w_gu, ffn1_w_down, ln1_g, ln1_b, w_in, conv_w, pool_w, pool_scale, w_out, ln2_g, ln2_b, ffn2_w_gu, ffn2_w_down, ln3_g, ln3_b, loss_target, m_meta_tokens, m_ffn1_w_gu, m_ffn1_w_down, m_ln1_g, m_ln1_b, m_w_in, m_conv_w, m_pool_w, m_pool_scale, m_w_out, m_ln2_g, m_ln2_b, m_ffn2_w_gu, m_ffn2_w_down, m_ln3_g, m_ln3_b, v_meta_tokens, v_ffn1_w_gu, v_ffn1_w_down, v_ln1_g, v_ln1_b, v_w_in, v_conv_w, v_pool_w, v_pool_scale, v_w_out, v_ln2_g, v_ln2_b, v_ffn2_w_gu, v_ffn2_w_down, v_ln3_g, v_ln3_b):
    names = ["meta_tokens", "ffn1_w_gu", "ffn1_w_down", "ln1_g", "ln1_b", "w_in", "conv_w", "pool_w", "pool_scale", "w_out",
             "ln2_g", "ln2_b", "ffn2_w_gu", "ffn2_w_down", "ln3_g", "ln3_b"]
    w_of = dict(zip(names, [meta_tokens, ffn1_w_gu, ffn1_w_down, ln1_g, ln1_b, w_in, conv_w, pool_w, pool_scale, w_out,
                            ln2_g, ln2_b, ffn2_w_gu, ffn2_w_down, ln3_g, ln3_b]))
    m_of = dict(zip(names, [m_meta_tokens, m_ffn1_w_gu, m_ffn1_w_down, m_ln1_g, m_ln1_b, m_w_in, m_conv_w, m_pool_w, m_pool_scale,
                            m_w_out, m_ln2_g, m_ln2_b, m_ffn2_w_gu, m_ffn2_w_down, m_ln3_g, m_ln3_b]))
    v_of = dict(zip(names, [v_meta_tokens, v_ffn1_w_gu, v_ffn1_w_down, v_ln1_g, v_ln1_b, v_w_in, v_conv_w, v_pool_w, v_pool_scale,
                            v_w_out, v_ln2_g, v_ln2_b, v_ffn2_w_gu, v_ffn2_w_down, v_ln3_g, v_ln3_b]))

    n_seq, seq, d = x.shape
    seq_len = seq + N_META
    n_rows = n_seq * seq_len
    tp = -(-n_rows // ROW_ALIGN) * ROW_ALIGN
    c_conv = conv_w.shape[2] * N_DEV
    p_pool = pool_scale.shape[1]
    pg = pool_w.shape[3]
    assert c_conv == p_pool and p_pool == N_POOL_GROUPS * pg and POOL_WINDOWS == tuple(2 << g for g in range(N_POOL_GROUPS))
    assert (N_POOL_GROUPS * pg * pg) % d == 0 and pg % LANE == 0

    xi, yi, ci = _place()
    dev = 4 * xi + 2 * yi + ci
    core = jnp.reshape(ci, (1,)).astype(jnp.int32)
    chip = jnp.reshape(2 * xi + yi, (1,)).astype(jnp.int32)

    big = ["ffn1_w_gu", "ffn1_w_down", "w_in", "w_out", "ffn2_w_gu", "ffn2_w_down"]
    wcol = d // N_DEV
    conv_rows = 8
    small_local = jnp.concatenate([
        meta_tokens,
        pool_w[0].reshape(N_POOL_GROUPS * (pg // N_DEV), pg),
        jnp.pad(conv_w[0], ((0, conv_rows - CONV_K), (0, wcol - conv_w.shape[2]))),
    ], axis=0)
    shards = {n: w_of[n][0].astype(BF16) for n in big}
    shards["small"] = small_local
    started = {}

    def start(tag, which, after=()):
        per, token = _gather_start(f"ag_start_{tag}", [shards[n] for n in which], after)
        started.update(zip(which, per))
        return token

    def gathered(n, after, then_start=()):
        shard, land = _gather_wait(f"ag_wait_{n}", started[n], after)
        deps = (start(f"after_{n}", then_start, (shard,)),) if then_start else ()
        return _gather_finish(f"ag_finish_{n}", land, shard, deps)

    gather_token = start("first", ["small", "ffn1_w_gu"])
    small_all = gathered("small", gather_token)
    r0, r1 = N_META, N_META + N_POOL_GROUPS * (pg // N_DEV)
    meta_full = jnp.transpose(small_all[:, :r0], (1, 0, 2)).reshape(N_META, d)
    pool_w_full = jnp.transpose(small_all[:, r0:r1].reshape(N_DEV, N_POOL_GROUPS, pg // N_DEV, pg), (1, 0, 2, 3)).reshape(N_POOL_GROUPS, pg, pg)
    conv_w_full = jnp.transpose(small_all[:, r1:r1 + CONV_K, :conv_w.shape[2]], (1, 0, 2)).reshape(CONV_K, c_conv)
    pool_w_b = pool_w_full.astype(BF16)

    h0 = jnp.concatenate([jnp.broadcast_to(meta_full[None], (n_seq, N_META, d)), x], axis=1).reshape(n_rows, d)
    h0 = _pad_rows(h0, tp)
    h0_b = h0.astype(BF16)
    tgt = _pad_rows(jnp.pad(loss_target, ((0, 0), (N_META, 0), (0, 0))).reshape(n_rows, d), tp)

    wgu1 = gathered("ffn1_w_gu", h0_b, ["ffn1_w_down", "w_in"])
    gu1, act1 = _ffn_gu("ffn1_gu", h0_b, wgu1)
    wd1 = gathered("ffn1_w_down", act1, ["w_out", "ffn2_w_gu"]).reshape(N_CHIP, -1, d)
    pre1 = _ffn_down("ffn1_down", act1, wd1, h0)
    win_all = gathered("w_in", pre1)
    h1, h1_b = _ln_fwd("ln1", pre1, ln1_g, ln1_b)

    u = _proj_in("mix_in", h1_b, win_all)
    wout_all = gathered("w_out", u)
    y_conv = _conv_fwd("mix_conv", u, conv_w_full, n_rows, seq_len)
    dpool = _pool_fwd("mix_pool", u, 3 * c_conv, p_pool, pg, n_rows, seq_len)
    ypre, y_pool = _pool_mix("mix_pool_w", dpool, pool_w_b, pool_scale)
    y_mix = jnp.concatenate([y_conv, y_pool], axis=1)
    pre2 = _proj_out("mix_out", y_mix, wout_all, h1)
    wgu2 = gathered("ffn2_w_gu", pre2, ["ffn2_w_down"])
    h2, h2_b = _ln_fwd("ln2", pre2, ln2_g, ln2_b)

    gu2, act2 = _ffn_gu("ffn2_gu", h2_b, wgu2)
    wd2 = gathered("ffn2_w_down", act2).reshape(N_CHIP, -1, d)
    pre3 = _ffn_down("ffn2_down", act2, wd2, h2)

    dpre3, dpre3_b, d_ln3_g, d_ln3_b, sq = _ln_loss_bwd("ln3_loss", pre3, tgt, ln3_g, ln3_b, n_rows, seq_len)
    loss = lax.psum(0.5 * jnp.sum(sq) / d, ("x", "y", "c"))
    reducing = {}

    dgu2 = _ffn_bwd_dgu("ffn2_bwd_dgu", dpre3_b, wd2, gu2)
    g_wd2 = _ffn_bwd_wd("ffn2_bwd_wd", act2, dpre3_b)
    reducing["ffn2_w_down"], token = _reduce_start("ffn2_w_down", g_wd2.reshape(N_DEV, -1, d), core)
    g_wgu2 = _ffn_bwd_wgu("ffn2_bwd_wgu", h2_b, dgu2, (token,))
    reducing["ffn2_w_gu"], token = _reduce_start("ffn2_w_gu", g_wgu2, core)
    dh2 = _ffn_bwd_dh("ffn2_bwd_dh", dgu2, wgu2, dpre3, (token,))
    dpre2, dpre2_b, d_ln2_g, d_ln2_b = _ln_bwd("ln2_bwd", pre2, dh2, ln2_g)

    dy_mix = _proj_out_bwd_y("mix_out_bwd_y", dpre2_b, wout_all)
    g_wout = _proj_out_bwd_w("mix_out_bwd_w", y_mix, dpre2_b, N_DEV)
    reducing["w_out"], token = _reduce_start("w_out", g_wout, core)
    dyps, d_pool_scale = _pool_scale_bwd("mix_pool_scale_bwd", dy_mix, 1, ypre, pool_scale)
    dd = _pool_mix_bwd_in("mix_pool_w_bwd_in", dyps, pool_w_b)
    d_pool_w = _pool_mix_bwd_w("mix_pool_w_bwd_w", dpool, dyps, pg)
    du_pool = _pool_bwd("mix_pool_bwd", dd, pg, n_rows, seq_len)
    du_b, du_c, du_x, d_conv_w = _conv_bwd("mix_conv_bwd", u, conv_w_full, dy_mix, n_rows, seq_len)
    du = jnp.concatenate([du_b, du_c, du_x, du_pool], axis=1)
    g_win = _proj_in_bwd_w("mix_in_bwd_w", h1_b, du, N_DEV, (token,))
    reducing["w_in"], token = _reduce_start("w_in", g_win, core)
    dh1 = _proj_in_bwd_h("mix_in_bwd_h", du, win_all, dpre2, (token,))
    dpre1, dpre1_b, d_ln1_g, d_ln1_b = _ln_bwd("ln1_bwd", pre1, dh1, ln1_g)

    dgu1 = _ffn_bwd_dgu("ffn1_bwd_dgu", dpre1_b, wd1, gu1)
    g_wd1 = _ffn_bwd_wd("ffn1_bwd_wd", act1, dpre1_b)
    reducing["ffn1_w_down"], token = _reduce_start("ffn1_w_down", g_wd1.reshape(N_DEV, -1, d), core)
    g_wgu1 = _ffn_bwd_wgu("ffn1_bwd_wgu", h0_b, dgu1, (token,))
    reducing["ffn1_w_gu"], token = _reduce_start("ffn1_w_gu", g_wgu1, core)
    dh0 = _ffn_bwd_dh("ffn1_bwd_dh", dgu1, wgu1, dpre1, (token,))

    dh0_seq = dh0[:n_rows].reshape(n_seq, seq_len, d)
    grad_x = dh0_seq[:, N_META:]
    d_meta = jnp.sum(dh0_seq[:, :N_META], axis=0)

    grads, deltas, new_m, new_v = {}, {}, {}, {}
    for n in ["ffn2_w_down", "ffn2_w_gu", "w_out", "w_in", "ffn1_w_down", "ffn1_w_gu"]:
        summed, from_chips = _chips_wait(f"rs_chips_wait_{n}", reducing[n], dh0)
        g, dl, mm, vv = _adamw_big(f"adamw_{n}", summed, from_chips, chip, w_of[n][0], m_of[n][0], v_of[n][0])
        grads[n], deltas[n], new_m[n], new_v[n] = g[None], dl[None], mm[None], vv[None]

    def widen(a):
        return jnp.pad(a, ((0, 0), (0, d - a.shape[1])))

    small_part = jnp.concatenate([
        d_ln1_g, d_ln1_b, d_ln2_g, d_ln2_b, d_ln3_g, d_ln3_b, widen(d_pool_scale), widen(d_conv_w), d_meta,
        d_pool_w.reshape(-1, d)], axis=0)
    n_small_rows = small_part.shape[0]
    small_part = _pad_rows(small_part, -(-n_small_rows // 8) * 8)
    small_sum = _sum_parts("small_sum", _all_gather("ag_small_grads", small_part))
    o = 7 + CONV_K
    g_small = {
        "ln1_g": small_sum[0:1], "ln1_b": small_sum[1:2], "ln2_g": small_sum[2:3], "ln2_b": small_sum[3:4],
        "ln3_g": small_sum[4:5], "ln3_b": small_sum[5:6], "pool_scale": small_sum[6:7, :p_pool],
        "conv_w": lax.dynamic_slice_in_dim(small_sum[7:o, :c_conv], dev * (c_conv // N_DEV), c_conv // N_DEV, axis=1)[None],
        "meta_tokens": lax.dynamic_slice_in_dim(small_sum[o:o + N_META], dev * wcol, wcol, axis=1),
        "pool_w": lax.dynamic_slice_in_dim(small_sum[o + N_META:n_small_rows].reshape(N_POOL_GROUPS, pg, pg),
                                           dev * (pg // N_DEV), pg // N_DEV, axis=1)[None],
    }
    small = ["meta_tokens", "ln1_g", "ln1_b", "conv_w", "pool_w", "pool_scale", "ln2_g", "ln2_b", "ln3_g", "ln3_b"]

    def flat(a):
        return a.reshape(-1, a.shape[-1])

    outs = _adamw_small("adamw_small", [flat(w_of[n]) for n in small], [flat(g_small[n]) for n in small],
                        [flat(m_of[n]) for n in small], [flat(v_of[n]) for n in small])
    ns = len(small)
    for i, n in enumerate(small):
        shape = w_of[n].shape
        grads[n] = g_small[n].reshape(shape)
        deltas[n], new_m[n], new_v[n] = outs[i].reshape(shape), outs[ns + i].reshape(shape), outs[2 * ns + i].reshape(shape)

    return (loss, grad_x, *[grads[n] for n in names], *[deltas[n] for n in names],
            *[new_m[n] for n in names], *[new_v[n] for n in names])
```

```python
import functools

import jax
import jax.numpy as jnp
from jax import lax
from jax.experimental import pallas as pl
from jax.experimental.pallas import tpu as pltpu

N_DEV = 8
N_CHIP = 4
N_META = 16
CONV_K = 3
POOL_WINDOWS = (2, 4, 8, 16)
N_POOL_GROUPS = len(POOL_WINDOWS)
LN_EPS = 1e-5
DEPTH = 1
ALPHA = (2.0 * DEPTH) ** 0.25
ADAM_LR = 0.001
ADAM_B1 = 0.9
ADAM_B2 = 0.999
ADAM_EPS = 1e-08
ADAM_WD = 0.01
ADAM_STEP = 10

V7X_VMEM_BYTES = 64 * 1024 * 1024
VMEM_LIMIT = V7X_VMEM_BYTES - 6 * 1024 * 1024
LANE = 128
ROW_ALIGN = 3 * LANE
TM_BIG = 1408
TM_WIDE = 704
TK = 512
TK_TOKENS = 1408
TN = 1024
TR_LN = 128
ELEM_BLOCK_BYTES = 1 << 20
TC_MIX = LANE
EPILOGUE_ROWS = 64

NN = (((1,), (0,)), ((), ()))
NT = (((1,), (1,)), ((), ()))
TN_DIMS = (((0,), (0,)), ((), ()))
MESH = pl.DeviceIdType.MESH
BF16 = jnp.bfloat16
F32 = jnp.float32
ANY = pl.BlockSpec(memory_space=pl.ANY)
HBM = pl.BlockSpec(memory_space=pltpu.HBM)
SEM = pl.BlockSpec(memory_space=pltpu.SEMAPHORE)
EFFECT = pltpu.SideEffectType.DATAFLOW_SIDE_EFFECTING


def _tile(n, target, mult):
    best = None
    for t in range(mult, min(n, target) + 1, mult):
        if n % t == 0:
            best = t
    return n if best is None else best


def _params(sem):
    return pltpu.CompilerParams(dimension_semantics=sem, vmem_limit_bytes=VMEM_LIMIT)


def _sds(shape, dtype):
    return jax.ShapeDtypeStruct(shape, dtype)


def _row_chunks(n_rows, fn):
    ch = _tile(n_rows, EPILOGUE_ROWS, 16)

    def step(i, carry):
        fn(pl.ds(pl.multiple_of(i * ch, ch), ch))
        return carry

    lax.fori_loop(0, n_rows // ch, step, 0)


def _mm(name, grid, dims, ab, ab_specs, extras, extra_specs, out_shape, out_specs, acc_shape, epilogue, deps=()):
    nk = grid[-1]
    n_extra = len(extras)
    n_in = 2 + n_extra + len(deps)
    n_out = len(out_shape)
    kax = len(grid) - 1

    def body(*refs):
        a_ref, b_ref = refs[0], refs[1]
        ex = refs[2:2 + n_extra]
        outs = refs[n_in:n_in + n_out]
        if nk == 1:
            epilogue(lax.dot_general(a_ref[...], b_ref[...], dims, preferred_element_type=F32), ex, outs, slice(None))
            return
        acc = refs[-1]
        k = pl.program_id(kax)

        @pl.when(k == 0)
        def _():
            acc[...] = jnp.zeros_like(acc)

        acc[...] += lax.dot_general(a_ref[...], b_ref[...], dims, preferred_element_type=F32)

        @pl.when(k == nk - 1)
        def _():
            _row_chunks(acc_shape[0], lambda rows: epilogue(acc[rows, :], ex, outs, rows))

    scratch = [] if nk == 1 else [pltpu.VMEM(acc_shape, F32)]
    sem = ("parallel",) * kax + ("arbitrary",)
    return pl.pallas_call(
        body, name=name, grid=grid, in_specs=list(ab_specs) + list(extra_specs) + [ANY] * len(deps), out_specs=list(out_specs),
        out_shape=list(out_shape), scratch_shapes=scratch, compiler_params=_params(sem),
    )(*ab, *extras, *deps)


def _silu_parts(g):
    s = 1.0 / (1.0 + jnp.exp(-g))
    return s, g * s


def _ffn_gu(name, h_b, wgu_all):
    tp, d = h_b.shape
    ns, _, ng = wgu_all.shape
    half = ns // 2
    tm, tk = _tile(tp, TM_WIDE, 16), _tile(d, TK, LANE)
    grid = (tp // tm, half, d // tk)
    nk = grid[-1]

    def body(h_ref, wg_ref, wu_ref, gu_ref, act_ref, acc_g, acc_u):
        k = pl.program_id(2)

        @pl.when(k == 0)
        def _():
            acc_g[...] = jnp.zeros_like(acc_g)
            acc_u[...] = jnp.zeros_like(acc_u)

        acc_g[...] += jnp.dot(h_ref[...], wg_ref[...], preferred_element_type=F32)
        acc_u[...] += jnp.dot(h_ref[...], wu_ref[...], preferred_element_type=F32)

        @pl.when(k == nk - 1)
        def _():
            def finish(rows):
                g = acc_g[rows, :]
                u = acc_u[rows, :]
                _, silu = _silu_parts(g)
                gu_ref[0, rows, :] = g.astype(BF16)
                gu_ref[1, rows, :] = u.astype(BF16)
                act_ref[rows, :] = (silu * u).astype(BF16)

            _row_chunks(tm, finish)

    return pl.pallas_call(
        body, name=name, grid=grid,
        in_specs=[pl.BlockSpec((tm, tk), lambda m, s, k: (m, k)),
                  pl.BlockSpec((None, tk, ng), lambda m, s, k: (s, k, 0)),
                  pl.BlockSpec((None, tk, ng), lambda m, s, k: (s + half, k, 0))],
        out_specs=[pl.BlockSpec((None, 2, tm, ng), lambda m, s, k: (s, 0, m, 0)),
                   pl.BlockSpec((None, tm, ng), lambda m, s, k: (s, m, 0))],
        out_shape=[_sds((half, 2, tp, ng), BF16), _sds((half, tp, ng), BF16)],
        scratch_shapes=[pltpu.VMEM((tm, ng), F32), pltpu.VMEM((tm, ng), F32)],
        compiler_params=_params(("parallel", "parallel", "arbitrary")),
    )(h_b, wgu_all, wgu_all)


def _ffn_down(name, act, wd4, h):
    ns, tp, ng = act.shape
    d = wd4.shape[2]
    tm, tn = _tile(tp, TM_WIDE, 16), _tile(d, TN, LANE)

    def epi(acc, ex, outs, rows):
        outs[0][rows, :] = ALPHA * ex[0][rows, :] + 0.5 * acc

    return _mm(name, (tp // tm, d // tn, ns), NN, (act, wd4),
               [pl.BlockSpec((None, tm, ng), lambda m, n, s: (s, m, 0)),
                pl.BlockSpec((None, ng, tn), lambda m, n, s: (s, 0, n))],
               (h,), [pl.BlockSpec((tm, tn), lambda m, n, s: (m, n))],
               [_sds((tp, d), F32)], [pl.BlockSpec((tm, tn), lambda m, n, s: (m, n))], (tm, tn), epi)[0]


def _ffn_bwd_dgu(name, dp_b, wd4, gu):
    tp, d = dp_b.shape
    ns, ng, _ = wd4.shape
    tm, tk = _tile(tp, TM_WIDE, 16), _tile(d, TK, LANE)

    def epi(acc, ex, outs, rows):
        g = ex[0][0, rows, :].astype(F32)
        u = ex[0][1, rows, :].astype(F32)
        da = 0.5 * acc
        s, silu = _silu_parts(g)
        outs[0][0, rows, :] = (da * u * (s + silu * (1.0 - s))).astype(BF16)
        outs[0][1, rows, :] = (da * silu).astype(BF16)

    return _mm(name, (tp // tm, ns, d // tk), NT, (dp_b, wd4),
               [pl.BlockSpec((tm, tk), lambda m, s, k: (m, k)),
                pl.BlockSpec((None, ng, tk), lambda m, s, k: (s, 0, k))],
               (gu,), [pl.BlockSpec((None, 2, tm, ng), lambda m, s, k: (s, 0, m, 0))],
               [_sds((ns, 2, tp, ng), BF16)], [pl.BlockSpec((None, 2, tm, ng), lambda m, s, k: (s, 0, m, 0))],
               (tm, ng), epi)[0]


def _ffn_bwd_wd(name, act, dp_b):
    ns, tp, ng = act.shape
    d = dp_b.shape[1]
    tkt, tn = _tile(tp, TK_TOKENS, LANE), _tile(d, TN, LANE)

    def epi(acc, ex, outs, rows):
        outs[0][rows, :] = (0.5 * acc).astype(BF16)

    return _mm(name, (ns, d // tn, tp // tkt), TN_DIMS, (act, dp_b),
               [pl.BlockSpec((None, tkt, ng), lambda s, n, t: (s, t, 0)),
                pl.BlockSpec((tkt, tn), lambda s, n, t: (t, n))],
               (), [], [_sds((ns, ng, d), BF16)], [pl.BlockSpec((None, ng, tn), lambda s, n, t: (s, 0, n))],
               (ng, tn), epi)[0]


def _ffn_bwd_wgu(name, h_b, dgu, deps=()):
    tp, d = h_b.shape
    ns, _, _, ng = dgu.shape
    tkt, tmd = _tile(tp, TK_TOKENS, LANE), _tile(d, TN, LANE)

    def epi(acc, ex, outs, rows):
        outs[0][rows, :] = acc.astype(BF16)

    return _mm(name, (ns, 2, d // tmd, tp // tkt), TN_DIMS, (h_b, dgu),
               [pl.BlockSpec((tkt, tmd), lambda s, j, i, t: (t, i)),
                pl.BlockSpec((None, None, tkt, ng), lambda s, j, i, t: (s, j, t, 0))],
               (), [], [_sds((2 * ns, d, ng), BF16)],
               [pl.BlockSpec((None, tmd, ng), lambda s, j, i, t: (j * ns + s, i, 0))], (tmd, ng), epi, deps)[0]


def _ffn_bwd_dh(name, dgu, wgu_all, dp, deps=()):
    ns, _, tp, ng = dgu.shape
    d = wgu_all.shape[1]
    tm, tn = _tile(tp, TM_WIDE, 16), _tile(d, TN, LANE)

    def epi(acc, ex, outs, rows):
        outs[0][rows, :] = ALPHA * ex[0][rows, :] + acc

    return _mm(name, (tp // tm, d // tn, 2 * ns), NT, (dgu, wgu_all),
               [pl.BlockSpec((None, None, tm, ng), lambda m, n, j: (j % ns, j // ns, m, 0)),
                pl.BlockSpec((None, tn, ng), lambda m, n, j: (j, n, 0))],
               (dp,), [pl.BlockSpec((tm, tn), lambda m, n, j: (m, n))],
               [_sds((tp, d), F32)], [pl.BlockSpec((tm, tn), lambda m, n, j: (m, n))], (tm, tn), epi, deps)[0]


def _ln_stats(x):
    mu = jnp.mean(x, axis=-1, keepdims=True)
    xc = x - mu
    var = jnp.mean(xc * xc, axis=-1, keepdims=True)
    rstd = lax.rsqrt(var + LN_EPS)
    return xc * rstd, rstd


def _ln_bwd_rows(dy, xhat, rstd, g):
    dxh = dy * g
    m1 = jnp.mean(dxh, axis=-1, keepdims=True)
    m2 = jnp.mean(dxh * xhat, axis=-1, keepdims=True)
    return rstd * (dxh - m1 - xhat * m2)


def _ln_fwd(name, pre, g, b):
    tp, d = pre.shape
    tr = _tile(tp, TR_LN, 16)

    def body(x_ref, g_ref, b_ref, y_ref, yb_ref):
        xhat, _ = _ln_stats(x_ref[...])
        y = xhat * g_ref[...] + b_ref[...]
        y_ref[...] = y
        yb_ref[...] = y.astype(BF16)

    row = pl.BlockSpec((tr, d), lambda i: (i, 0))
    vec = pl.BlockSpec((1, d), lambda i: (0, 0))
    return pl.pallas_call(body, name=name, grid=(tp // tr,), in_specs=[row, vec, vec], out_specs=[row, row],
                          out_shape=[_sds((tp, d), F32), _sds((tp, d), BF16)],
                          compiler_params=_params(("parallel",)))(pre, g, b)


def _accumulate(i, ref, val):
    @pl.when(i == 0)
    def _():
        ref[...] = val

    @pl.when(i > 0)
    def _():
        ref[...] += val


def _ln_bwd(name, pre, dy, g):
    tp, d = pre.shape
    tr = _tile(tp, TR_LN, 16)

    def body(x_ref, dy_ref, g_ref, dx_ref, dxb_ref, dg_ref, db_ref):
        i = pl.program_id(0)
        xhat, rstd = _ln_stats(x_ref[...])
        dy = dy_ref[...]
        dx = _ln_bwd_rows(dy, xhat, rstd, g_ref[...])
        dx_ref[...] = dx
        dxb_ref[...] = dx.astype(BF16)
        _accumulate(i, dg_ref, jnp.sum(dy * xhat, axis=0, keepdims=True))
        _accumulate(i, db_ref, jnp.sum(dy, axis=0, keepdims=True))

    row = pl.BlockSpec((tr, d), lambda i: (i, 0))
    vec = pl.BlockSpec((1, d), lambda i: (0, 0))
    return pl.pallas_call(body, name=name, grid=(tp // tr,), in_specs=[row, row, vec], out_specs=[row, row, vec, vec],
                          out_shape=[_sds((tp, d), F32), _sds((tp, d), BF16), _sds((1, d), F32), _sds((1, d), F32)],
                          compiler_params=_params(("arbitrary",)))(pre, dy, g)


def _ln_loss_bwd(name, pre, tgt, g, b, n_rows, seq_len):
    tp, d = pre.shape
    tr = _tile(tp, TR_LN, 16)
    n_seq = n_rows // seq_len

    def body(x_ref, t_ref, g_ref, b_ref, dx_ref, dxb_ref, dg_ref, db_ref, sq_ref):
        i = pl.program_id(0)
        xhat, rstd = _ln_stats(x_ref[...])
        gain = g_ref[...]
        y = xhat * gain + b_ref[...]
        r = i * tr + lax.broadcasted_iota(jnp.int32, (tr, 1), 0)
        pos = r
        for s in range(1, n_seq):
            pos = jnp.where(r >= s * seq_len, r - s * seq_len, pos)
        live = jnp.logical_and(r < n_rows, pos >= N_META)
        err = jnp.where(live, y - t_ref[...], 0.0)
        dy = err * (1.0 / d)
        dx = _ln_bwd_rows(dy, xhat, rstd, gain)
        dx_ref[...] = dx
        dxb_ref[...] = dx.astype(BF16)
        _accumulate(i, dg_ref, jnp.sum(dy * xhat, axis=0, keepdims=True))
        _accumulate(i, db_ref, jnp.sum(dy, axis=0, keepdims=True))
        _accumulate(i, sq_ref, jnp.sum(err * err, axis=0, keepdims=True))

    row = pl.BlockSpec((tr, d), lambda i: (i, 0))
    vec = pl.BlockSpec((1, d), lambda i: (0, 0))
    return pl.pallas_call(
        body, name=name, grid=(tp // tr,), in_specs=[row, row, vec, vec], out_specs=[row, row, vec, vec, vec],
        out_shape=[_sds((tp, d), F32), _sds((tp, d), BF16), _sds((1, d), F32), _sds((1, d), F32), _sds((1, d), F32)],
        compiler_params=_params(("arbitrary",)))(pre, tgt, g, b)


def _proj_in(name, h_b, win_all):
    tp, d = h_b.shape
    ns, _, ni = win_all.shape
    tm, tk = _tile(tp, TM_BIG, 16), _tile(d, TK, LANE)

    def epi(acc, ex, outs, rows):
        outs[0][rows, :] = acc

    return _mm(name, (tp // tm, ns, d // tk), NN, (h_b, win_all),
               [pl.BlockSpec((tm, tk), lambda m, j, k: (m, k)),
                pl.BlockSpec((None, tk, ni), lambda m, j, k: (j, k, 0))],
               (), [], [_sds((tp, ns * ni), F32)], [pl.BlockSpec((tm, ni), lambda m, j, k: (m, j))], (tm, ni), epi)[0]


def _positions(tp, n_rows, seq_len):
    r = lax.broadcasted_iota(jnp.int32, (tp, 1), 0)
    pos = r
    for s in range(1, n_rows // seq_len):
        pos = jnp.where(r >= s * seq_len, r - s * seq_len, pos)
    return pos


def _shift_down(x, s, pos):
    return jnp.where(pos >= s, pltpu.roll(x, s, 0), 0.0)


def _shift_up(x, s, pos, seq_len):
    return jnp.where(pos + s < seq_len, pltpu.roll(x, x.shape[0] - s, 0), 0.0)


def _conv_fwd(name, u, conv_w, n_rows, seq_len):
    tp = u.shape[0]
    c = conv_w.shape[1]
    tc = _tile(c, TC_MIX, LANE)
    nb = c // tc

    def body(gb_ref, gc_ref, xi_ref, w_ref, y_ref):
        pos = _positions(tp, n_rows, seq_len)
        v = gc_ref[...] * xi_ref[...]
        w = w_ref[...]
        y = _shift_down(v, 2, pos) * w[0:1]
        y = y + _shift_down(v, 1, pos) * w[1:2]
        y = y + v * w[2:3]
        y_ref[...] = (gb_ref[...] * y).astype(BF16)

    col = lambda off: pl.BlockSpec((tp, tc), lambda i: (0, off + i))
    return pl.pallas_call(body, name=name, grid=(nb,), in_specs=[col(0), col(nb), col(2 * nb), pl.BlockSpec((CONV_K, tc), lambda i: (0, i))],
                          out_specs=pl.BlockSpec((tp, tc), lambda i: (0, i)), out_shape=_sds((tp, c), BF16),
                          compiler_params=_params(("parallel",)))(u, u, u, conv_w)


def _conv_bwd(name, u, conv_w, dy, n_rows, seq_len):
    tp = u.shape[0]
    c = conv_w.shape[1]
    tc = _tile(c, TC_MIX, LANE)
    nb = c // tc

    def body(gb_ref, gc_ref, xi_ref, w_ref, dy_ref, dgb_ref, dgc_ref, dxi_ref, dw_ref):
        pos = _positions(tp, n_rows, seq_len)
        gc, xi = gc_ref[...], xi_ref[...]
        v = gc * xi
        w = w_ref[...]
        v2, v1 = _shift_down(v, 2, pos), _shift_down(v, 1, pos)
        conv = v2 * w[0:1]
        conv = conv + v1 * w[1:2]
        conv = conv + v * w[2:3]
        dyc = dy_ref[...]
        dgb_ref[...] = (dyc * conv).astype(BF16)
        dconv = dyc * gb_ref[...]
        dv = dconv * w[2:3] + _shift_up(dconv, 1, pos, seq_len) * w[1:2] + _shift_up(dconv, 2, pos, seq_len) * w[0:1]
        dgc_ref[...] = (dv * xi).astype(BF16)
        dxi_ref[...] = (dv * gc).astype(BF16)
        dw_ref[0:1, :] = jnp.sum(dconv * v2, axis=0, keepdims=True)
        dw_ref[1:2, :] = jnp.sum(dconv * v1, axis=0, keepdims=True)
        dw_ref[2:3, :] = jnp.sum(dconv * v, axis=0, keepdims=True)

    col = lambda off: pl.BlockSpec((tp, tc), lambda i: (0, off + i))
    wspec = pl.BlockSpec((CONV_K, tc), lambda i: (0, i))
    return pl.pallas_call(body, name=name, grid=(nb,), in_specs=[col(0), col(nb), col(2 * nb), wspec, col(0)],
                          out_specs=[col(0), col(0), col(0), wspec],
                          out_shape=[_sds((tp, c), BF16)] * 3 + [_sds((CONV_K, c), F32)],
                          compiler_params=_params(("parallel",)))(u, u, u, conv_w, dy)


def _window_select(group, parts):
    out = parts[-1]
    for gi in range(len(parts) - 2, -1, -1):
        out = jnp.where(group == gi, parts[gi], out)
    return out


def _pool_fwd(name, u, col0, p, pg, n_rows, seq_len):
    tp = u.shape[0]
    tc = _tile(pg, TC_MIX, LANE)
    per_group = pg // tc

    def body(z_ref, d_ref):
        group = pl.program_id(0) // per_group
        pos = _positions(tp, n_rows, seq_len)
        z = z_ref[...]
        sums, s, w = [], z, 1
        for _ in POOL_WINDOWS:
            s = s + _shift_down(s, w, pos)
            w *= 2
            sums.append(s)
        total = _window_select(group, sums)
        count = jnp.minimum(pos + 1, 2 << group).astype(F32)
        d_ref[...] = (total / count - z).astype(BF16)

    return pl.pallas_call(body, name=name, grid=(p // tc,), in_specs=[pl.BlockSpec((tp, tc), lambda i: (0, col0 // tc + i))],
                          out_specs=pl.BlockSpec((tp, tc), lambda i: (0, i)), out_shape=_sds((tp, p), BF16),
                          compiler_params=_params(("parallel",)))(u)


def _pool_bwd(name, dd, pg, n_rows, seq_len):
    tp, p = dd.shape
    tc = _tile(pg, TC_MIX, LANE)
    per_group = pg // tc

    def body(dd_ref, dz_ref):
        group = pl.program_id(0) // per_group
        pos = _positions(tp, n_rows, seq_len)
        dd_v = dd_ref[...]
        count = jnp.minimum(pos + 1, 2 << group).astype(F32)
        sums, s, w = [], dd_v / count, 1
        for _ in POOL_WINDOWS:
            s = s + _shift_up(s, w, pos, seq_len)
            w *= 2
            sums.append(s)
        dz_ref[...] = (_window_select(group, sums) - dd_v).astype(BF16)

    spec = pl.BlockSpec((tp, tc), lambda i: (0, i))
    return pl.pallas_call(body, name=name, grid=(p // tc,), in_specs=[spec], out_specs=spec, out_shape=_sds((tp, p), BF16),
                          compiler_params=_params(("parallel",)))(dd)


def _pool_mix(name, dpool, pool_w_b, scale):
    tp, p = dpool.shape
    ng, pg, _ = pool_w_b.shape
    tm = _tile(tp, TM_BIG, 16)

    def epi(acc, ex, outs, rows):
        outs[0][rows, :] = acc
        outs[1][rows, :] = (acc * ex[0][...]).astype(BF16)

    blk = pl.BlockSpec((tm, pg), lambda m, g, k: (m, g))
    return _mm(name, (tp // tm, ng, 1), NN, (dpool, pool_w_b), [blk, pl.BlockSpec((None, pg, pg), lambda m, g, k: (g, 0, 0))],
               (scale,), [pl.BlockSpec((1, pg), lambda m, g, k: (0, g))],
               [_sds((tp, p), F32), _sds((tp, p), BF16)], [blk, blk], None, epi)


def _pool_scale_bwd(name, dy, col_block, ypre, scale):
    tp, p = ypre.shape
    tr = _tile(tp, TR_LN, 16)

    def body(dy_ref, yp_ref, s_ref, o_ref, ds_ref):
        i = pl.program_id(0)
        dyp = dy_ref[...]
        o_ref[...] = (dyp * s_ref[...]).astype(BF16)
        _accumulate(i, ds_ref, jnp.sum(dyp * yp_ref[...], axis=0, keepdims=True))

    row = pl.BlockSpec((tr, p), lambda i: (i, 0))
    vec = pl.BlockSpec((1, p), lambda i: (0, 0))
    return pl.pallas_call(body, name=name, grid=(tp // tr,), in_specs=[pl.BlockSpec((tr, p), lambda i: (i, col_block)), row, vec],
                          out_specs=[row, vec], out_shape=[_sds((tp, p), BF16), _sds((1, p), F32)],
                          compiler_params=_params(("arbitrary",)))(dy, ypre, scale)


def _pool_mix_bwd_in(name, dyps, pool_w_b):
    tp, p = dyps.shape
    ng, pg, _ = pool_w_b.shape
    tm = _tile(tp, TM_BIG, 16)

    def epi(acc, ex, outs, rows):
        outs[0][rows, :] = acc

    blk = pl.BlockSpec((tm, pg), lambda m, g, k: (m, g))
    return _mm(name, (tp // tm, ng, 1), NT, (dyps, pool_w_b), [blk, pl.BlockSpec((None, pg, pg), lambda m, g, k: (g, 0, 0))],
               (), [], [_sds((tp, p), F32)], [blk], None, epi)[0]


def _pool_mix_bwd_w(name, dpool, dyps, pg):
    tp, p = dpool.shape
    ng = p // pg
    tkt = _tile(tp, TM_BIG, LANE)

    def epi(acc, ex, outs, rows):
        outs[0][rows, :] = acc

    blk = pl.BlockSpec((tkt, pg), lambda g, t: (t, g))
    return _mm(name, (ng, tp // tkt), TN_DIMS, (dpool, dyps), [blk, blk], (), [],
               [_sds((ng, pg, pg), F32)], [pl.BlockSpec((None, pg, pg), lambda g, t: (g, 0, 0))], (pg, pg), epi)[0]


def _proj_out(name, y, wout_all, h):
    tp = y.shape[0]
    ns, ro, d = wout_all.shape
    tm, tn = _tile(tp, TM_BIG, 16), _tile(d, TN, LANE)

    def epi(acc, ex, outs, rows):
        outs[0][rows, :] = ALPHA * ex[0][rows, :] + acc

    mn = pl.BlockSpec((tm, tn), lambda m, n, j: (m, n))
    return _mm(name, (tp // tm, d // tn, ns), NN, (y, wout_all),
               [pl.BlockSpec((tm, ro), lambda m, n, j: (m, j)), pl.BlockSpec((None, ro, tn), lambda m, n, j: (j, 0, n))],
               (h,), [mn], [_sds((tp, d), F32)], [mn], (tm, tn), epi)[0]


def _proj_out_bwd_y(name, dp_b, wout_all):
    tp, d = dp_b.shape
    ns, ro, _ = wout_all.shape
    tm, tk = _tile(tp, TM_BIG, 16), _tile(d, TK, LANE)

    def epi(acc, ex, outs, rows):
        outs[0][rows, :] = acc

    return _mm(name, (tp // tm, ns, d // tk), NT, (dp_b, wout_all),
               [pl.BlockSpec((tm, tk), lambda m, j, k: (m, k)), pl.BlockSpec((None, ro, tk), lambda m, j, k: (j, 0, k))],
               (), [], [_sds((tp, ns * ro), F32)], [pl.BlockSpec((tm, ro), lambda m, j, k: (m, j))], (tm, ro), epi)[0]


def _proj_out_bwd_w(name, y, dp_b, ns):
    tp, c = y.shape
    d = dp_b.shape[1]
    ro = c // ns
    tkt, tn = _tile(tp, TM_BIG, LANE), _tile(d, 2 * TN, LANE)

    def epi(acc, ex, outs, rows):
        outs[0][rows, :] = acc.astype(BF16)

    return _mm(name, (ns, d // tn, tp // tkt), TN_DIMS, (y, dp_b),
               [pl.BlockSpec((tkt, ro), lambda j, n, t: (t, j)), pl.BlockSpec((tkt, tn), lambda j, n, t: (t, n))],
               (), [], [_sds((ns, ro, d), BF16)], [pl.BlockSpec((None, ro, tn), lambda j, n, t: (j, 0, n))], (ro, tn), epi)[0]


def _proj_in_bwd_w(name, h_b, du, ns, deps=()):
    tp, d = h_b.shape
    ni = du.shape[1] // ns
    tkt, tmd = _tile(tp, TK_TOKENS, LANE), _tile(d, 2 * TN, LANE)

    def epi(acc, ex, outs, rows):
        outs[0][rows, :] = acc.astype(BF16)

    return _mm(name, (ns, d // tmd, tp // tkt), TN_DIMS, (h_b, du),
               [pl.BlockSpec((tkt, tmd), lambda j, i, t: (t, i)), pl.BlockSpec((tkt, ni), lambda j, i, t: (t, j))],
               (), [], [_sds((ns, d, ni), BF16)], [pl.BlockSpec((None, tmd, ni), lambda j, i, t: (j, i, 0))], (tmd, ni), epi, deps)[0]


def _proj_in_bwd_h(name, du, win_all, dp, deps=()):
    tp = du.shape[0]
    ns, d, ni = win_all.shape
    tm, tn = _tile(tp, TM_BIG, 16), _tile(d, TN, LANE)

    def epi(acc, ex, outs, rows):
        outs[0][rows, :] = ALPHA * ex[0][rows, :] + acc

    mn = pl.BlockSpec((tm, tn), lambda m, n, j: (m, n))
    return _mm(name, (tp // tm, d // tn, ns), NT, (du, win_all),
               [pl.BlockSpec((tm, ni), lambda m, n, j: (m, j)), pl.BlockSpec((None, tn, ni), lambda m, n, j: (j, n, 0))],
               (dp,), [mn], [_sds((tp, d), F32)], [mn], (tm, tn), epi, deps)[0]


def _place():
    return lax.axis_index("x"), lax.axis_index("y"), lax.axis_index("c")


def _hbm(a):
    return pltpu.with_memory_space_constraint(a, pltpu.HBM)


def _token_shape():
    return _sds((8, LANE), F32)


def _gather_peers(x, y, c):
    return [(x, y, 1 - c), (1 - x, y, c), (x, 1 - y, c), (1 - x, 1 - y, c)]


def _into_slab(name, w, dev, dtype):
    r, c = w.shape
    tr = _tile(r, max(16, ELEM_BLOCK_BYTES // (4 * c)), 16 if dtype == BF16 else 8)

    def body(dev_ref, w_ref, o_ref):
        o_ref[...] = w_ref[...].astype(dtype)

    return pl.pallas_call(
        body, name=name,
        grid_spec=pltpu.PrefetchScalarGridSpec(
            num_scalar_prefetch=1, grid=(r // tr,), in_specs=[pl.BlockSpec((tr, c), lambda i, dev_ref: (i, 0))],
            out_specs=pl.BlockSpec((None, tr, c), lambda i, dev_ref: (dev_ref[0], i, 0))),
        out_shape=_sds((N_DEV, r, c), dtype), compiler_params=_params(("parallel",)),
    )(dev, w)


def _gather_start(name, lands, after=()):
    n = len(lands)
    n_peer = N_CHIP
    n_in = n + len(after)

    def body(*refs):
        land_refs = refs[:n]
        send_sems, recv_sems = refs[n_in:n_in + n], refs[n_in + n:n_in + 2 * n]
        token = refs[-1]
        x, y, c = _place()
        me = 4 * x + 2 * y + c
        for i in range(n):
            for k, peer in enumerate(_gather_peers(x, y, c)):
                pltpu.make_async_remote_copy(
                    src_ref=land_refs[i].at[me], dst_ref=land_refs[i].at[me], send_sem=send_sems[i].at[k],
                    recv_sem=recv_sems[i].at[k], device_id=peer, device_id_type=MESH).start()
        token[...] = jnp.zeros_like(token)

    sem = pltpu.SemaphoreType.DMA((n_peer,))
    out = pl.pallas_call(
        body, name=name,
        out_shape=[sem] * (2 * n) + [pltpu.HBM(l.shape, l.dtype) for l in lands] + [_token_shape()],
        in_specs=[HBM] * n + [ANY] * len(after),
        out_specs=[SEM] * (2 * n) + [HBM] * n + [pl.BlockSpec(memory_space=pltpu.VMEM)],
        input_output_aliases={i: 2 * n + i for i in range(n)},
        compiler_params=pltpu.CompilerParams(has_side_effects=EFFECT),
    )(*[_hbm(l) for l in lands], *after)
    per = [(out[i], out[n + i], out[2 * n + i]) for i in range(n)]
    return per, out[-1]


def _gather_wait(name, started, after):
    send_sems, recv_sems, land = started

    def body(land_ref, send_ref, recv_ref, after_ref, land_out):
        x, y, c = _place()
        for k, (px, py, pc) in enumerate(_gather_peers(x, y, c)):
            cp = pltpu.make_async_remote_copy(
                src_ref=land_ref.at[4 * x + 2 * y + c], dst_ref=land_ref.at[4 * px + 2 * py + pc], send_sem=send_ref.at[k],
                recv_sem=recv_ref.at[k], device_id=(px, py, pc), device_id_type=MESH)
            cp.wait_send()
            cp.wait_recv()

    return pl.pallas_call(
        body, name=name, out_shape=pltpu.HBM(land.shape, land.dtype),
        in_specs=(HBM, SEM, SEM, ANY), out_specs=HBM, input_output_aliases={0: 0},
        compiler_params=pltpu.CompilerParams(has_side_effects=EFFECT),
    )(land, send_sems, recv_sems, after)


def _gather_finish(name, land, deps=()):
    def body(land_ref, *rest):
        out_ref, send_sems, recv_sems = rest[len(deps):]
        x, y, c = _place()
        copies = []
        for k, (px, py) in enumerate([(1 - x, y), (x, 1 - y), (1 - x, 1 - y)]):
            slab = 4 * px + 2 * py + c
            copies.append(pltpu.make_async_remote_copy(
                src_ref=land_ref.at[slab], dst_ref=out_ref.at[slab], send_sem=send_sems.at[k], recv_sem=recv_sems.at[k],
                device_id=(x, y, 1 - c), device_id_type=MESH))
        for cp in copies:
            cp.start()
        for cp in copies:
            cp.wait()

    return pl.pallas_call(
        body, name=name, out_shape=_sds(land.shape, land.dtype), in_specs=[ANY] + [ANY] * len(deps), out_specs=ANY,
        input_output_aliases={0: 0},
        scratch_shapes=[pltpu.SemaphoreType.DMA((N_CHIP - 1,)), pltpu.SemaphoreType.DMA((N_CHIP - 1,))],
    )(land, *deps)


def _chips_start(name, s):
    n_peer = N_CHIP - 1

    def body(s_ref, land_ref, send_sems, recv_sems, s_out, land_out, token):
        x, y, c = _place()
        for k, (px, py) in enumerate([(1 - x, y), (x, 1 - y), (1 - x, 1 - y)]):
            pltpu.make_async_remote_copy(
                src_ref=s_ref.at[2 * px + py], dst_ref=land_ref.at[k], send_sem=send_sems.at[k], recv_sem=recv_sems.at[k],
                device_id=(px, py, c), device_id_type=MESH).start()
        token[...] = jnp.zeros_like(token)

    land = lax.empty((n_peer,) + s.shape[1:], s.dtype)
    sem = pltpu.SemaphoreType.DMA((n_peer,))
    out = pl.pallas_call(
        body, name=name,
        out_shape=[sem, sem, pltpu.HBM(s.shape, s.dtype), pltpu.HBM(land.shape, land.dtype), _token_shape()],
        in_specs=[HBM, HBM], out_specs=[SEM, SEM, HBM, HBM, pl.BlockSpec(memory_space=pltpu.VMEM)],
        input_output_aliases={0: 2, 1: 3}, compiler_params=pltpu.CompilerParams(has_side_effects=EFFECT),
    )(_hbm(s), _hbm(land))
    return out[:4], out[4]


def _chips_wait(name, started, after):
    send_sems, recv_sems, s, land = started

    def body(s_ref, land_ref, send_ref, recv_ref, after_ref, s_out, land_out):
        x, y, c = _place()
        for k, (px, py) in enumerate([(1 - x, y), (x, 1 - y), (1 - x, 1 - y)]):
            cp = pltpu.make_async_remote_copy(
                src_ref=s_ref.at[2 * px + py], dst_ref=land_ref.at[k], send_sem=send_ref.at[k], recv_sem=recv_ref.at[k],
                device_id=(px, py, c), device_id_type=MESH)
            cp.wait_send()
            cp.wait_recv()

    return pl.pallas_call(
        body, name=name, out_shape=(pltpu.HBM(s.shape, s.dtype), pltpu.HBM(land.shape, land.dtype)),
        in_specs=(HBM, HBM, SEM, SEM, ANY), out_specs=(HBM, HBM), input_output_aliases={0: 0, 1: 1},
        compiler_params=pltpu.CompilerParams(has_side_effects=EFFECT),
    )(s, land, send_sems, recv_sems, after)


def _all_gather(name, land):
    def body(land_ref, out_ref, send_sems, recv_sems):
        x, y, c = _place()
        me, sibling = (x, y, c), (x, y, 1 - c)
        chips = [(1 - x, y), (x, 1 - y), (1 - x, 1 - y)]

        def copy(k, block, to):
            slab = 4 * block[0] + 2 * block[1] + block[2]
            return pltpu.make_async_remote_copy(
                src_ref=land_ref.at[slab], dst_ref=out_ref.at[slab],
                send_sem=send_sems.at[k], recv_sem=recv_sems.at[k], device_id=to, device_id_type=MESH)

        first = [copy(0, me, sibling)] + [copy(1 + j, me, (*chip, c)) for j, chip in enumerate(chips)]
        for cp in first:
            cp.start()
        passed = [copy(4 + j, (*chip, c), sibling) for j, chip in enumerate(chips)]
        for j, chip in enumerate(chips):
            copy(1 + j, (*chip, c), me).wait_recv()
            passed[j].start()
        copy(0, sibling, me).wait_recv()
        for j, chip in enumerate(chips):
            copy(4 + j, (*chip, 1 - c), me).wait_recv()
        for cp in first + passed:
            cp.wait_send()

    return pl.pallas_call(
        body, name=name, out_shape=_sds(land.shape, land.dtype), in_specs=[ANY], out_specs=ANY, input_output_aliases={0: 0},
        scratch_shapes=[pltpu.SemaphoreType.DMA((N_DEV - 1,)), pltpu.SemaphoreType.DMA((N_DEV - 1,))],
    )(land)


def _exchange_sibling(name, part):
    def body(p_ref, r_ref, send_sems, recv_sems):
        x, y, c = _place()
        copies = [pltpu.make_async_remote_copy(
            src_ref=p_ref.at[2 * j + (1 - c)], dst_ref=r_ref.at[j], send_sem=send_sems.at[j], recv_sem=recv_sems.at[j],
            device_id=(x, y, 1 - c), device_id_type=MESH) for j in range(N_CHIP)]
        for cp in copies:
            cp.start()
        for cp in copies:
            cp.wait()

    return pl.pallas_call(
        body, name=name, out_shape=_sds((N_CHIP,) + part.shape[1:], part.dtype), in_specs=[ANY], out_specs=ANY,
        scratch_shapes=[pltpu.SemaphoreType.DMA((N_CHIP,)), pltpu.SemaphoreType.DMA((N_CHIP,))],
    )(part)


def _add_sibling(name, part, recv, core):
    _, r, c = part.shape
    tr = _tile(r, max(16, 2 * ELEM_BLOCK_BYTES // (2 * c)), 16)

    def body(core_ref, p_ref, r_ref, o_ref):
        o_ref[...] = (p_ref[...].astype(F32) + r_ref[...].astype(F32)).astype(BF16)

    blk = pl.BlockSpec((None, tr, c), lambda j, i, core_ref: (j, i, 0))
    return pl.pallas_call(
        body, name=name,
        grid_spec=pltpu.PrefetchScalarGridSpec(
            num_scalar_prefetch=1, grid=(N_CHIP, r // tr),
            in_specs=[pl.BlockSpec((None, tr, c), lambda j, i, core_ref: (2 * j + core_ref[0], i, 0)), blk], out_specs=blk),
        out_shape=_sds((N_CHIP, r, c), BF16), compiler_params=_params(("parallel", "parallel")),
    )(core, part, recv)


def _adamw_math(w, g, m, v):
    m = ADAM_B1 * m + (1.0 - ADAM_B1) * g
    v = ADAM_B2 * v + (1.0 - ADAM_B2) * (g * g)
    m_hat = m / (1.0 - ADAM_B1 ** ADAM_STEP)
    v_hat = v / (1.0 - ADAM_B2 ** ADAM_STEP)
    delta = -ADAM_LR * (m_hat / (jnp.sqrt(v_hat) + ADAM_EPS) + ADAM_WD * w)
    return delta, m, v


def _adamw_big(name, s, recv, chip, w, m, v):
    r, c = w.shape
    tr = _tile(r, max(16, ELEM_BLOCK_BYTES // (4 * c)), 16)

    def body(chip_ref, s_ref, r_ref, w_ref, m_ref, v_ref, g_out, d_out, m_out, v_out):
        g = s_ref[...].astype(F32)
        for k in range(N_CHIP - 1):
            g = g + r_ref[k].astype(F32)
        delta, m_new, v_new = _adamw_math(w_ref[...], g, m_ref[...], v_ref[...])
        g_out[...] = g
        d_out[...] = delta
        m_out[...] = m_new
        v_out[...] = v_new

    blk = pl.BlockSpec((tr, c), lambda i, chip_ref: (i, 0))
    return pl.pallas_call(
        body, name=name,
        grid_spec=pltpu.PrefetchScalarGridSpec(
            num_scalar_prefetch=1, grid=(r // tr,),
            in_specs=[pl.BlockSpec((None, tr, c), lambda i, chip_ref: (chip_ref[0], i, 0)),
                      pl.BlockSpec((N_CHIP - 1, tr, c), lambda i, chip_ref: (0, i, 0)), blk, blk, blk],
            out_specs=[blk, blk, blk, blk]),
        out_shape=[_sds((r, c), F32)] * 4, compiler_params=_params(("parallel",)),
    )(chip, s, recv, w, m, v)


def _sum_parts(name, parts):
    n, r, c = parts.shape
    tr = _tile(r, max(8, ELEM_BLOCK_BYTES // (4 * c)), 8)

    def body(p_ref, o_ref):
        acc = p_ref[0]
        for k in range(1, n):
            acc = acc + p_ref[k]
        o_ref[...] = acc

    return pl.pallas_call(body, name=name, grid=(r // tr,), in_specs=[pl.BlockSpec((n, tr, c), lambda i: (0, i, 0))],
                          out_specs=pl.BlockSpec((tr, c), lambda i: (i, 0)), out_shape=_sds((r, c), F32),
                          compiler_params=_params(("parallel",)))(parts)


def _adamw_small(name, ws, gs, ms, vs):
    n = len(ws)

    def body(*refs):
        ins, outs = refs[:4 * n], refs[4 * n:]
        for i in range(n):
            delta, m_new, v_new = _adamw_math(ins[i][...], ins[n + i][...], ins[2 * n + i][...], ins[3 * n + i][...])
            outs[i][...] = delta
            outs[n + i][...] = m_new
            outs[2 * n + i][...] = v_new

    shapes = [_sds(w.shape, F32) for w in ws]
    return pl.pallas_call(body, name=name, out_shape=shapes * 3)(*ws, *gs, *ms, *vs)


def _reduce_start(tag, part, core):
    from_sibling = _exchange_sibling(f"rs_sibling_{tag}", part)
    summed = _add_sibling(f"rs_add_{tag}", part, from_sibling, core)
    return _chips_start(f"rs_chips_start_{tag}", summed)


def _pad_rows(a, rows):
    return jnp.pad(a, ((0, rows - a.shape[0]), (0, 0)))


def kernel(x, meta_tokens, ffn1_w_gu, ffn1_w_down, ln1_g, ln1_b, w_in, conv_w, pool_w, pool_scale, w_out, ln2_g, ln2_b, ffn2_w_gu, ffn2_w_down, ln3_g, ln3_b, loss_target, m_meta_tokens, m_ffn1_w_gu, m_ffn1_w_down, m_ln1_g, m_ln1_b, m_w_in, m_conv_w, m_pool_w, m_pool_scale, m_w_out, m_ln2_g, m_ln2_b, m_ffn2_w_gu, m_ffn2_w_down, m_ln3_g, m_ln3_b, v_meta_tokens, v_ffn1_w_gu, v_ffn1_w_down, v_ln1_g, v_ln1_b, v_w_in, v_conv_w, v_pool_w, v_pool_scale, v_w_out, v_ln2_g, v_ln2_b, v_ffn2_w_gu, v_ffn2_w_down, v_ln3_g, v_ln3_b):
    names = ["meta_tokens", "ffn1_w_gu", "ffn1_w_down", "ln1_g", "ln1_b", "w_in", "conv_w", "pool_w", "pool_scale", "w_out",
             "ln2_g", "ln2_b", "ffn2_w_gu", "ffn2_w_down", "ln3_g", "ln3_b"]
    w_of = dict(zip(names, [meta_tokens, ffn1_w_gu, ffn1_w_down, ln1_g, ln1_b, w_in, conv_w, pool_w, pool_scale, w_out,
                            ln2_g, ln2_b, ffn2_w_gu, ffn2_w_down, ln3_g, ln3_b]))
    m_of = dict(zip(names, [m_meta_tokens, m_ffn1_w_gu, m_ffn1_w_down, m_ln1_g, m_ln1_b, m_w_in, m_conv_w, m_pool_w, m_pool_scale,
                            m_w_out, m_ln2_g, m_ln2_b, m_ffn2_w_gu, m_ffn2_w_down, m_ln3_g, m_ln3_b]))
    v_of = dict(zip(names, [v_meta_tokens, v_ffn1_w_gu, v_ffn1_w_down, v_ln1_g, v_ln1_b, v_w_in, v_conv_w, v_pool_w, v_pool_scale,
                            v_w_out, v_ln2_g, v_ln2_b, v_ffn2_w_gu, v_ffn2_w_down, v_ln3_g, v_ln3_b]))

    n_seq, seq, d = x.shape
    seq_len = seq + N_META
    n_rows = n_seq * seq_len
    tp = -(-n_rows // ROW_ALIGN) * ROW_ALIGN
    c_conv = conv_w.shape[2] * N_DEV
    p_pool = pool_scale.shape[1]
    pg = pool_w.shape[3]
    assert c_conv == p_pool and p_pool == N_POOL_GROUPS * pg and POOL_WINDOWS == tuple(2 << g for g in range(N_POOL_GROUPS))
    assert (N_POOL_GROUPS * pg * pg) % d == 0 and pg % LANE == 0

    xi, yi, ci = _place()
    dev_index = 4 * xi + 2 * yi + ci
    dev = jnp.reshape(dev_index, (1,)).astype(jnp.int32)
    core = jnp.reshape(ci, (1,)).astype(jnp.int32)
    chip = jnp.reshape(2 * xi + yi, (1,)).astype(jnp.int32)

    big = ["ffn1_w_gu", "ffn1_w_down", "w_in", "w_out", "ffn2_w_gu", "ffn2_w_down"]
    wcol = d // N_DEV
    conv_rows = 8
    small_local = jnp.concatenate([
        meta_tokens,
        pool_w[0].reshape(N_POOL_GROUPS * (pg // N_DEV), pg),
        jnp.pad(conv_w[0], ((0, conv_rows - CONV_K), (0, wcol - conv_w.shape[2]))),
    ], axis=0)
    lands = {n: _into_slab(f"slab_{n}", w_of[n][0], dev, BF16) for n in big}
    lands["small"] = _into_slab("slab_small", small_local, dev, F32)
    started = {}

    def start(tag, which, after=()):
        per, token = _gather_start(f"ag_start_{tag}", [lands[n] for n in which], after)
        started.update(zip(which, per))
        return token

    def gathered(n, after, then_start=()):
        land = _gather_wait(f"ag_wait_{n}", started[n], after)
        deps = (start(f"after_{n}", then_start, (land,)),) if then_start else ()
        return _gather_finish(f"ag_finish_{n}", land, deps)

    gather_token = start("first", ["small", "ffn1_w_gu"])
    small_all = gathered("small", gather_token)
    r0, r1 = N_META, N_META + N_POOL_GROUPS * (pg // N_DEV)
    meta_full = jnp.transpose(small_all[:, :r0], (1, 0, 2)).reshape(N_META, d)
    pool_w_full = jnp.transpose(small_all[:, r0:r1].reshape(N_DEV, N_POOL_GROUPS, pg // N_DEV, pg), (1, 0, 2, 3)).reshape(N_POOL_GROUPS, pg, pg)
    conv_w_full = jnp.transpose(small_all[:, r1:r1 + CONV_K, :conv_w.shape[2]], (1, 0, 2)).reshape(CONV_K, c_conv)
    pool_w_b = pool_w_full.astype(BF16)

    h0 = jnp.concatenate([jnp.broadcast_to(meta_full[None], (n_seq, N_META, d)), x], axis=1).reshape(n_rows, d)
    h0 = _pad_rows(h0, tp)
    h0_b = h0.astype(BF16)
    tgt = _pad_rows(jnp.pad(loss_target, ((0, 0), (N_META, 0), (0, 0))).reshape(n_rows, d), tp)

    wgu1 = gathered("ffn1_w_gu", h0_b, ["ffn1_w_down", "w_in"])
    gu1, act1 = _ffn_gu("ffn1_gu", h0_b, wgu1)
    wd1 = gathered("ffn1_w_down", act1, ["w_out", "ffn2_w_gu"]).reshape(N_CHIP, -1, d)
    pre1 = _ffn_down("ffn1_down", act1, wd1, h0)
    win_all = gathered("w_in", pre1)
    h1, h1_b = _ln_fwd("ln1", pre1, ln1_g, ln1_b)

    u = _proj_in("mix_in", h1_b, win_all)
    wout_all = gathered("w_out", u)
    y_conv = _conv_fwd("mix_conv", u, conv_w_full, n_rows, seq_len)
    dpool = _pool_fwd("mix_pool", u, 3 * c_conv, p_pool, pg, n_rows, seq_len)
    ypre, y_pool = _pool_mix("mix_pool_w", dpool, pool_w_b, pool_scale)
    y_mix = jnp.concatenate([y_conv, y_pool], axis=1)
    pre2 = _proj_out("mix_out", y_mix, wout_all, h1)
    wgu2 = gathered("ffn2_w_gu", pre2, ["ffn2_w_down"])
    h2, h2_b = _ln_fwd("ln2", pre2, ln2_g, ln2_b)

    gu2, act2 = _ffn_gu("ffn2_gu", h2_b, wgu2)
    wd2 = gathered("ffn2_w_down", act2).reshape(N_CHIP, -1, d)
    pre3 = _ffn_down("ffn2_down", act2, wd2, h2)

    dpre3, dpre3_b, d_ln3_g, d_ln3_b, sq = _ln_loss_bwd("ln3_loss", pre3, tgt, ln3_g, ln3_b, n_rows, seq_len)
    loss = lax.psum(0.5 * jnp.sum(sq) / d, ("x", "y", "c"))
    reducing = {}

    dgu2 = _ffn_bwd_dgu("ffn2_bwd_dgu", dpre3_b, wd2, gu2)
    g_wd2 = _ffn_bwd_wd("ffn2_bwd_wd", act2, dpre3_b)
    reducing["ffn2_w_down"], token = _reduce_start("ffn2_w_down", g_wd2.reshape(N_DEV, -1, d), core)
    g_wgu2 = _ffn_bwd_wgu("ffn2_bwd_wgu", h2_b, dgu2, (token,))
    reducing["ffn2_w_gu"], token = _reduce_start("ffn2_w_gu", g_wgu2, core)
    dh2 = _ffn_bwd_dh("ffn2_bwd_dh", dgu2, wgu2, dpre3, (token,))
    dpre2, dpre2_b, d_ln2_g, d_ln2_b = _ln_bwd("ln2_bwd", pre2, dh2, ln2_g)

    dy_mix = _proj_out_bwd_y("mix_out_bwd_y", dpre2_b, wout_all)
    g_wout = _proj_out_bwd_w("mix_out_bwd_w", y_mix, dpre2_b, N_DEV)
    reducing["w_out"], token = _reduce_start("w_out", g_wout, core)
    dyps, d_pool_scale = _pool_scale_bwd("mix_pool_scale_bwd", dy_mix, 1, ypre, pool_scale)
    dd = _pool_mix_bwd_in("mix_pool_w_bwd_in", dyps, pool_w_b)
    d_pool_w = _pool_mix_bwd_w("mix_pool_w_bwd_w", dpool, dyps, pg)
    du_pool = _pool_bwd("mix_pool_bwd", dd, pg, n_rows, seq_len)
    du_b, du_c, du_x, d_conv_w = _conv_bwd("mix_conv_bwd", u, conv_w_full, dy_mix, n_rows, seq_len)
    du = jnp.concatenate([du_b, du_c, du_x, du_pool], axis=1)
    g_win = _proj_in_bwd_w("mix_in_bwd_w", h1_b, du, N_DEV, (token,))
    reducing["w_in"], token = _reduce_start("w_in", g_win, core)
    dh1 = _proj_in_bwd_h("mix_in_bwd_h", du, win_all, dpre2, (token,))
    dpre1, dpre1_b, d_ln1_g, d_ln1_b = _ln_bwd("ln1_bwd", pre1, dh1, ln1_g)

    dgu1 = _ffn_bwd_dgu("ffn1_bwd_dgu", dpre1_b, wd1, gu1)
    g_wd1 = _ffn_bwd_wd("ffn1_bwd_wd", act1, dpre1_b)
    reducing["ffn1_w_down"], token = _reduce_start("ffn1_w_down", g_wd1.reshape(N_DEV, -1, d), core)
    g_wgu1 = _ffn_bwd_wgu("ffn1_bwd_wgu", h0_b, dgu1, (token,))
    reducing["ffn1_w_gu"], token = _reduce_start("ffn1_w_gu", g_wgu1, core)
    dh0 = _ffn_bwd_dh("ffn1_bwd_dh", dgu1, wgu1, dpre1, (token,))

    dh0_seq = dh0[:n_rows].reshape(n_seq, seq_len, d)
    grad_x = dh0_seq[:, N_META:]
    d_meta = jnp.sum(dh0_seq[:, :N_META], axis=0)

    grads, deltas, new_m, new_v = {}, {}, {}, {}
    for n in ["ffn2_w_down", "ffn2_w_gu", "w_out", "w_in", "ffn1_w_down", "ffn1_w_gu"]:
        summed, from_chips = _chips_wait(f"rs_chips_wait_{n}", reducing[n], dh0)
        g, dl, mm, vv = _adamw_big(f"adamw_{n}", summed, from_chips, chip, w_of[n][0], m_of[n][0], v_of[n][0])
        grads[n], deltas[n], new_m[n], new_v[n] = g[None], dl[None], mm[None], vv[None]

    def widen(a):
        return jnp.pad(a, ((0, 0), (0, d - a.shape[1])))

    small_part = jnp.concatenate([
        d_ln1_g, d_ln1_b, d_ln2_g, d_ln2_b, d_ln3_g, d_ln3_b, widen(d_pool_scale), widen(d_conv_w), d_meta,
        d_pool_w.reshape(-1, d)], axis=0)
    n_small_rows = small_part.shape[0]
    small_part = _pad_rows(small_part, -(-n_small_rows // 8) * 8)
    small_sum = _sum_parts("small_sum", _all_gather("ag_small_grads", _into_slab("slab_small_grads", small_part, dev, F32)))
    o = 7 + CONV_K
    g_small = {
        "ln1_g": small_sum[0:1], "ln1_b": small_sum[1:2], "ln2_g": small_sum[2:3], "ln2_b": small_sum[3:4],
        "ln3_g": small_sum[4:5], "ln3_b": small_sum[5:6], "pool_scale": small_sum[6:7, :p_pool],
        "conv_w": lax.dynamic_slice_in_dim(small_sum[7:o, :c_conv], dev_index * (c_conv // N_DEV), c_conv // N_DEV, axis=1)[None],
        "meta_tokens": lax.dynamic_slice_in_dim(small_sum[o:o + N_META], dev_index * wcol, wcol, axis=1),
        "pool_w": lax.dynamic_slice_in_dim(small_sum[o + N_META:n_small_rows].reshape(N_POOL_GROUPS, pg, pg),
                                           dev_index * (pg // N_DEV), pg // N_DEV, axis=1)[None],
    }
    small = ["meta_tokens", "ln1_g", "ln1_b", "conv_w", "pool_w", "pool_scale", "ln2_g", "ln2_b", "ln3_g", "ln3_b"]

    def flat(a):
        return a.reshape(-1, a.shape[-1])

    outs = _adamw_small("adamw_small", [flat(w_of[n]) for n in small], [flat(g_small[n]) for n in small],
                        [flat(m_of[n]) for n in small], [flat(v_of[n]) for n in small])
    ns = len(small)
    for i, n in enumerate(small):
        shape = w_of[n].shape
        grads[n] = g_small[n].reshape(shape)
        deltas[n], new_m[n], new_v[n] = outs[i].reshape(shape), outs[ns + i].reshape(shape), outs[2 * ns + i].reshape(shape)

    return (loss, grad_x, *[grads[n] for n in names], *[deltas[n] for n in names],
            *[new_m[n] for n in names], *[new_v[n] for n in names])
```

```python
import functools

import jax
import jax.numpy as jnp
from jax import lax
from jax.experimental import pallas as pl
from jax.experimental.pallas import tpu as pltpu

N_DEV = 8
N_CHIP = 4
N_META = 16
CONV_K = 3
POOL_WINDOWS = (2, 4, 8, 16)
N_POOL_GROUPS = len(POOL_WINDOWS)
LN_EPS = 1e-5
DEPTH = 1
ALPHA = (2.0 * DEPTH) ** 0.25
ADAM_LR = 0.001
ADAM_B1 = 0.9
ADAM_B2 = 0.999
ADAM_EPS = 1e-08
ADAM_WD = 0.01
ADAM_STEP = 10

V7X_VMEM_BYTES = 64 * 1024 * 1024
VMEM_LIMIT = V7X_VMEM_BYTES - 6 * 1024 * 1024
LANE = 128
ROW_ALIGN = 3 * LANE
TM_BIG = 1408
TM_WIDE = 704
TK = 512
TK_TOKENS = 1408
TN = 1024
TR_LN = 128
ELEM_BLOCK_BYTES = 1 << 20
TC_MIX = LANE
EPILOGUE_ROWS = 64

NN = (((1,), (0,)), ((), ()))
NT = (((1,), (1,)), ((), ()))
TN_DIMS = (((0,), (0,)), ((), ()))
MESH = pl.DeviceIdType.MESH
BF16 = jnp.bfloat16
F32 = jnp.float32
ANY = pl.BlockSpec(memory_space=pl.ANY)
HBM = pl.BlockSpec(memory_space=pltpu.HBM)
SEM = pl.BlockSpec(memory_space=pltpu.SEMAPHORE)
EFFECT = pltpu.SideEffectType.DATAFLOW_SIDE_EFFECTING


def _tile(n, target, mult):
    best = None
    for t in range(mult, min(n, target) + 1, mult):
        if n % t == 0:
            best = t
    return n if best is None else best


def _params(sem):
    return pltpu.CompilerParams(dimension_semantics=sem, vmem_limit_bytes=VMEM_LIMIT)


def _sds(shape, dtype):
    return jax.ShapeDtypeStruct(shape, dtype)


def _row_chunks(n_rows, fn):
    ch = _tile(n_rows, EPILOGUE_ROWS, 16)

    def step(i, carry):
        fn(pl.ds(pl.multiple_of(i * ch, ch), ch))
        return carry

    lax.fori_loop(0, n_rows // ch, step, 0)


def _mm(name, grid, dims, ab, ab_specs, extras, extra_specs, out_shape, out_specs, acc_shape, epilogue, deps=()):
    nk = grid[-1]
    n_extra = len(extras)
    n_in = 2 + n_extra + len(deps)
    n_out = len(out_shape)
    kax = len(grid) - 1

    def body(*refs):
        a_ref, b_ref = refs[0], refs[1]
        ex = refs[2:2 + n_extra]
        outs = refs[n_in:n_in + n_out]
        if nk == 1:
            epilogue(lax.dot_general(a_ref[...], b_ref[...], dims, preferred_element_type=F32), ex, outs, slice(None))
            return
        acc = refs[-1]
        k = pl.program_id(kax)

        @pl.when(k == 0)
        def _():
            acc[...] = jnp.zeros_like(acc)

        acc[...] += lax.dot_general(a_ref[...], b_ref[...], dims, preferred_element_type=F32)

        @pl.when(k == nk - 1)
        def _():
            _row_chunks(acc_shape[0], lambda rows: epilogue(acc[rows, :], ex, outs, rows))

    scratch = [] if nk == 1 else [pltpu.VMEM(acc_shape, F32)]
    sem = ("parallel",) * kax + ("arbitrary",)
    return pl.pallas_call(
        body, name=name, grid=grid, in_specs=list(ab_specs) + list(extra_specs) + [ANY] * len(deps), out_specs=list(out_specs),
        out_shape=list(out_shape), scratch_shapes=scratch, compiler_params=_params(sem),
    )(*ab, *extras, *deps)


def _silu_parts(g):
    s = 1.0 / (1.0 + jnp.exp(-g))
    return s, g * s


def _ffn_gu(name, h_b, wgu_all):
    tp, d = h_b.shape
    ns, _, ng = wgu_all.shape
    half = ns // 2
    tm, tk = _tile(tp, TM_WIDE, 16), _tile(d, TK, LANE)
    grid = (tp // tm, half, d // tk)
    nk = grid[-1]

    def body(h_ref, wg_ref, wu_ref, gu_ref, act_ref, acc_g, acc_u):
        k = pl.program_id(2)

        @pl.when(k == 0)
        def _():
            acc_g[...] = jnp.zeros_like(acc_g)
            acc_u[...] = jnp.zeros_like(acc_u)

        acc_g[...] += jnp.dot(h_ref[...], wg_ref[...], preferred_element_type=F32)
        acc_u[...] += jnp.dot(h_ref[...], wu_ref[...], preferred_element_type=F32)

        @pl.when(k == nk - 1)
        def _():
            def finish(rows):
                g = acc_g[rows, :]
                u = acc_u[rows, :]
                _, silu = _silu_parts(g)
                gu_ref[0, rows, :] = g.astype(BF16)
                gu_ref[1, rows, :] = u.astype(BF16)
                act_ref[rows, :] = (silu * u).astype(BF16)

            _row_chunks(tm, finish)

    return pl.pallas_call(
        body, name=name, grid=grid,
        in_specs=[pl.BlockSpec((tm, tk), lambda m, s, k: (m, k)),
                  pl.BlockSpec((None, tk, ng), lambda m, s, k: (s, k, 0)),
                  pl.BlockSpec((None, tk, ng), lambda m, s, k: (s + half, k, 0))],
        out_specs=[pl.BlockSpec((None, 2, tm, ng), lambda m, s, k: (s, 0, m, 0)),
                   pl.BlockSpec((None, tm, ng), lambda m, s, k: (s, m, 0))],
        out_shape=[_sds((half, 2, tp, ng), BF16), _sds((half, tp, ng), BF16)],
        scratch_shapes=[pltpu.VMEM((tm, ng), F32), pltpu.VMEM((tm, ng), F32)],
        compiler_params=_params(("parallel", "parallel", "arbitrary")),
    )(h_b, wgu_all, wgu_all)


def _ffn_down(name, act, wd4, h):
    ns, tp, ng = act.shape
    d = wd4.shape[2]
    tm, tn = _tile(tp, TM_WIDE, 16), _tile(d, TN, LANE)

    def epi(acc, ex, outs, rows):
        outs[0][rows, :] = ALPHA * ex[0][rows, :] + 0.5 * acc

    return _mm(name, (tp // tm, d // tn, ns), NN, (act, wd4),
               [pl.BlockSpec((None, tm, ng), lambda m, n, s: (s, m, 0)),
                pl.BlockSpec((None, ng, tn), lambda m, n, s: (s, 0, n))],
               (h,), [pl.BlockSpec((tm, tn), lambda m, n, s: (m, n))],
               [_sds((tp, d), F32)], [pl.BlockSpec((tm, tn), lambda m, n, s: (m, n))], (tm, tn), epi)[0]


def _ffn_bwd_dgu(name, dp_b, wd4, gu, deps=()):
    tp, d = dp_b.shape
    ns, ng, _ = wd4.shape
    tm, tk = _tile(tp, TM_WIDE, 16), _tile(d, TK, LANE)

    def epi(acc, ex, outs, rows):
        g = ex[0][0, rows, :].astype(F32)
        u = ex[0][1, rows, :].astype(F32)
        da = 0.5 * acc
        s, silu = _silu_parts(g)
        outs[0][0, rows, :] = (da * u * (s + silu * (1.0 - s))).astype(BF16)
        outs[0][1, rows, :] = (da * silu).astype(BF16)

    return _mm(name, (tp // tm, ns, d // tk), NT, (dp_b, wd4),
               [pl.BlockSpec((tm, tk), lambda m, s, k: (m, k)),
                pl.BlockSpec((None, ng, tk), lambda m, s, k: (s, 0, k))],
               (gu,), [pl.BlockSpec((None, 2, tm, ng), lambda m, s, k: (s, 0, m, 0))],
               [_sds((ns, 2, tp, ng), BF16)], [pl.BlockSpec((None, 2, tm, ng), lambda m, s, k: (s, 0, m, 0))],
               (tm, ng), epi, deps)[0]


def _ffn_bwd_wd(name, act, dp_b, deps=()):
    ns, tp, ng = act.shape
    d = dp_b.shape[1]
    tkt, tn = _tile(tp, TK_TOKENS, LANE), _tile(d, TN, LANE)

    def epi(acc, ex, outs, rows):
        outs[0][rows, :] = (0.5 * acc).astype(BF16)

    return _mm(name, (ns, d // tn, tp // tkt), TN_DIMS, (act, dp_b),
               [pl.BlockSpec((None, tkt, ng), lambda s, n, t: (s, t, 0)),
                pl.BlockSpec((tkt, tn), lambda s, n, t: (t, n))],
               (), [], [_sds((ns, ng, d), BF16)], [pl.BlockSpec((None, ng, tn), lambda s, n, t: (s, 0, n))],
               (ng, tn), epi, deps)[0]


def _ffn_bwd_wgu(name, h_b, dgu, deps=()):
    tp, d = h_b.shape
    ns, _, _, ng = dgu.shape
    tkt, tmd = _tile(tp, TK_TOKENS, LANE), _tile(d, TN, LANE)

    def epi(acc, ex, outs, rows):
        outs[0][rows, :] = acc.astype(BF16)

    return _mm(name, (ns, 2, d // tmd, tp // tkt), TN_DIMS, (h_b, dgu),
               [pl.BlockSpec((tkt, tmd), lambda s, j, i, t: (t, i)),
                pl.BlockSpec((None, None, tkt, ng), lambda s, j, i, t: (s, j, t, 0))],
               (), [], [_sds((2 * ns, d, ng), BF16)],
               [pl.BlockSpec((None, tmd, ng), lambda s, j, i, t: (j * ns + s, i, 0))], (tmd, ng), epi, deps)[0]


def _ffn_bwd_dh(name, dgu, wgu_all, dp, deps=()):
    ns, _, tp, ng = dgu.shape
    d = wgu_all.shape[1]
    tm, tn = _tile(tp, TM_WIDE, 16), _tile(d, TN, LANE)

    def epi(acc, ex, outs, rows):
        outs[0][rows, :] = ALPHA * ex[0][rows, :] + acc

    return _mm(name, (tp // tm, d // tn, 2 * ns), NT, (dgu, wgu_all),
               [pl.BlockSpec((None, None, tm, ng), lambda m, n, j: (j % ns, j // ns, m, 0)),
                pl.BlockSpec((None, tn, ng), lambda m, n, j: (j, n, 0))],
               (dp,), [pl.BlockSpec((tm, tn), lambda m, n, j: (m, n))],
               [_sds((tp, d), F32)], [pl.BlockSpec((tm, tn), lambda m, n, j: (m, n))], (tm, tn), epi, deps)[0]


def _ln_stats(x):
    mu = jnp.mean(x, axis=-1, keepdims=True)
    xc = x - mu
    var = jnp.mean(xc * xc, axis=-1, keepdims=True)
    rstd = lax.rsqrt(var + LN_EPS)
    return xc * rstd, rstd


def _ln_bwd_rows(dy, xhat, rstd, g):
    dxh = dy * g
    m1 = jnp.mean(dxh, axis=-1, keepdims=True)
    m2 = jnp.mean(dxh * xhat, axis=-1, keepdims=True)
    return rstd * (dxh - m1 - xhat * m2)


def _ln_fwd(name, pre, g, b):
    tp, d = pre.shape
    tr = _tile(tp, TR_LN, 16)

    def body(x_ref, g_ref, b_ref, y_ref, yb_ref):
        xhat, _ = _ln_stats(x_ref[...])
        y = xhat * g_ref[...] + b_ref[...]
        y_ref[...] = y
        yb_ref[...] = y.astype(BF16)

    row = pl.BlockSpec((tr, d), lambda i: (i, 0))
    vec = pl.BlockSpec((1, d), lambda i: (0, 0))
    return pl.pallas_call(body, name=name, grid=(tp // tr,), in_specs=[row, vec, vec], out_specs=[row, row],
                          out_shape=[_sds((tp, d), F32), _sds((tp, d), BF16)],
                          compiler_params=_params(("parallel",)))(pre, g, b)


def _accumulate(i, ref, val):
    @pl.when(i == 0)
    def _():
        ref[...] = val

    @pl.when(i > 0)
    def _():
        ref[...] += val


def _ln_bwd(name, pre, dy, g):
    tp, d = pre.shape
    tr = _tile(tp, TR_LN, 16)

    def body(x_ref, dy_ref, g_ref, dx_ref, dxb_ref, dg_ref, db_ref):
        i = pl.program_id(0)
        xhat, rstd = _ln_stats(x_ref[...])
        dy = dy_ref[...]
        dx = _ln_bwd_rows(dy, xhat, rstd, g_ref[...])
        dx_ref[...] = dx
        dxb_ref[...] = dx.astype(BF16)
        _accumulate(i, dg_ref, jnp.sum(dy * xhat, axis=0, keepdims=True))
        _accumulate(i, db_ref, jnp.sum(dy, axis=0, keepdims=True))

    row = pl.BlockSpec((tr, d), lambda i: (i, 0))
    vec = pl.BlockSpec((1, d), lambda i: (0, 0))
    return pl.pallas_call(body, name=name, grid=(tp // tr,), in_specs=[row, row, vec], out_specs=[row, row, vec, vec],
                          out_shape=[_sds((tp, d), F32), _sds((tp, d), BF16), _sds((1, d), F32), _sds((1, d), F32)],
                          compiler_params=_params(("arbitrary",)))(pre, dy, g)


def _ln_loss_bwd(name, pre, tgt, g, b, n_rows, seq_len):
    tp, d = pre.shape
    tr = _tile(tp, TR_LN, 16)
    n_seq = n_rows // seq_len

    def body(x_ref, t_ref, g_ref, b_ref, dx_ref, dxb_ref, dg_ref, db_ref, sq_ref):
        i = pl.program_id(0)
        xhat, rstd = _ln_stats(x_ref[...])
        gain = g_ref[...]
        y = xhat * gain + b_ref[...]
        r = i * tr + lax.broadcasted_iota(jnp.int32, (tr, 1), 0)
        pos = r
        for s in range(1, n_seq):
            pos = jnp.where(r >= s * seq_len, r - s * seq_len, pos)
        live = jnp.logical_and(r < n_rows, pos >= N_META)
        err = jnp.where(live, y - t_ref[...], 0.0)
        dy = err * (1.0 / d)
        dx = _ln_bwd_rows(dy, xhat, rstd, gain)
        dx_ref[...] = dx
        dxb_ref[...] = dx.astype(BF16)
        _accumulate(i, dg_ref, jnp.sum(dy * xhat, axis=0, keepdims=True))
        _accumulate(i, db_ref, jnp.sum(dy, axis=0, keepdims=True))
        _accumulate(i, sq_ref, jnp.sum(err * err, axis=0, keepdims=True))

    row = pl.BlockSpec((tr, d), lambda i: (i, 0))
    vec = pl.BlockSpec((1, d), lambda i: (0, 0))
    return pl.pallas_call(
        body, name=name, grid=(tp // tr,), in_specs=[row, row, vec, vec], out_specs=[row, row, vec, vec, vec],
        out_shape=[_sds((tp, d), F32), _sds((tp, d), BF16), _sds((1, d), F32), _sds((1, d), F32), _sds((1, d), F32)],
        compiler_params=_params(("arbitrary",)))(pre, tgt, g, b)


def _proj_in(name, h_b, win_all):
    tp, d = h_b.shape
    ns, _, ni = win_all.shape
    tm, tk = _tile(tp, TM_BIG, 16), _tile(d, TK, LANE)

    def epi(acc, ex, outs, rows):
        outs[0][rows, :] = acc

    return _mm(name, (tp // tm, ns, d // tk), NN, (h_b, win_all),
               [pl.BlockSpec((tm, tk), lambda m, j, k: (m, k)),
                pl.BlockSpec((None, tk, ni), lambda m, j, k: (j, k, 0))],
               (), [], [_sds((tp, ns * ni), F32)], [pl.BlockSpec((tm, ni), lambda m, j, k: (m, j))], (tm, ni), epi)[0]


def _positions(tp, n_rows, seq_len):
    r = lax.broadcasted_iota(jnp.int32, (tp, 1), 0)
    pos = r
    for s in range(1, n_rows // seq_len):
        pos = jnp.where(r >= s * seq_len, r - s * seq_len, pos)
    return pos


def _shift_down(x, s, pos):
    return jnp.where(pos >= s, pltpu.roll(x, s, 0), 0.0)


def _shift_up(x, s, pos, seq_len):
    return jnp.where(pos + s < seq_len, pltpu.roll(x, x.shape[0] - s, 0), 0.0)


def _conv_fwd(name, u, conv_w, n_rows, seq_len):
    tp = u.shape[0]
    c = conv_w.shape[1]
    tc = _tile(c, TC_MIX, LANE)
    nb = c // tc

    def body(gb_ref, gc_ref, xi_ref, w_ref, y_ref):
        pos = _positions(tp, n_rows, seq_len)
        v = gc_ref[...] * xi_ref[...]
        w = w_ref[...]
        y = _shift_down(v, 2, pos) * w[0:1]
        y = y + _shift_down(v, 1, pos) * w[1:2]
        y = y + v * w[2:3]
        y_ref[...] = (gb_ref[...] * y).astype(BF16)

    col = lambda off: pl.BlockSpec((tp, tc), lambda i: (0, off + i))
    return pl.pallas_call(body, name=name, grid=(nb,), in_specs=[col(0), col(nb), col(2 * nb), pl.BlockSpec((CONV_K, tc), lambda i: (0, i))],
                          out_specs=pl.BlockSpec((tp, tc), lambda i: (0, i)), out_shape=_sds((tp, c), BF16),
                          compiler_params=_params(("parallel",)))(u, u, u, conv_w)


def _conv_bwd(name, u, conv_w, dy, n_rows, seq_len):
    tp = u.shape[0]
    c = conv_w.shape[1]
    tc = _tile(c, TC_MIX, LANE)
    nb = c // tc

    def body(gb_ref, gc_ref, xi_ref, w_ref, dy_ref, dgb_ref, dgc_ref, dxi_ref, dw_ref):
        pos = _positions(tp, n_rows, seq_len)
        gc, xi = gc_ref[...], xi_ref[...]
        v = gc * xi
        w = w_ref[...]
        v2, v1 = _shift_down(v, 2, pos), _shift_down(v, 1, pos)
        conv = v2 * w[0:1]
        conv = conv + v1 * w[1:2]
        conv = conv + v * w[2:3]
        dyc = dy_ref[...]
        dgb_ref[...] = (dyc * conv).astype(BF16)
        dconv = dyc * gb_ref[...]
        dv = dconv * w[2:3] + _shift_up(dconv, 1, pos, seq_len) * w[1:2] + _shift_up(dconv, 2, pos, seq_len) * w[0:1]
        dgc_ref[...] = (dv * xi).astype(BF16)
        dxi_ref[...] = (dv * gc).astype(BF16)
        dw_ref[0:1, :] = jnp.sum(dconv * v2, axis=0, keepdims=True)
        dw_ref[1:2, :] = jnp.sum(dconv * v1, axis=0, keepdims=True)
        dw_ref[2:3, :] = jnp.sum(dconv * v, axis=0, keepdims=True)

    col = lambda off: pl.BlockSpec((tp, tc), lambda i: (0, off + i))
    wspec = pl.BlockSpec((CONV_K, tc), lambda i: (0, i))
    return pl.pallas_call(body, name=name, grid=(nb,), in_specs=[col(0), col(nb), col(2 * nb), wspec, col(0)],
                          out_specs=[col(0), col(0), col(0), wspec],
                          out_shape=[_sds((tp, c), BF16)] * 3 + [_sds((CONV_K, c), F32)],
                          compiler_params=_params(("parallel",)))(u, u, u, conv_w, dy)


def _window_select(group, parts):
    out = parts[-1]
    for gi in range(len(parts) - 2, -1, -1):
        out = jnp.where(group == gi, parts[gi], out)
    return out


def _pool_fwd(name, u, col0, p, pg, n_rows, seq_len):
    tp = u.shape[0]
    tc = _tile(pg, TC_MIX, LANE)
    per_group = pg // tc

    def body(z_ref, d_ref):
        group = pl.program_id(0) // per_group
        pos = _positions(tp, n_rows, seq_len)
        z = z_ref[...]
        sums, s, w = [], z, 1
        for _ in POOL_WINDOWS:
            s = s + _shift_down(s, w, pos)
            w *= 2
            sums.append(s)
        total = _window_select(group, sums)
        count = jnp.minimum(pos + 1, 2 << group).astype(F32)
        d_ref[...] = (total / count - z).astype(BF16)

    return pl.pallas_call(body, name=name, grid=(p // tc,), in_specs=[pl.BlockSpec((tp, tc), lambda i: (0, col0 // tc + i))],
                          out_specs=pl.BlockSpec((tp, tc), lambda i: (0, i)), out_shape=_sds((tp, p), BF16),
                          compiler_params=_params(("parallel",)))(u)


def _pool_bwd(name, dd, pg, n_rows, seq_len):
    tp, p = dd.shape
    tc = _tile(pg, TC_MIX, LANE)
    per_group = pg // tc

    def body(dd_ref, dz_ref):
        group = pl.program_id(0) // per_group
        pos = _positions(tp, n_rows, seq_len)
        dd_v = dd_ref[...]
        count = jnp.minimum(pos + 1, 2 << group).astype(F32)
        sums, s, w = [], dd_v / count, 1
        for _ in POOL_WINDOWS:
            s = s + _shift_up(s, w, pos, seq_len)
            w *= 2
            sums.append(s)
        dz_ref[...] = (_window_select(group, sums) - dd_v).astype(BF16)

    spec = pl.BlockSpec((tp, tc), lambda i: (0, i))
    return pl.pallas_call(body, name=name, grid=(p // tc,), in_specs=[spec], out_specs=spec, out_shape=_sds((tp, p), BF16),
                          compiler_params=_params(("parallel",)))(dd)


def _pool_mix(name, dpool, pool_w_b, scale):
    tp, p = dpool.shape
    ng, pg, _ = pool_w_b.shape
    tm = _tile(tp, TM_BIG, 16)

    def epi(acc, ex, outs, rows):
        outs[0][rows, :] = acc
        outs[1][rows, :] = (acc * ex[0][...]).astype(BF16)

    blk = pl.BlockSpec((tm, pg), lambda m, g, k: (m, g))
    return _mm(name, (tp // tm, ng, 1), NN, (dpool, pool_w_b), [blk, pl.BlockSpec((None, pg, pg), lambda m, g, k: (g, 0, 0))],
               (scale,), [pl.BlockSpec((1, pg), lambda m, g, k: (0, g))],
               [_sds((tp, p), F32), _sds((tp, p), BF16)], [blk, blk], None, epi)


def _pool_scale_bwd(name, dy, col_block, ypre, scale):
    tp, p = ypre.shape
    tr = _tile(tp, TR_LN, 16)

    def body(dy_ref, yp_ref, s_ref, o_ref, ds_ref):
        i = pl.program_id(0)
        dyp = dy_ref[...]
        o_ref[...] = (dyp * s_ref[...]).astype(BF16)
        _accumulate(i, ds_ref, jnp.sum(dyp * yp_ref[...], axis=0, keepdims=True))

    row = pl.BlockSpec((tr, p), lambda i: (i, 0))
    vec = pl.BlockSpec((1, p), lambda i: (0, 0))
    return pl.pallas_call(body, name=name, grid=(tp // tr,), in_specs=[pl.BlockSpec((tr, p), lambda i: (i, col_block)), row, vec],
                          out_specs=[row, vec], out_shape=[_sds((tp, p), BF16), _sds((1, p), F32)],
                          compiler_params=_params(("arbitrary",)))(dy, ypre, scale)


def _pool_mix_bwd_in(name, dyps, pool_w_b):
    tp, p = dyps.shape
    ng, pg, _ = pool_w_b.shape
    tm = _tile(tp, TM_BIG, 16)

    def epi(acc, ex, outs, rows):
        outs[0][rows, :] = acc

    blk = pl.BlockSpec((tm, pg), lambda m, g, k: (m, g))
    return _mm(name, (tp // tm, ng, 1), NT, (dyps, pool_w_b), [blk, pl.BlockSpec((None, pg, pg), lambda m, g, k: (g, 0, 0))],
               (), [], [_sds((tp, p), F32)], [blk], None, epi)[0]


def _pool_mix_bwd_w(name, dpool, dyps, pg):
    tp, p = dpool.shape
    ng = p // pg
    tkt = _tile(tp, TM_BIG, LANE)

    def epi(acc, ex, outs, rows):
        outs[0][rows, :] = acc

    blk = pl.BlockSpec((tkt, pg), lambda g, t: (t, g))
    return _mm(name, (ng, tp // tkt), TN_DIMS, (dpool, dyps), [blk, blk], (), [],
               [_sds((ng, pg, pg), F32)], [pl.BlockSpec((None, pg, pg), lambda g, t: (g, 0, 0))], (pg, pg), epi)[0]


def _proj_out(name, y, wout_all, h):
    tp = y.shape[0]
    ns, ro, d = wout_all.shape
    tm, tn = _tile(tp, TM_BIG, 16), _tile(d, TN, LANE)

    def epi(acc, ex, outs, rows):
        outs[0][rows, :] = ALPHA * ex[0][rows, :] + acc

    mn = pl.BlockSpec((tm, tn), lambda m, n, j: (m, n))
    return _mm(name, (tp // tm, d // tn, ns), NN, (y, wout_all),
               [pl.BlockSpec((tm, ro), lambda m, n, j: (m, j)), pl.BlockSpec((None, ro, tn), lambda m, n, j: (j, 0, n))],
               (h,), [mn], [_sds((tp, d), F32)], [mn], (tm, tn), epi)[0]


def _proj_out_bwd_y(name, dp_b, wout_all, deps=()):
    tp, d = dp_b.shape
    ns, ro, _ = wout_all.shape
    tm, tk = _tile(tp, TM_BIG, 16), _tile(d, TK, LANE)

    def epi(acc, ex, outs, rows):
        outs[0][rows, :] = acc

    return _mm(name, (tp // tm, ns, d // tk), NT, (dp_b, wout_all),
               [pl.BlockSpec((tm, tk), lambda m, j, k: (m, k)), pl.BlockSpec((None, ro, tk), lambda m, j, k: (j, 0, k))],
               (), [], [_sds((tp, ns * ro), F32)], [pl.BlockSpec((tm, ro), lambda m, j, k: (m, j))], (tm, ro), epi, deps)[0]


def _proj_out_bwd_w(name, y, dp_b, ns):
    tp, c = y.shape
    d = dp_b.shape[1]
    ro = c // ns
    tkt, tn = _tile(tp, TM_BIG, LANE), _tile(d, 2 * TN, LANE)

    def epi(acc, ex, outs, rows):
        outs[0][rows, :] = acc.astype(BF16)

    return _mm(name, (ns, d // tn, tp // tkt), TN_DIMS, (y, dp_b),
               [pl.BlockSpec((tkt, ro), lambda j, n, t: (t, j)), pl.BlockSpec((tkt, tn), lambda j, n, t: (t, n))],
               (), [], [_sds((ns, ro, d), BF16)], [pl.BlockSpec((None, ro, tn), lambda j, n, t: (j, 0, n))], (ro, tn), epi)[0]


def _proj_in_bwd_w(name, h_b, du, ns, deps=()):
    tp, d = h_b.shape
    ni = du.shape[1] // ns
    tkt, tmd = _tile(tp, TK_TOKENS, LANE), _tile(d, 2 * TN, LANE)

    def epi(acc, ex, outs, rows):
        outs[0][rows, :] = acc.astype(BF16)

    return _mm(name, (ns, d // tmd, tp // tkt), TN_DIMS, (h_b, du),
               [pl.BlockSpec((tkt, tmd), lambda j, i, t: (t, i)), pl.BlockSpec((tkt, ni), lambda j, i, t: (t, j))],
               (), [], [_sds((ns, d, ni), BF16)], [pl.BlockSpec((None, tmd, ni), lambda j, i, t: (j, i, 0))], (tmd, ni), epi, deps)[0]


def _proj_in_bwd_h(name, du, win_all, dp, deps=()):
    tp = du.shape[0]
    ns, d, ni = win_all.shape
    tm, tn = _tile(tp, TM_BIG, 16), _tile(d, TN, LANE)

    def epi(acc, ex, outs, rows):
        outs[0][rows, :] = ALPHA * ex[0][rows, :] + acc

    mn = pl.BlockSpec((tm, tn), lambda m, n, j: (m, n))
    return _mm(name, (tp // tm, d // tn, ns), NT, (du, win_all),
               [pl.BlockSpec((tm, ni), lambda m, n, j: (m, j)), pl.BlockSpec((None, tn, ni), lambda m, n, j: (j, n, 0))],
               (dp,), [mn], [_sds((tp, d), F32)], [mn], (tm, tn), epi, deps)[0]


def _place():
    return lax.axis_index("x"), lax.axis_index("y"), lax.axis_index("c")


def _hbm(a):
    return pltpu.with_memory_space_constraint(a, pltpu.HBM)


def _token_shape():
    return _sds((8, LANE), F32)


def _gather_peers(x, y, c):
    return [(x, y, 1 - c), (1 - x, y, c), (x, 1 - y, c), (1 - x, 1 - y, c)]


def _into_slab(name, w, dev, dtype):
    r, c = w.shape
    tr = _tile(r, max(16, ELEM_BLOCK_BYTES // (4 * c)), 16 if dtype == BF16 else 8)

    def body(dev_ref, w_ref, o_ref):
        o_ref[...] = w_ref[...].astype(dtype)

    return pl.pallas_call(
        body, name=name,
        grid_spec=pltpu.PrefetchScalarGridSpec(
            num_scalar_prefetch=1, grid=(r // tr,), in_specs=[pl.BlockSpec((tr, c), lambda i, dev_ref: (i, 0))],
            out_specs=pl.BlockSpec((None, tr, c), lambda i, dev_ref: (dev_ref[0], i, 0))),
        out_shape=_sds((N_DEV, r, c), dtype), compiler_params=_params(("parallel",)),
    )(dev, w)


def _gather_start(name, lands, after=()):
    n = len(lands)
    n_peer = N_CHIP
    n_in = n + len(after)

    def body(*refs):
        land_refs = refs[:n]
        send_sems, recv_sems = refs[n_in:n_in + n], refs[n_in + n:n_in + 2 * n]
        token = refs[-1]
        x, y, c = _place()
        me = 4 * x + 2 * y + c
        for i in range(n):
            for k, peer in enumerate(_gather_peers(x, y, c)):
                pltpu.make_async_remote_copy(
                    src_ref=land_refs[i].at[me], dst_ref=land_refs[i].at[me], send_sem=send_sems[i].at[k],
                    recv_sem=recv_sems[i].at[k], device_id=peer, device_id_type=MESH).start()
        token[...] = jnp.zeros_like(token)

    sem = pltpu.SemaphoreType.DMA((n_peer,))
    out = pl.pallas_call(
        body, name=name,
        out_shape=[sem] * (2 * n) + [pltpu.HBM(l.shape, l.dtype) for l in lands] + [_token_shape()],
        in_specs=[HBM] * n + [ANY] * len(after),
        out_specs=[SEM] * (2 * n) + [HBM] * n + [pl.BlockSpec(memory_space=pltpu.VMEM)],
        input_output_aliases={i: 2 * n + i for i in range(n)},
        compiler_params=pltpu.CompilerParams(has_side_effects=EFFECT),
    )(*[_hbm(l) for l in lands], *after)
    per = [(out[i], out[n + i], out[2 * n + i]) for i in range(n)]
    return per, out[-1]


def _gather_wait(name, started, after):
    send_sems, recv_sems, land = started

    def body(land_ref, send_ref, recv_ref, after_ref, land_out):
        x, y, c = _place()
        for k, (px, py, pc) in enumerate(_gather_peers(x, y, c)):
            cp = pltpu.make_async_remote_copy(
                src_ref=land_ref.at[4 * x + 2 * y + c], dst_ref=land_ref.at[4 * px + 2 * py + pc], send_sem=send_ref.at[k],
                recv_sem=recv_ref.at[k], device_id=(px, py, pc), device_id_type=MESH)
            cp.wait_send()
            cp.wait_recv()

    return pl.pallas_call(
        body, name=name, out_shape=pltpu.HBM(land.shape, land.dtype),
        in_specs=(HBM, SEM, SEM, ANY), out_specs=HBM, input_output_aliases={0: 0},
        compiler_params=pltpu.CompilerParams(has_side_effects=EFFECT),
    )(land, send_sems, recv_sems, after)


def _gather_finish(name, land, deps=()):
    def body(land_ref, *rest):
        out_ref, send_sems, recv_sems = rest[len(deps):]
        x, y, c = _place()
        copies = []
        for k, (px, py) in enumerate([(1 - x, y), (x, 1 - y), (1 - x, 1 - y)]):
            slab = 4 * px + 2 * py + c
            copies.append(pltpu.make_async_remote_copy(
                src_ref=land_ref.at[slab], dst_ref=out_ref.at[slab], send_sem=send_sems.at[k], recv_sem=recv_sems.at[k],
                device_id=(x, y, 1 - c), device_id_type=MESH))
        for cp in copies:
            cp.start()
        for cp in copies:
            cp.wait()

    return pl.pallas_call(
        body, name=name, out_shape=_sds(land.shape, land.dtype), in_specs=[ANY] + [ANY] * len(deps), out_specs=ANY,
        input_output_aliases={0: 0},
        scratch_shapes=[pltpu.SemaphoreType.DMA((N_CHIP - 1,)), pltpu.SemaphoreType.DMA((N_CHIP - 1,))],
    )(land, *deps)


def _route_sibling(x, y, c):
    return [(2 * j + (1 - c), (x, y, 1 - c)) for j in range(N_CHIP)]


def _route_chips(x, y, c):
    return [(k, (px, py, c)) for k, (px, py) in enumerate([(1 - x, y), (x, 1 - y), (1 - x, 1 - y)])]


def _exchange_start(name, src, route):
    n_copy = len(route(0, 0, 0))

    def body(s_ref, land_ref, send_sems, recv_sems, s_out, land_out, token):
        for k, (slab, peer) in enumerate(route(*_place())):
            pltpu.make_async_remote_copy(
                src_ref=s_ref.at[slab], dst_ref=land_ref.at[k], send_sem=send_sems.at[k], recv_sem=recv_sems.at[k],
                device_id=peer, device_id_type=MESH).start()
        token[...] = jnp.zeros_like(token)

    land = lax.empty((n_copy,) + src.shape[1:], src.dtype)
    sem = pltpu.SemaphoreType.DMA((n_copy,))
    out = pl.pallas_call(
        body, name=name,
        out_shape=[sem, sem, pltpu.HBM(src.shape, src.dtype), pltpu.HBM(land.shape, land.dtype), _token_shape()],
        in_specs=[HBM, HBM], out_specs=[SEM, SEM, HBM, HBM, pl.BlockSpec(memory_space=pltpu.VMEM)],
        input_output_aliases={0: 2, 1: 3}, compiler_params=pltpu.CompilerParams(has_side_effects=EFFECT),
    )(_hbm(src), _hbm(land))
    return out[:4], out[4]


def _exchange_wait(name, started, route, after):
    send_sems, recv_sems, src, land = started

    def body(s_ref, land_ref, send_ref, recv_ref, after_ref, s_out, land_out):
        for k, (slab, peer) in enumerate(route(*_place())):
            cp = pltpu.make_async_remote_copy(
                src_ref=s_ref.at[slab], dst_ref=land_ref.at[k], send_sem=send_ref.at[k], recv_sem=recv_ref.at[k],
                device_id=peer, device_id_type=MESH)
            cp.wait_send()
            cp.wait_recv()

    return pl.pallas_call(
        body, name=name, out_shape=(pltpu.HBM(src.shape, src.dtype), pltpu.HBM(land.shape, land.dtype)),
        in_specs=(HBM, HBM, SEM, SEM, ANY), out_specs=(HBM, HBM), input_output_aliases={0: 0, 1: 1},
        compiler_params=pltpu.CompilerParams(has_side_effects=EFFECT),
    )(src, land, send_sems, recv_sems, after)


def _all_gather(name, land):
    def body(land_ref, out_ref, send_sems, recv_sems):
        x, y, c = _place()
        me, sibling = (x, y, c), (x, y, 1 - c)
        chips = [(1 - x, y), (x, 1 - y), (1 - x, 1 - y)]

        def copy(k, block, to):
            slab = 4 * block[0] + 2 * block[1] + block[2]
            return pltpu.make_async_remote_copy(
                src_ref=land_ref.at[slab], dst_ref=out_ref.at[slab],
                send_sem=send_sems.at[k], recv_sem=recv_sems.at[k], device_id=to, device_id_type=MESH)

        first = [copy(0, me, sibling)] + [copy(1 + j, me, (*chip, c)) for j, chip in enumerate(chips)]
        for cp in first:
            cp.start()
        passed = [copy(4 + j, (*chip, c), sibling) for j, chip in enumerate(chips)]
        for j, chip in enumerate(chips):
            copy(1 + j, (*chip, c), me).wait_recv()
            passed[j].start()
        copy(0, sibling, me).wait_recv()
        for j, chip in enumerate(chips):
            copy(4 + j, (*chip, 1 - c), me).wait_recv()
        for cp in first + passed:
            cp.wait_send()

    return pl.pallas_call(
        body, name=name, out_shape=_sds(land.shape, land.dtype), in_specs=[ANY], out_specs=ANY, input_output_aliases={0: 0},
        scratch_shapes=[pltpu.SemaphoreType.DMA((N_DEV - 1,)), pltpu.SemaphoreType.DMA((N_DEV - 1,))],
    )(land)


def _add_sibling(name, part, recv, place):
    _, r, c = part.shape
    tr = _tile(r, max(16, 2 * ELEM_BLOCK_BYTES // (2 * c)), 16)
    n_out = N_CHIP - 1

    def chip_of(k, x_ref, y_ref):
        px = jnp.where(k == 1, x_ref[0], 1 - x_ref[0])
        py = jnp.where(k == 0, y_ref[0], 1 - y_ref[0])
        return 2 * px + py

    def body(x_ref, y_ref, c_ref, p_ref, r_ref, o_ref):
        o_ref[...] = (p_ref[...].astype(F32) + r_ref[...].astype(F32)).astype(BF16)

    return pl.pallas_call(
        body, name=name,
        grid_spec=pltpu.PrefetchScalarGridSpec(
            num_scalar_prefetch=3, grid=(n_out, r // tr),
            in_specs=[pl.BlockSpec((None, tr, c), lambda k, i, x_ref, y_ref, c_ref: (2 * chip_of(k, x_ref, y_ref) + c_ref[0], i, 0)),
                      pl.BlockSpec((None, tr, c), lambda k, i, x_ref, y_ref, c_ref: (chip_of(k, x_ref, y_ref), i, 0))],
            out_specs=pl.BlockSpec((None, tr, c), lambda k, i, x_ref, y_ref, c_ref: (k, i, 0))),
        out_shape=_sds((n_out, r, c), BF16), compiler_params=_params(("parallel", "parallel")),
    )(*place, part, recv)


def _adamw_math(w, g, m, v):
    m = ADAM_B1 * m + (1.0 - ADAM_B1) * g
    v = ADAM_B2 * v + (1.0 - ADAM_B2) * (g * g)
    m_hat = m / (1.0 - ADAM_B1 ** ADAM_STEP)
    v_hat = v / (1.0 - ADAM_B2 ** ADAM_STEP)
    delta = -ADAM_LR * (m_hat / (jnp.sqrt(v_hat) + ADAM_EPS) + ADAM_WD * w)
    return delta, m, v


def _adamw_big(name, part, from_sibling, from_chips, dev, chip, w, m, v):
    r, c = w.shape
    tr = _tile(r, max(16, ELEM_BLOCK_BYTES // (4 * c)), 16)

    def body(dev_ref, chip_ref, p_ref, s_ref, r_ref, w_ref, m_ref, v_ref, g_out, d_out, m_out, v_out):
        g = p_ref[...].astype(F32) + s_ref[...].astype(F32)
        for k in range(N_CHIP - 1):
            g = g + r_ref[k].astype(F32)
        delta, m_new, v_new = _adamw_math(w_ref[...], g, m_ref[...], v_ref[...])
        g_out[...] = g
        d_out[...] = delta
        m_out[...] = m_new
        v_out[...] = v_new

    blk = pl.BlockSpec((tr, c), lambda i, dev_ref, chip_ref: (i, 0))
    return pl.pallas_call(
        body, name=name,
        grid_spec=pltpu.PrefetchScalarGridSpec(
            num_scalar_prefetch=2, grid=(r // tr,),
            in_specs=[pl.BlockSpec((None, tr, c), lambda i, dev_ref, chip_ref: (dev_ref[0], i, 0)),
                      pl.BlockSpec((None, tr, c), lambda i, dev_ref, chip_ref: (chip_ref[0], i, 0)),
                      pl.BlockSpec((N_CHIP - 1, tr, c), lambda i, dev_ref, chip_ref: (0, i, 0)), blk, blk, blk],
            out_specs=[blk, blk, blk, blk]),
        out_shape=[_sds((r, c), F32)] * 4, compiler_params=_params(("parallel",)),
    )(dev, chip, part, from_sibling, from_chips, w, m, v)


def _sum_parts(name, parts):
    n, r, c = parts.shape
    tr = _tile(r, max(8, ELEM_BLOCK_BYTES // (4 * c)), 8)

    def body(p_ref, o_ref):
        acc = p_ref[0]
        for k in range(1, n):
            acc = acc + p_ref[k]
        o_ref[...] = acc

    return pl.pallas_call(body, name=name, grid=(r // tr,), in_specs=[pl.BlockSpec((n, tr, c), lambda i: (0, i, 0))],
                          out_specs=pl.BlockSpec((tr, c), lambda i: (i, 0)), out_shape=_sds((r, c), F32),
                          compiler_params=_params(("parallel",)))(parts)


def _adamw_small(name, ws, gs, ms, vs):
    n = len(ws)

    def body(*refs):
        ins, outs = refs[:4 * n], refs[4 * n:]
        for i in range(n):
            delta, m_new, v_new = _adamw_math(ins[i][...], ins[n + i][...], ins[2 * n + i][...], ins[3 * n + i][...])
            outs[i][...] = delta
            outs[n + i][...] = m_new
            outs[2 * n + i][...] = v_new

    shapes = [_sds(w.shape, F32) for w in ws]
    return pl.pallas_call(body, name=name, out_shape=shapes * 3)(*ws, *gs, *ms, *vs)


def _pad_rows(a, rows):
    return jnp.pad(a, ((0, rows - a.shape[0]), (0, 0)))


def kernel(x, meta_tokens, ffn1_w_gu, ffn1_w_down, ln1_g, ln1_b, w_in, conv_w, pool_w, pool_scale, w_out, ln2_g, ln2_b, ffn2_w_gu, ffn2_w_down, ln3_g, ln3_b, loss_target, m_meta_tokens, m_ffn1_w_gu, m_ffn1_w_down, m_ln1_g, m_ln1_b, m_w_in, m_conv_w, m_pool_w, m_pool_scale, m_w_out, m_ln2_g, m_ln2_b, m_ffn2_w_gu, m_ffn2_w_down, m_ln3_g, m_ln3_b, v_meta_tokens, v_ffn1_w_gu, v_ffn1_w_down, v_ln1_g, v_ln1_b, v_w_in, v_conv_w, v_pool_w, v_pool_scale, v_w_out, v_ln2_g, v_ln2_b, v_ffn2_w_gu, v_ffn2_w_down, v_ln3_g, v_ln3_b):
    names = ["meta_tokens", "ffn1_w_gu", "ffn1_w_down", "ln1_g", "ln1_b", "w_in", "conv_w", "pool_w", "pool_scale", "w_out",
             "ln2_g", "ln2_b", "ffn2_w_gu", "ffn2_w_down", "ln3_g", "ln3_b"]
    w_of = dict(zip(names, [meta_tokens, ffn1_w_gu, ffn1_w_down, ln1_g, ln1_b, w_in, conv_w, pool_w, pool_scale, w_out,
                            ln2_g, ln2_b, ffn2_w_gu, ffn2_w_down, ln3_g, ln3_b]))
    m_of = dict(zip(names, [m_meta_tokens, m_ffn1_w_gu, m_ffn1_w_down, m_ln1_g, m_ln1_b, m_w_in, m_conv_w, m_pool_w, m_pool_scale,
                            m_w_out, m_ln2_g, m_ln2_b, m_ffn2_w_gu, m_ffn2_w_down, m_ln3_g, m_ln3_b]))
    v_of = dict(zip(names, [v_meta_tokens, v_ffn1_w_gu, v_ffn1_w_down, v_ln1_g, v_ln1_b, v_w_in, v_conv_w, v_pool_w, v_pool_scale,
                            v_w_out, v_ln2_g, v_ln2_b, v_ffn2_w_gu, v_ffn2_w_down, v_ln3_g, v_ln3_b]))

    n_seq, seq, d = x.shape
    seq_len = seq + N_META
    n_rows = n_seq * seq_len
    tp = -(-n_rows // ROW_ALIGN) * ROW_ALIGN
    c_conv = conv_w.shape[2] * N_DEV
    p_pool = pool_scale.shape[1]
    pg = pool_w.shape[3]
    assert c_conv == p_pool and p_pool == N_POOL_GROUPS * pg and POOL_WINDOWS == tuple(2 << g for g in range(N_POOL_GROUPS))
    assert (N_POOL_GROUPS * pg * pg) % d == 0 and pg % LANE == 0

    xi, yi, ci = _place()
    dev_index = 4 * xi + 2 * yi + ci
    dev = jnp.reshape(dev_index, (1,)).astype(jnp.int32)
    chip = jnp.reshape(2 * xi + yi, (1,)).astype(jnp.int32)
    place = tuple(jnp.reshape(a, (1,)).astype(jnp.int32) for a in (xi, yi, ci))

    big = ["ffn1_w_gu", "ffn1_w_down", "w_in", "w_out", "ffn2_w_gu", "ffn2_w_down"]
    wcol = d // N_DEV
    conv_rows = 8
    small_local = jnp.concatenate([
        meta_tokens,
        pool_w[0].reshape(N_POOL_GROUPS * (pg // N_DEV), pg),
        jnp.pad(conv_w[0], ((0, conv_rows - CONV_K), (0, wcol - conv_w.shape[2]))),
    ], axis=0)
    lands = {n: _into_slab(f"slab_{n}", w_of[n][0], dev, BF16) for n in big}
    lands["small"] = _into_slab("slab_small", small_local, dev, F32)
    started = {}

    def start(tag, which, after=()):
        per, token = _gather_start(f"ag_start_{tag}", [lands[n] for n in which], after)
        started.update(zip(which, per))
        return token

    def gathered(n, after, then_start=()):
        land = _gather_wait(f"ag_wait_{n}", started[n], after)
        deps = (start(f"after_{n}", then_start, (land,)),) if then_start else ()
        return _gather_finish(f"ag_finish_{n}", land, deps)

    gather_token = start("first", ["small", "ffn1_w_gu"])
    small_all = gathered("small", gather_token)
    r0, r1 = N_META, N_META + N_POOL_GROUPS * (pg // N_DEV)
    meta_full = jnp.transpose(small_all[:, :r0], (1, 0, 2)).reshape(N_META, d)
    pool_w_full = jnp.transpose(small_all[:, r0:r1].reshape(N_DEV, N_POOL_GROUPS, pg // N_DEV, pg), (1, 0, 2, 3)).reshape(N_POOL_GROUPS, pg, pg)
    conv_w_full = jnp.transpose(small_all[:, r1:r1 + CONV_K, :conv_w.shape[2]], (1, 0, 2)).reshape(CONV_K, c_conv)
    pool_w_b = pool_w_full.astype(BF16)

    h0 = jnp.concatenate([jnp.broadcast_to(meta_full[None], (n_seq, N_META, d)), x], axis=1).reshape(n_rows, d)
    h0 = _pad_rows(h0, tp)
    h0_b = h0.astype(BF16)
    tgt = _pad_rows(jnp.pad(loss_target, ((0, 0), (N_META, 0), (0, 0))).reshape(n_rows, d), tp)

    wgu1 = gathered("ffn1_w_gu", h0_b, ["ffn1_w_down", "w_in"])
    gu1, act1 = _ffn_gu("ffn1_gu", h0_b, wgu1)
    wd1 = gathered("ffn1_w_down", act1, ["w_out", "ffn2_w_gu"]).reshape(N_CHIP, -1, d)
    pre1 = _ffn_down("ffn1_down", act1, wd1, h0)
    win_all = gathered("w_in", pre1)
    h1, h1_b = _ln_fwd("ln1", pre1, ln1_g, ln1_b)

    u = _proj_in("mix_in", h1_b, win_all)
    wout_all = gathered("w_out", u)
    y_conv = _conv_fwd("mix_conv", u, conv_w_full, n_rows, seq_len)
    dpool = _pool_fwd("mix_pool", u, 3 * c_conv, p_pool, pg, n_rows, seq_len)
    ypre, y_pool = _pool_mix("mix_pool_w", dpool, pool_w_b, pool_scale)
    y_mix = jnp.concatenate([y_conv, y_pool], axis=1)
    pre2 = _proj_out("mix_out", y_mix, wout_all, h1)
    wgu2 = gathered("ffn2_w_gu", pre2, ["ffn2_w_down"])
    h2, h2_b = _ln_fwd("ln2", pre2, ln2_g, ln2_b)

    gu2, act2 = _ffn_gu("ffn2_gu", h2_b, wgu2)
    wd2 = gathered("ffn2_w_down", act2).reshape(N_CHIP, -1, d)
    pre3 = _ffn_down("ffn2_down", act2, wd2, h2)

    dpre3, dpre3_b, d_ln3_g, d_ln3_b, sq = _ln_loss_bwd("ln3_loss", pre3, tgt, ln3_g, ln3_b, n_rows, seq_len)
    loss = lax.psum(0.5 * jnp.sum(sq) / d, ("x", "y", "c"))
    in_sibling, reducing = [], {}

    def exchange(after, new=None):
        tokens = []
        while in_sibling:
            n, started = in_sibling.pop(0)
            part, from_sibling = _exchange_wait(f"rs_sibling_wait_{n}", started, _route_sibling, after)
            summed = _add_sibling(f"rs_add_{n}", part, from_sibling, place)
            started, token = _exchange_start(f"rs_chips_start_{n}", summed, _route_chips)
            reducing[n] = (part, from_sibling, started)
            tokens.append(token)
        if new is not None:
            started, token = _exchange_start(f"rs_sibling_start_{new[0]}", new[1], _route_sibling)
            in_sibling.append((new[0], started))
            tokens.append(token)
        return tuple(tokens)

    dgu2 = _ffn_bwd_dgu("ffn2_bwd_dgu", dpre3_b, wd2, gu2)
    g_wd2 = _ffn_bwd_wd("ffn2_bwd_wd", act2, dpre3_b)
    tokens = exchange(g_wd2, ("ffn2_w_down", g_wd2.reshape(N_DEV, -1, d)))
    g_wgu2 = _ffn_bwd_wgu("ffn2_bwd_wgu", h2_b, dgu2, tokens)
    tokens = exchange(g_wgu2, ("ffn2_w_gu", g_wgu2))
    dh2 = _ffn_bwd_dh("ffn2_bwd_dh", dgu2, wgu2, dpre3, tokens)
    tokens = exchange(dh2)
    dpre2, dpre2_b, d_ln2_g, d_ln2_b = _ln_bwd("ln2_bwd", pre2, dh2, ln2_g)

    dy_mix = _proj_out_bwd_y("mix_out_bwd_y", dpre2_b, wout_all, tokens)
    g_wout = _proj_out_bwd_w("mix_out_bwd_w", y_mix, dpre2_b, N_DEV)
    tokens = exchange(g_wout, ("w_out", g_wout))
    dyps, d_pool_scale = _pool_scale_bwd("mix_pool_scale_bwd", dy_mix, 1, ypre, pool_scale)
    dd = _pool_mix_bwd_in("mix_pool_w_bwd_in", dyps, pool_w_b)
    d_pool_w = _pool_mix_bwd_w("mix_pool_w_bwd_w", dpool, dyps, pg)
    du_pool = _pool_bwd("mix_pool_bwd", dd, pg, n_rows, seq_len)
    du_b, du_c, du_x, d_conv_w = _conv_bwd("mix_conv_bwd", u, conv_w_full, dy_mix, n_rows, seq_len)
    du = jnp.concatenate([du_b, du_c, du_x, du_pool], axis=1)
    g_win = _proj_in_bwd_w("mix_in_bwd_w", h1_b, du, N_DEV, tokens)
    tokens = exchange(g_win, ("w_in", g_win))
    dh1 = _proj_in_bwd_h("mix_in_bwd_h", du, win_all, dpre2, tokens)
    tokens = exchange(dh1)
    dpre1, dpre1_b, d_ln1_g, d_ln1_b = _ln_bwd("ln1_bwd", pre1, dh1, ln1_g)

    def widen(a):
        return jnp.pad(a, ((0, 0), (0, d - a.shape[1])))

    small_part = jnp.concatenate([
        d_ln1_g, d_ln1_b, d_ln2_g, d_ln2_b, d_ln3_g, d_ln3_b, widen(d_pool_scale), widen(d_conv_w), d_pool_w.reshape(-1, d)], axis=0)
    n_small_rows = small_part.shape[0]
    small_part = _pad_rows(small_part, -(-n_small_rows // 8) * 8)
    small_started, token = _gather_start("ag_start_small_grads", [_into_slab("slab_small_grads", small_part, dev, F32)])

    dgu1 = _ffn_bwd_dgu("ffn1_bwd_dgu", dpre1_b, wd1, gu1, tokens + (token,))
    g_wgu1 = _ffn_bwd_wgu("ffn1_bwd_wgu", h0_b, dgu1)
    tokens = exchange(g_wgu1, ("ffn1_w_gu", g_wgu1))
    g_wd1 = _ffn_bwd_wd("ffn1_bwd_wd", act1, dpre1_b, tokens)
    tokens = exchange(g_wd1, ("ffn1_w_down", g_wd1.reshape(N_DEV, -1, d)))
    dh0 = _ffn_bwd_dh("ffn1_bwd_dh", dgu1, wgu1, dpre1, tokens)
    tokens = exchange(dh0)

    dh0_seq = dh0[:n_rows].reshape(n_seq, seq_len, d)
    grad_x = dh0_seq[:, N_META:]
    d_meta = jnp.sum(dh0_seq[:, :N_META], axis=0)

    grads, deltas, new_m, new_v = {}, {}, {}, {}
    after = tokens[0]
    for n in ["ffn2_w_down", "ffn2_w_gu", "w_out", "w_in", "ffn1_w_gu", "ffn1_w_down"]:
        part, from_sibling, started = reducing[n]
        _, from_chips = _exchange_wait(f"rs_chips_wait_{n}", started, _route_chips, after)
        g, dl, mm, vv = _adamw_big(f"adamw_{n}", part, from_sibling, from_chips, dev, chip, w_of[n][0], m_of[n][0], v_of[n][0])
        grads[n], deltas[n], new_m[n], new_v[n] = g[None], dl[None], mm[None], vv[None]
        after = g

    small_sum = _sum_parts("small_sum", _gather_finish("ag_finish_small_grads", _gather_wait("ag_wait_small_grads", small_started[0], dh0)))
    meta_sum = _sum_parts("meta_sum", _all_gather("ag_meta_grads", _into_slab("slab_meta_grads", d_meta, dev, F32)))
    o = 7 + CONV_K
    g_small = {
        "ln1_g": small_sum[0:1], "ln1_b": small_sum[1:2], "ln2_g": small_sum[2:3], "ln2_b": small_sum[3:4],
        "ln3_g": small_sum[4:5], "ln3_b": small_sum[5:6], "pool_scale": small_sum[6:7, :p_pool],
        "conv_w": lax.dynamic_slice_in_dim(small_sum[7:o, :c_conv], dev_index * (c_conv // N_DEV), c_conv // N_DEV, axis=1)[None],
        "meta_tokens": lax.dynamic_slice_in_dim(meta_sum, dev_index * wcol, wcol, axis=1),
        "pool_w": lax.dynamic_slice_in_dim(small_sum[o:n_small_rows].reshape(N_POOL_GROUPS, pg, pg),
                                           dev_index * (pg // N_DEV), pg // N_DEV, axis=1)[None],
    }
    small = ["meta_tokens", "ln1_g", "ln1_b", "conv_w", "pool_w", "pool_scale", "ln2_g", "ln2_b", "ln3_g", "ln3_b"]

    def flat(a):
        return a.reshape(-1, a.shape[-1])

    outs = _adamw_small("adamw_small", [flat(w_of[n]) for n in small], [flat(g_small[n]) for n in small],
                        [flat(m_of[n]) for n in small], [flat(v_of[n]) for n in small])
    ns = len(small)
    for i, n in enumerate(small):
        shape = w_of[n].shape
        grads[n] = g_small[n].reshape(shape)
        deltas[n], new_m[n], new_v[n] = outs[i].reshape(shape), outs[ns + i].reshape(shape), outs[2 * ns + i].reshape(shape)

    return (loss, grad_x, *[grads[n] for n in names], *[deltas[n] for n in names],
            *[new_m[n] for n in names], *[new_v[n] for n in names])
```

```python
import functools

import jax
import jax.numpy as jnp
from jax import lax
from jax.experimental import pallas as pl
from jax.experimental.pallas import tpu as pltpu

N_DEV = 8
N_CHIP = 4
N_META = 16
CONV_K = 3
POOL_WINDOWS = (2, 4, 8, 16)
N_POOL_GROUPS = len(POOL_WINDOWS)
LN_EPS = 1e-5
DEPTH = 1
ALPHA = (2.0 * DEPTH) ** 0.25
ADAM_LR = 0.001
ADAM_B1 = 0.9
ADAM_B2 = 0.999
ADAM_EPS = 1e-08
ADAM_WD = 0.01
ADAM_STEP = 10

V7X_VMEM_BYTES = 64 * 1024 * 1024
VMEM_LIMIT = V7X_VMEM_BYTES - 6 * 1024 * 1024
LANE = 128
ROW_ALIGN = 3 * LANE
TM_BIG = 1408
TM_WIDE = 704
TK = 512
TK_TOKENS = 1408
TN = 1024
TR_LN = 128
ELEM_BLOCK_BYTES = 1 << 20
TC_MIX = LANE
EPILOGUE_ROWS = 64

NN = (((1,), (0,)), ((), ()))
NT = (((1,), (1,)), ((), ()))
TN_DIMS = (((0,), (0,)), ((), ()))
MESH = pl.DeviceIdType.MESH
BF16 = jnp.bfloat16
F32 = jnp.float32
ANY = pl.BlockSpec(memory_space=pl.ANY)
HBM = pl.BlockSpec(memory_space=pltpu.HBM)
SEM = pl.BlockSpec(memory_space=pltpu.SEMAPHORE)
EFFECT = pltpu.SideEffectType.DATAFLOW_SIDE_EFFECTING


def _tile(n, target, mult):
    best = None
    for t in range(mult, min(n, target) + 1, mult):
        if n % t == 0:
            best = t
    return n if best is None else best


def _params(sem):
    return pltpu.CompilerParams(dimension_semantics=sem, vmem_limit_bytes=VMEM_LIMIT)


def _sds(shape, dtype):
    return jax.ShapeDtypeStruct(shape, dtype)


def _row_chunks(n_rows, fn):
    ch = _tile(n_rows, EPILOGUE_ROWS, 16)

    def step(i, carry):
        fn(pl.ds(pl.multiple_of(i * ch, ch), ch))
        return carry

    lax.fori_loop(0, n_rows // ch, step, 0)


def _mm(name, grid, dims, ab, ab_specs, extras, extra_specs, out_shape, out_specs, acc_shape, epilogue, deps=()):
    nk = grid[-1]
    n_extra = len(extras)
    n_in = 2 + n_extra + len(deps)
    n_out = len(out_shape)
    kax = len(grid) - 1

    def body(*refs):
        a_ref, b_ref = refs[0], refs[1]
        ex = refs[2:2 + n_extra]
        outs = refs[n_in:n_in + n_out]
        if nk == 1:
            epilogue(lax.dot_general(a_ref[...], b_ref[...], dims, preferred_element_type=F32), ex, outs, slice(None))
            return
        acc = refs[-1]
        k = pl.program_id(kax)

        @pl.when(k == 0)
        def _():
            acc[...] = jnp.zeros_like(acc)

        acc[...] += lax.dot_general(a_ref[...], b_ref[...], dims, preferred_element_type=F32)

        @pl.when(k == nk - 1)
        def _():
            _row_chunks(acc_shape[0], lambda rows: epilogue(acc[rows, :], ex, outs, rows))

    scratch = [] if nk == 1 else [pltpu.VMEM(acc_shape, F32)]
    sem = ("parallel",) * kax + ("arbitrary",)
    return pl.pallas_call(
        body, name=name, grid=grid, in_specs=list(ab_specs) + list(extra_specs) + [ANY] * len(deps), out_specs=list(out_specs),
        out_shape=list(out_shape), scratch_shapes=scratch, compiler_params=_params(sem),
    )(*ab, *extras, *deps)


def _silu_parts(g):
    s = 1.0 / (1.0 + jnp.exp(-g))
    return s, g * s


def _ffn_gu(name, h_b, wgu_all):
    tp, d = h_b.shape
    ns, _, ng = wgu_all.shape
    half = ns // 2
    tm, tk = _tile(tp, TM_WIDE, 16), _tile(d, TK, LANE)
    grid = (tp // tm, half, d // tk)
    nk = grid[-1]

    def body(h_ref, wg_ref, wu_ref, gu_ref, act_ref, acc_g, acc_u):
        k = pl.program_id(2)

        @pl.when(k == 0)
        def _():
            acc_g[...] = jnp.zeros_like(acc_g)
            acc_u[...] = jnp.zeros_like(acc_u)

        acc_g[...] += jnp.dot(h_ref[...], wg_ref[...], preferred_element_type=F32)
        acc_u[...] += jnp.dot(h_ref[...], wu_ref[...], preferred_element_type=F32)

        @pl.when(k == nk - 1)
        def _():
            def finish(rows):
                g = acc_g[rows, :]
                u = acc_u[rows, :]
                _, silu = _silu_parts(g)
                gu_ref[0, rows, :] = g.astype(BF16)
                gu_ref[1, rows, :] = u.astype(BF16)
                act_ref[rows, :] = (silu * u).astype(BF16)

            _row_chunks(tm, finish)

    return pl.pallas_call(
        body, name=name, grid=grid,
        in_specs=[pl.BlockSpec((tm, tk), lambda m, s, k: (m, k)),
                  pl.BlockSpec((None, tk, ng), lambda m, s, k: (s, k, 0)),
                  pl.BlockSpec((None, tk, ng), lambda m, s, k: (s + half, k, 0))],
        out_specs=[pl.BlockSpec((None, 2, tm, ng), lambda m, s, k: (s, 0, m, 0)),
                   pl.BlockSpec((None, tm, ng), lambda m, s, k: (s, m, 0))],
        out_shape=[_sds((half, 2, tp, ng), BF16), _sds((half, tp, ng), BF16)],
        scratch_shapes=[pltpu.VMEM((tm, ng), F32), pltpu.VMEM((tm, ng), F32)],
        compiler_params=_params(("parallel", "parallel", "arbitrary")),
    )(h_b, wgu_all, wgu_all)


def _ffn_down(name, act, wd4, h):
    ns, tp, ng = act.shape
    d = wd4.shape[2]
    tm, tn = _tile(tp, TM_WIDE, 16), _tile(d, TN, LANE)

    def epi(acc, ex, outs, rows):
        outs[0][rows, :] = ALPHA * ex[0][rows, :] + 0.5 * acc

    return _mm(name, (tp // tm, d // tn, ns), NN, (act, wd4),
               [pl.BlockSpec((None, tm, ng), lambda m, n, s: (s, m, 0)),
                pl.BlockSpec((None, ng, tn), lambda m, n, s: (s, 0, n))],
               (h,), [pl.BlockSpec((tm, tn), lambda m, n, s: (m, n))],
               [_sds((tp, d), F32)], [pl.BlockSpec((tm, tn), lambda m, n, s: (m, n))], (tm, tn), epi)[0]


def _ffn_bwd_dgu(name, dp_b, wd4, gu, deps=()):
    tp, d = dp_b.shape
    ns, ng, _ = wd4.shape
    tm, tk = _tile(tp, TM_WIDE, 16), _tile(d, TK, LANE)

    def epi(acc, ex, outs, rows):
        g = ex[0][0, rows, :].astype(F32)
        u = ex[0][1, rows, :].astype(F32)
        da = 0.5 * acc
        s, silu = _silu_parts(g)
        outs[0][0, rows, :] = (da * u * (s + silu * (1.0 - s))).astype(BF16)
        outs[0][1, rows, :] = (da * silu).astype(BF16)

    return _mm(name, (tp // tm, ns, d // tk), NT, (dp_b, wd4),
               [pl.BlockSpec((tm, tk), lambda m, s, k: (m, k)),
                pl.BlockSpec((None, ng, tk), lambda m, s, k: (s, 0, k))],
               (gu,), [pl.BlockSpec((None, 2, tm, ng), lambda m, s, k: (s, 0, m, 0))],
               [_sds((ns, 2, tp, ng), BF16)], [pl.BlockSpec((None, 2, tm, ng), lambda m, s, k: (s, 0, m, 0))],
               (tm, ng), epi, deps)[0]


def _ffn_bwd_wd(name, act, dp_b, deps=()):
    ns, tp, ng = act.shape
    d = dp_b.shape[1]
    tkt, tn = _tile(tp, TK_TOKENS, LANE), _tile(d, TN, LANE)

    def epi(acc, ex, outs, rows):
        outs[0][rows, :] = (0.5 * acc).astype(BF16)

    return _mm(name, (ns, d // tn, tp // tkt), TN_DIMS, (act, dp_b),
               [pl.BlockSpec((None, tkt, ng), lambda s, n, t: (s, t, 0)),
                pl.BlockSpec((tkt, tn), lambda s, n, t: (t, n))],
               (), [], [_sds((ns, ng, d), BF16)], [pl.BlockSpec((None, ng, tn), lambda s, n, t: (s, 0, n))],
               (ng, tn), epi, deps)[0]


def _ffn_bwd_wgu(name, h_b, dgu, deps=()):
    tp, d = h_b.shape
    ns, _, _, ng = dgu.shape
    tkt, tmd = _tile(tp, TK_TOKENS, LANE), _tile(d, TN, LANE)

    def epi(acc, ex, outs, rows):
        outs[0][rows, :] = acc.astype(BF16)

    return _mm(name, (ns, 2, d // tmd, tp // tkt), TN_DIMS, (h_b, dgu),
               [pl.BlockSpec((tkt, tmd), lambda s, j, i, t: (t, i)),
                pl.BlockSpec((None, None, tkt, ng), lambda s, j, i, t: (s, j, t, 0))],
               (), [], [_sds((2 * ns, d, ng), BF16)],
               [pl.BlockSpec((None, tmd, ng), lambda s, j, i, t: (j * ns + s, i, 0))], (tmd, ng), epi, deps)[0]


def _ffn_bwd_dh(name, dgu, wgu_all, dp, deps=()):
    ns, _, tp, ng = dgu.shape
    d = wgu_all.shape[1]
    tm, tn = _tile(tp, TM_WIDE, 16), _tile(d, TN, LANE)

    def epi(acc, ex, outs, rows):
        outs[0][rows, :] = ALPHA * ex[0][rows, :] + acc

    return _mm(name, (tp // tm, d // tn, 2 * ns), NT, (dgu, wgu_all),
               [pl.BlockSpec((None, None, tm, ng), lambda m, n, j: (j % ns, j // ns, m, 0)),
                pl.BlockSpec((None, tn, ng), lambda m, n, j: (j, n, 0))],
               (dp,), [pl.BlockSpec((tm, tn), lambda m, n, j: (m, n))],
               [_sds((tp, d), F32)], [pl.BlockSpec((tm, tn), lambda m, n, j: (m, n))], (tm, tn), epi, deps)[0]


def _ln_stats(x):
    mu = jnp.mean(x, axis=-1, keepdims=True)
    xc = x - mu
    var = jnp.mean(xc * xc, axis=-1, keepdims=True)
    rstd = lax.rsqrt(var + LN_EPS)
    return xc * rstd, rstd


def _ln_bwd_rows(dy, xhat, rstd, g):
    dxh = dy * g
    m1 = jnp.mean(dxh, axis=-1, keepdims=True)
    m2 = jnp.mean(dxh * xhat, axis=-1, keepdims=True)
    return rstd * (dxh - m1 - xhat * m2)


def _ln_fwd(name, pre, g, b):
    tp, d = pre.shape
    tr = _tile(tp, TR_LN, 16)

    def body(x_ref, g_ref, b_ref, y_ref, yb_ref):
        xhat, _ = _ln_stats(x_ref[...])
        y = xhat * g_ref[...] + b_ref[...]
        y_ref[...] = y
        yb_ref[...] = y.astype(BF16)

    row = pl.BlockSpec((tr, d), lambda i: (i, 0))
    vec = pl.BlockSpec((1, d), lambda i: (0, 0))
    return pl.pallas_call(body, name=name, grid=(tp // tr,), in_specs=[row, vec, vec], out_specs=[row, row],
                          out_shape=[_sds((tp, d), F32), _sds((tp, d), BF16)],
                          compiler_params=_params(("parallel",)))(pre, g, b)


def _accumulate(i, ref, val):
    @pl.when(i == 0)
    def _():
        ref[...] = val

    @pl.when(i > 0)
    def _():
        ref[...] += val


def _ln_bwd(name, pre, dy, g):
    tp, d = pre.shape
    tr = _tile(tp, TR_LN, 16)

    def body(x_ref, dy_ref, g_ref, dx_ref, dxb_ref, dg_ref, db_ref):
        i = pl.program_id(0)
        xhat, rstd = _ln_stats(x_ref[...])
        dy = dy_ref[...]
        dx = _ln_bwd_rows(dy, xhat, rstd, g_ref[...])
        dx_ref[...] = dx
        dxb_ref[...] = dx.astype(BF16)
        _accumulate(i, dg_ref, jnp.sum(dy * xhat, axis=0, keepdims=True))
        _accumulate(i, db_ref, jnp.sum(dy, axis=0, keepdims=True))

    row = pl.BlockSpec((tr, d), lambda i: (i, 0))
    vec = pl.BlockSpec((1, d), lambda i: (0, 0))
    return pl.pallas_call(body, name=name, grid=(tp // tr,), in_specs=[row, row, vec], out_specs=[row, row, vec, vec],
                          out_shape=[_sds((tp, d), F32), _sds((tp, d), BF16), _sds((1, d), F32), _sds((1, d), F32)],
                          compiler_params=_params(("arbitrary",)))(pre, dy, g)


def _ln_loss_bwd(name, pre, tgt, g, b, n_rows, seq_len):
    tp, d = pre.shape
    tr = _tile(tp, TR_LN, 16)
    n_seq = n_rows // seq_len

    def body(x_ref, t_ref, g_ref, b_ref, dx_ref, dxb_ref, dg_ref, db_ref, sq_ref):
        i = pl.program_id(0)
        xhat, rstd = _ln_stats(x_ref[...])
        gain = g_ref[...]
        y = xhat * gain + b_ref[...]
        r = i * tr + lax.broadcasted_iota(jnp.int32, (tr, 1), 0)
        pos = r
        for s in range(1, n_seq):
            pos = jnp.where(r >= s * seq_len, r - s * seq_len, pos)
        live = jnp.logical_and(r < n_rows, pos >= N_META)
        err = jnp.where(live, y - t_ref[...], 0.0)
        dy = err * (1.0 / d)
        dx = _ln_bwd_rows(dy, xhat, rstd, gain)
        dx_ref[...] = dx
        dxb_ref[...] = dx.astype(BF16)
        _accumulate(i, dg_ref, jnp.sum(dy * xhat, axis=0, keepdims=True))
        _accumulate(i, db_ref, jnp.sum(dy, axis=0, keepdims=True))
        _accumulate(i, sq_ref, jnp.sum(err * err, axis=0, keepdims=True))

    row = pl.BlockSpec((tr, d), lambda i: (i, 0))
    vec = pl.BlockSpec((1, d), lambda i: (0, 0))
    return pl.pallas_call(
        body, name=name, grid=(tp // tr,), in_specs=[row, row, vec, vec], out_specs=[row, row, vec, vec, vec],
        out_shape=[_sds((tp, d), F32), _sds((tp, d), BF16), _sds((1, d), F32), _sds((1, d), F32), _sds((1, d), F32)],
        compiler_params=_params(("arbitrary",)))(pre, tgt, g, b)


def _proj_in(name, h_b, win_all):
    tp, d = h_b.shape
    ns, _, ni = win_all.shape
    tm, tk = _tile(tp, TM_BIG, 16), _tile(d, TK, LANE)

    def epi(acc, ex, outs, rows):
        outs[0][rows, :] = acc

    return _mm(name, (tp // tm, ns, d // tk), NN, (h_b, win_all),
               [pl.BlockSpec((tm, tk), lambda m, j, k: (m, k)),
                pl.BlockSpec((None, tk, ni), lambda m, j, k: (j, k, 0))],
               (), [], [_sds((tp, ns * ni), F32)], [pl.BlockSpec((tm, ni), lambda m, j, k: (m, j))], (tm, ni), epi)[0]


def _positions(tp, n_rows, seq_len):
    r = lax.broadcasted_iota(jnp.int32, (tp, 1), 0)
    pos = r
    for s in range(1, n_rows // seq_len):
        pos = jnp.where(r >= s * seq_len, r - s * seq_len, pos)
    return pos


def _shift_down(x, s, pos):
    return jnp.where(pos >= s, pltpu.roll(x, s, 0), 0.0)


def _shift_up(x, s, pos, seq_len):
    return jnp.where(pos + s < seq_len, pltpu.roll(x, x.shape[0] - s, 0), 0.0)


def _conv_fwd(name, u, conv_w, n_rows, seq_len):
    tp = u.shape[0]
    c = conv_w.shape[1]
    tc = _tile(c, TC_MIX, LANE)
    nb = c // tc

    def body(gb_ref, gc_ref, xi_ref, w_ref, y_ref):
        pos = _positions(tp, n_rows, seq_len)
        v = gc_ref[...] * xi_ref[...]
        w = w_ref[...]
        y = _shift_down(v, 2, pos) * w[0:1]
        y = y + _shift_down(v, 1, pos) * w[1:2]
        y = y + v * w[2:3]
        y_ref[...] = (gb_ref[...] * y).astype(BF16)

    col = lambda off: pl.BlockSpec((tp, tc), lambda i: (0, off + i))
    return pl.pallas_call(body, name=name, grid=(nb,), in_specs=[col(0), col(nb), col(2 * nb), pl.BlockSpec((CONV_K, tc), lambda i: (0, i))],
                          out_specs=pl.BlockSpec((tp, tc), lambda i: (0, i)), out_shape=_sds((tp, c), BF16),
                          compiler_params=_params(("parallel",)))(u, u, u, conv_w)


def _conv_bwd(name, u, conv_w, dy, n_rows, seq_len):
    tp = u.shape[0]
    c = conv_w.shape[1]
    tc = _tile(c, TC_MIX, LANE)
    nb = c // tc

    def body(gb_ref, gc_ref, xi_ref, w_ref, dy_ref, dgb_ref, dgc_ref, dxi_ref, dw_ref):
        pos = _positions(tp, n_rows, seq_len)
        gc, xi = gc_ref[...], xi_ref[...]
        v = gc * xi
        w = w_ref[...]
        v2, v1 = _shift_down(v, 2, pos), _shift_down(v, 1, pos)
        conv = v2 * w[0:1]
        conv = conv + v1 * w[1:2]
        conv = conv + v * w[2:3]
        dyc = dy_ref[...]
        dgb_ref[...] = (dyc * conv).astype(BF16)
        dconv = dyc * gb_ref[...]
        dv = dconv * w[2:3] + _shift_up(dconv, 1, pos, seq_len) * w[1:2] + _shift_up(dconv, 2, pos, seq_len) * w[0:1]
        dgc_ref[...] = (dv * xi).astype(BF16)
        dxi_ref[...] = (dv * gc).astype(BF16)
        dw_ref[0:1, :] = jnp.sum(dconv * v2, axis=0, keepdims=True)
        dw_ref[1:2, :] = jnp.sum(dconv * v1, axis=0, keepdims=True)
        dw_ref[2:3, :] = jnp.sum(dconv * v, axis=0, keepdims=True)

    col = lambda off: pl.BlockSpec((tp, tc), lambda i: (0, off + i))
    wspec = pl.BlockSpec((CONV_K, tc), lambda i: (0, i))
    return pl.pallas_call(body, name=name, grid=(nb,), in_specs=[col(0), col(nb), col(2 * nb), wspec, col(0)],
                          out_specs=[col(0), col(0), col(0), wspec],
                          out_shape=[_sds((tp, c), BF16)] * 3 + [_sds((CONV_K, c), F32)],
                          compiler_params=_params(("parallel",)))(u, u, u, conv_w, dy)


def _window_select(group, parts):
    out = parts[-1]
    for gi in range(len(parts) - 2, -1, -1):
        out = jnp.where(group == gi, parts[gi], out)
    return out


def _pool_fwd(name, u, col0, p, pg, n_rows, seq_len):
    tp = u.shape[0]
    tc = _tile(pg, TC_MIX, LANE)
    per_group = pg // tc

    def body(z_ref, d_ref):
        group = pl.program_id(0) // per_group
        pos = _positions(tp, n_rows, seq_len)
        z = z_ref[...]
        sums, s, w = [], z, 1
        for _ in POOL_WINDOWS:
            s = s + _shift_down(s, w, pos)
            w *= 2
            sums.append(s)
        total = _window_select(group, sums)
        count = jnp.minimum(pos + 1, 2 << group).astype(F32)
        d_ref[...] = (total / count - z).astype(BF16)

    return pl.pallas_call(body, name=name, grid=(p // tc,), in_specs=[pl.BlockSpec((tp, tc), lambda i: (0, col0 // tc + i))],
                          out_specs=pl.BlockSpec((tp, tc), lambda i: (0, i)), out_shape=_sds((tp, p), BF16),
                          compiler_params=_params(("parallel",)))(u)


def _pool_bwd(name, dd, pg, n_rows, seq_len):
    tp, p = dd.shape
    tc = _tile(pg, TC_MIX, LANE)
    per_group = pg // tc

    def body(dd_ref, dz_ref):
        group = pl.program_id(0) // per_group
        pos = _positions(tp, n_rows, seq_len)
        dd_v = dd_ref[...]
        count = jnp.minimum(pos + 1, 2 << group).astype(F32)
        sums, s, w = [], dd_v / count, 1
        for _ in POOL_WINDOWS:
            s = s + _shift_up(s, w, pos, seq_len)
            w *= 2
            sums.append(s)
        dz_ref[...] = (_window_select(group, sums) - dd_v).astype(BF16)

    spec = pl.BlockSpec((tp, tc), lambda i: (0, i))
    return pl.pallas_call(body, name=name, grid=(p // tc,), in_specs=[spec], out_specs=spec, out_shape=_sds((tp, p), BF16),
                          compiler_params=_params(("parallel",)))(dd)


def _pool_mix(name, dpool, pool_w_b, scale):
    tp, p = dpool.shape
    ng, pg, _ = pool_w_b.shape
    tm = _tile(tp, TM_BIG, 16)

    def epi(acc, ex, outs, rows):
        outs[0][rows, :] = acc
        outs[1][rows, :] = (acc * ex[0][...]).astype(BF16)

    blk = pl.BlockSpec((tm, pg), lambda m, g, k: (m, g))
    return _mm(name, (tp // tm, ng, 1), NN, (dpool, pool_w_b), [blk, pl.BlockSpec((None, pg, pg), lambda m, g, k: (g, 0, 0))],
               (scale,), [pl.BlockSpec((1, pg), lambda m, g, k: (0, g))],
               [_sds((tp, p), F32), _sds((tp, p), BF16)], [blk, blk], None, epi)


def _pool_scale_bwd(name, dy, col_block, ypre, scale):
    tp, p = ypre.shape
    tr = _tile(tp, TR_LN, 16)

    def body(dy_ref, yp_ref, s_ref, o_ref, ds_ref):
        i = pl.program_id(0)
        dyp = dy_ref[...]
        o_ref[...] = (dyp * s_ref[...]).astype(BF16)
        _accumulate(i, ds_ref, jnp.sum(dyp * yp_ref[...], axis=0, keepdims=True))

    row = pl.BlockSpec((tr, p), lambda i: (i, 0))
    vec = pl.BlockSpec((1, p), lambda i: (0, 0))
    return pl.pallas_call(body, name=name, grid=(tp // tr,), in_specs=[pl.BlockSpec((tr, p), lambda i: (i, col_block)), row, vec],
                          out_specs=[row, vec], out_shape=[_sds((tp, p), BF16), _sds((1, p), F32)],
                          compiler_params=_params(("arbitrary",)))(dy, ypre, scale)


def _pool_mix_bwd_in(name, dyps, pool_w_b):
    tp, p = dyps.shape
    ng, pg, _ = pool_w_b.shape
    tm = _tile(tp, TM_BIG, 16)

    def epi(acc, ex, outs, rows):
        outs[0][rows, :] = acc

    blk = pl.BlockSpec((tm, pg), lambda m, g, k: (m, g))
    return _mm(name, (tp // tm, ng, 1), NT, (dyps, pool_w_b), [blk, pl.BlockSpec((None, pg, pg), lambda m, g, k: (g, 0, 0))],
               (), [], [_sds((tp, p), F32)], [blk], None, epi)[0]


def _pool_mix_bwd_w(name, dpool, dyps, pg):
    tp, p = dpool.shape
    ng = p // pg
    tkt = _tile(tp, TM_BIG, LANE)

    def epi(acc, ex, outs, rows):
        outs[0][rows, :] = acc

    blk = pl.BlockSpec((tkt, pg), lambda g, t: (t, g))
    return _mm(name, (ng, tp // tkt), TN_DIMS, (dpool, dyps), [blk, blk], (), [],
               [_sds((ng, pg, pg), F32)], [pl.BlockSpec((None, pg, pg), lambda g, t: (g, 0, 0))], (pg, pg), epi)[0]


def _proj_out(name, y, wout_all, h):
    tp = y.shape[0]
    ns, ro, d = wout_all.shape
    tm, tn = _tile(tp, TM_BIG, 16), _tile(d, TN, LANE)

    def epi(acc, ex, outs, rows):
        outs[0][rows, :] = ALPHA * ex[0][rows, :] + acc

    mn = pl.BlockSpec((tm, tn), lambda m, n, j: (m, n))
    return _mm(name, (tp // tm, d // tn, ns), NN, (y, wout_all),
               [pl.BlockSpec((tm, ro), lambda m, n, j: (m, j)), pl.BlockSpec((None, ro, tn), lambda m, n, j: (j, 0, n))],
               (h,), [mn], [_sds((tp, d), F32)], [mn], (tm, tn), epi)[0]


def _proj_out_bwd_y(name, dp_b, wout_all, deps=()):
    tp, d = dp_b.shape
    ns, ro, _ = wout_all.shape
    tm, tk = _tile(tp, TM_BIG, 16), _tile(d, TK, LANE)

    def epi(acc, ex, outs, rows):
        outs[0][rows, :] = acc

    return _mm(name, (tp // tm, ns, d // tk), NT, (dp_b, wout_all),
               [pl.BlockSpec((tm, tk), lambda m, j, k: (m, k)), pl.BlockSpec((None, ro, tk), lambda m, j, k: (j, 0, k))],
               (), [], [_sds((tp, ns * ro), F32)], [pl.BlockSpec((tm, ro), lambda m, j, k: (m, j))], (tm, ro), epi, deps)[0]


def _proj_out_bwd_w(name, y, dp_b, ns):
    tp, c = y.shape
    d = dp_b.shape[1]
    ro = c // ns
    tkt, tn = _tile(tp, TM_BIG, LANE), _tile(d, 2 * TN, LANE)

    def epi(acc, ex, outs, rows):
        outs[0][rows, :] = acc.astype(BF16)

    return _mm(name, (ns, d // tn, tp // tkt), TN_DIMS, (y, dp_b),
               [pl.BlockSpec((tkt, ro), lambda j, n, t: (t, j)), pl.BlockSpec((tkt, tn), lambda j, n, t: (t, n))],
               (), [], [_sds((ns, ro, d), BF16)], [pl.BlockSpec((None, ro, tn), lambda j, n, t: (j, 0, n))], (ro, tn), epi)[0]


def _proj_in_bwd_w(name, h_b, du, ns, deps=()):
    tp, d = h_b.shape
    ni = du.shape[1] // ns
    tkt, tmd = _tile(tp, TK_TOKENS, LANE), _tile(d, 2 * TN, LANE)

    def epi(acc, ex, outs, rows):
        outs[0][rows, :] = acc.astype(BF16)

    return _mm(name, (ns, d // tmd, tp // tkt), TN_DIMS, (h_b, du),
               [pl.BlockSpec((tkt, tmd), lambda j, i, t: (t, i)), pl.BlockSpec((tkt, ni), lambda j, i, t: (t, j))],
               (), [], [_sds((ns, d, ni), BF16)], [pl.BlockSpec((None, tmd, ni), lambda j, i, t: (j, i, 0))], (tmd, ni), epi, deps)[0]


def _proj_in_bwd_h(name, du, win_all, dp, deps=()):
    tp = du.shape[0]
    ns, d, ni = win_all.shape
    tm, tn = _tile(tp, TM_BIG, 16), _tile(d, TN, LANE)

    def epi(acc, ex, outs, rows):
        outs[0][rows, :] = ALPHA * ex[0][rows, :] + acc

    mn = pl.BlockSpec((tm, tn), lambda m, n, j: (m, n))
    return _mm(name, (tp // tm, d // tn, ns), NT, (du, win_all),
               [pl.BlockSpec((tm, ni), lambda m, n, j: (m, j)), pl.BlockSpec((None, tn, ni), lambda m, n, j: (j, n, 0))],
               (dp,), [mn], [_sds((tp, d), F32)], [mn], (tm, tn), epi, deps)[0]


def _place():
    return lax.axis_index("x"), lax.axis_index("y"), lax.axis_index("c")


def _hbm(a):
    return pltpu.with_memory_space_constraint(a, pltpu.HBM)


def _token_shape():
    return _sds((8, LANE), F32)


def _gather_peers(x, y, c):
    return [(x, y, 1 - c), (1 - x, y, c), (x, 1 - y, c), (1 - x, 1 - y, c)]


def _into_slab(name, w, dev, dtype):
    r, c = w.shape
    tr = _tile(r, max(16, ELEM_BLOCK_BYTES // (4 * c)), 16 if dtype == BF16 else 8)

    def body(dev_ref, w_ref, o_ref):
        o_ref[...] = w_ref[...].astype(dtype)

    return pl.pallas_call(
        body, name=name,
        grid_spec=pltpu.PrefetchScalarGridSpec(
            num_scalar_prefetch=1, grid=(r // tr,), in_specs=[pl.BlockSpec((tr, c), lambda i, dev_ref: (i, 0))],
            out_specs=pl.BlockSpec((None, tr, c), lambda i, dev_ref: (dev_ref[0], i, 0))),
        out_shape=_sds((N_DEV, r, c), dtype), compiler_params=_params(("parallel",)),
    )(dev, w)


def _gather_start(name, lands, after=()):
    n = len(lands)
    n_peer = N_CHIP
    n_in = n + len(after)

    def body(*refs):
        land_refs = refs[:n]
        send_sems, recv_sems = refs[n_in:n_in + n], refs[n_in + n:n_in + 2 * n]
        token = refs[-1]
        x, y, c = _place()
        me = 4 * x + 2 * y + c
        for i in range(n):
            for k, peer in enumerate(_gather_peers(x, y, c)):
                pltpu.make_async_remote_copy(
                    src_ref=land_refs[i].at[me], dst_ref=land_refs[i].at[me], send_sem=send_sems[i].at[k],
                    recv_sem=recv_sems[i].at[k], device_id=peer, device_id_type=MESH).start()
        token[...] = jnp.zeros_like(token)

    sem = pltpu.SemaphoreType.DMA((n_peer,))
    out = pl.pallas_call(
        body, name=name,
        out_shape=[sem] * (2 * n) + [pltpu.HBM(l.shape, l.dtype) for l in lands] + [_token_shape()],
        in_specs=[HBM] * n + [ANY] * len(after),
        out_specs=[SEM] * (2 * n) + [HBM] * n + [pl.BlockSpec(memory_space=pltpu.VMEM)],
        input_output_aliases={i: 2 * n + i for i in range(n)},
        compiler_params=pltpu.CompilerParams(has_side_effects=EFFECT),
    )(*[_hbm(l) for l in lands], *after)
    per = [(out[i], out[n + i], out[2 * n + i]) for i in range(n)]
    return per, out[-1]


def _gather_wait(name, started, after):
    send_sems, recv_sems, land = started
    after = after if isinstance(after, tuple) else (after,)

    def body(land_ref, send_ref, recv_ref, *rest):
        x, y, c = _place()
        for k, (px, py, pc) in enumerate(_gather_peers(x, y, c)):
            cp = pltpu.make_async_remote_copy(
                src_ref=land_ref.at[4 * x + 2 * y + c], dst_ref=land_ref.at[4 * px + 2 * py + pc], send_sem=send_ref.at[k],
                recv_sem=recv_ref.at[k], device_id=(px, py, pc), device_id_type=MESH)
            cp.wait_send()
            cp.wait_recv()

    return pl.pallas_call(
        body, name=name, out_shape=pltpu.HBM(land.shape, land.dtype),
        in_specs=(HBM, SEM, SEM) + (ANY,) * len(after), out_specs=HBM, input_output_aliases={0: 0},
        compiler_params=pltpu.CompilerParams(has_side_effects=EFFECT),
    )(land, send_sems, recv_sems, *after)


def _gather_finish(name, land, deps=()):
    def body(land_ref, *rest):
        out_ref, send_sems, recv_sems = rest[len(deps):]
        x, y, c = _place()
        copies = []
        for k, (px, py) in enumerate([(1 - x, y), (x, 1 - y), (1 - x, 1 - y)]):
            slab = 4 * px + 2 * py + c
            copies.append(pltpu.make_async_remote_copy(
                src_ref=land_ref.at[slab], dst_ref=out_ref.at[slab], send_sem=send_sems.at[k], recv_sem=recv_sems.at[k],
                device_id=(x, y, 1 - c), device_id_type=MESH))
        for cp in copies:
            cp.start()
        for cp in copies:
            cp.wait()

    return pl.pallas_call(
        body, name=name, out_shape=_sds(land.shape, land.dtype), in_specs=[ANY] + [ANY] * len(deps), out_specs=ANY,
        input_output_aliases={0: 0},
        scratch_shapes=[pltpu.SemaphoreType.DMA((N_CHIP - 1,)), pltpu.SemaphoreType.DMA((N_CHIP - 1,))],
    )(land, *deps)


def _route_sibling(x, y, c):
    return [(2 * j + (1 - c), (x, y, 1 - c)) for j in range(N_CHIP)]


def _route_chips(x, y, c):
    return [(k, (px, py, c)) for k, (px, py) in enumerate([(1 - x, y), (x, 1 - y), (1 - x, 1 - y)])]


def _exchange_start(name, src, route):
    n_copy = len(route(0, 0, 0))

    def body(s_ref, land_ref, send_sems, recv_sems, s_out, land_out, token):
        for k, (slab, peer) in enumerate(route(*_place())):
            pltpu.make_async_remote_copy(
                src_ref=s_ref.at[slab], dst_ref=land_ref.at[k], send_sem=send_sems.at[k], recv_sem=recv_sems.at[k],
                device_id=peer, device_id_type=MESH).start()
        token[...] = jnp.zeros_like(token)

    land = lax.empty((n_copy,) + src.shape[1:], src.dtype)
    sem = pltpu.SemaphoreType.DMA((n_copy,))
    out = pl.pallas_call(
        body, name=name,
        out_shape=[sem, sem, pltpu.HBM(src.shape, src.dtype), pltpu.HBM(land.shape, land.dtype), _token_shape()],
        in_specs=[HBM, HBM], out_specs=[SEM, SEM, HBM, HBM, pl.BlockSpec(memory_space=pltpu.VMEM)],
        input_output_aliases={0: 2, 1: 3}, compiler_params=pltpu.CompilerParams(has_side_effects=EFFECT),
    )(_hbm(src), _hbm(land))
    return out[:4], out[4]


def _exchange_wait(name, started, route, after):
    send_sems, recv_sems, src, land = started

    def body(s_ref, land_ref, send_ref, recv_ref, after_ref, s_out, land_out):
        for k, (slab, peer) in enumerate(route(*_place())):
            cp = pltpu.make_async_remote_copy(
                src_ref=s_ref.at[slab], dst_ref=land_ref.at[k], send_sem=send_ref.at[k], recv_sem=recv_ref.at[k],
                device_id=peer, device_id_type=MESH)
            cp.wait_send()
            cp.wait_recv()

    return pl.pallas_call(
        body, name=name, out_shape=(pltpu.HBM(src.shape, src.dtype), pltpu.HBM(land.shape, land.dtype)),
        in_specs=(HBM, HBM, SEM, SEM, ANY), out_specs=(HBM, HBM), input_output_aliases={0: 0, 1: 1},
        compiler_params=pltpu.CompilerParams(has_side_effects=EFFECT),
    )(src, land, send_sems, recv_sems, after)


def _all_gather(name, land):
    def body(land_ref, out_ref, send_sems, recv_sems):
        x, y, c = _place()
        me, sibling = (x, y, c), (x, y, 1 - c)
        chips = [(1 - x, y), (x, 1 - y), (1 - x, 1 - y)]

        def copy(k, block, to):
            slab = 4 * block[0] + 2 * block[1] + block[2]
            return pltpu.make_async_remote_copy(
                src_ref=land_ref.at[slab], dst_ref=out_ref.at[slab],
                send_sem=send_sems.at[k], recv_sem=recv_sems.at[k], device_id=to, device_id_type=MESH)

        first = [copy(0, me, sibling)] + [copy(1 + j, me, (*chip, c)) for j, chip in enumerate(chips)]
        for cp in first:
            cp.start()
        passed = [copy(4 + j, (*chip, c), sibling) for j, chip in enumerate(chips)]
        for j, chip in enumerate(chips):
            copy(1 + j, (*chip, c), me).wait_recv()
            passed[j].start()
        copy(0, sibling, me).wait_recv()
        for j, chip in enumerate(chips):
            copy(4 + j, (*chip, 1 - c), me).wait_recv()
        for cp in first + passed:
            cp.wait_send()

    return pl.pallas_call(
        body, name=name, out_shape=_sds(land.shape, land.dtype), in_specs=[ANY], out_specs=ANY, input_output_aliases={0: 0},
        scratch_shapes=[pltpu.SemaphoreType.DMA((N_DEV - 1,)), pltpu.SemaphoreType.DMA((N_DEV - 1,))],
    )(land)


def _add_sibling(name, part, recv, place):
    _, r, c = part.shape
    tr = _tile(r, max(16, 2 * ELEM_BLOCK_BYTES // (2 * c)), 16)
    n_out = N_CHIP - 1

    def chip_of(k, x_ref, y_ref):
        px = jnp.where(k == 1, x_ref[0], 1 - x_ref[0])
        py = jnp.where(k == 0, y_ref[0], 1 - y_ref[0])
        return 2 * px + py

    def body(x_ref, y_ref, c_ref, p_ref, r_ref, o_ref):
        o_ref[...] = (p_ref[...].astype(F32) + r_ref[...].astype(F32)).astype(BF16)

    return pl.pallas_call(
        body, name=name,
        grid_spec=pltpu.PrefetchScalarGridSpec(
            num_scalar_prefetch=3, grid=(n_out, r // tr),
            in_specs=[pl.BlockSpec((None, tr, c), lambda k, i, x_ref, y_ref, c_ref: (2 * chip_of(k, x_ref, y_ref) + c_ref[0], i, 0)),
                      pl.BlockSpec((None, tr, c), lambda k, i, x_ref, y_ref, c_ref: (chip_of(k, x_ref, y_ref), i, 0))],
            out_specs=pl.BlockSpec((None, tr, c), lambda k, i, x_ref, y_ref, c_ref: (k, i, 0))),
        out_shape=_sds((n_out, r, c), BF16), compiler_params=_params(("parallel", "parallel")),
    )(*place, part, recv)


def _adamw_math(w, g, m, v):
    m = ADAM_B1 * m + (1.0 - ADAM_B1) * g
    v = ADAM_B2 * v + (1.0 - ADAM_B2) * (g * g)
    m_hat = m / (1.0 - ADAM_B1 ** ADAM_STEP)
    v_hat = v / (1.0 - ADAM_B2 ** ADAM_STEP)
    delta = -ADAM_LR * (m_hat / (jnp.sqrt(v_hat) + ADAM_EPS) + ADAM_WD * w)
    return delta, m, v


def _adamw_big(name, part, from_sibling, from_chips, dev, chip, w, m, v):
    r, c = w.shape
    tr = _tile(r, max(16, ELEM_BLOCK_BYTES // (4 * c)), 16)

    def body(dev_ref, chip_ref, p_ref, s_ref, r_ref, w_ref, m_ref, v_ref, g_out, d_out, m_out, v_out):
        g = p_ref[...].astype(F32) + s_ref[...].astype(F32)
        for k in range(N_CHIP - 1):
            g = g + r_ref[k].astype(F32)
        delta, m_new, v_new = _adamw_math(w_ref[...], g, m_ref[...], v_ref[...])
        g_out[...] = g
        d_out[...] = delta
        m_out[...] = m_new
        v_out[...] = v_new

    blk = pl.BlockSpec((tr, c), lambda i, dev_ref, chip_ref: (i, 0))
    return pl.pallas_call(
        body, name=name,
        grid_spec=pltpu.PrefetchScalarGridSpec(
            num_scalar_prefetch=2, grid=(r // tr,),
            in_specs=[pl.BlockSpec((None, tr, c), lambda i, dev_ref, chip_ref: (dev_ref[0], i, 0)),
                      pl.BlockSpec((None, tr, c), lambda i, dev_ref, chip_ref: (chip_ref[0], i, 0)),
                      pl.BlockSpec((N_CHIP - 1, tr, c), lambda i, dev_ref, chip_ref: (0, i, 0)), blk, blk, blk],
            out_specs=[blk, blk, blk, blk]),
        out_shape=[_sds((r, c), F32)] * 4, compiler_params=_params(("parallel",)),
    )(dev, chip, part, from_sibling, from_chips, w, m, v)


def _sum_parts(name, parts):
    n, r, c = parts.shape
    tr = _tile(r, max(8, ELEM_BLOCK_BYTES // (4 * c)), 8)

    def body(p_ref, o_ref):
        acc = p_ref[0]
        for k in range(1, n):
            acc = acc + p_ref[k]
        o_ref[...] = acc

    return pl.pallas_call(body, name=name, grid=(r // tr,), in_specs=[pl.BlockSpec((n, tr, c), lambda i: (0, i, 0))],
                          out_specs=pl.BlockSpec((tr, c), lambda i: (i, 0)), out_shape=_sds((r, c), F32),
                          compiler_params=_params(("parallel",)))(parts)


def _adamw_small(name, ws, gs, ms, vs):
    n = len(ws)

    def body(*refs):
        ins, outs = refs[:4 * n], refs[4 * n:]
        for i in range(n):
            delta, m_new, v_new = _adamw_math(ins[i][...], ins[n + i][...], ins[2 * n + i][...], ins[3 * n + i][...])
            outs[i][...] = delta
            outs[n + i][...] = m_new
            outs[2 * n + i][...] = v_new

    shapes = [_sds(w.shape, F32) for w in ws]
    return pl.pallas_call(body, name=name, out_shape=shapes * 3)(*ws, *gs, *ms, *vs)


def _pad_rows(a, rows):
    return jnp.pad(a, ((0, rows - a.shape[0]), (0, 0)))


def kernel(x, meta_tokens, ffn1_w_gu, ffn1_w_down, ln1_g, ln1_b, w_in, conv_w, pool_w, pool_scale, w_out, ln2_g, ln2_b, ffn2_w_gu, ffn2_w_down, ln3_g, ln3_b, loss_target, m_meta_tokens, m_ffn1_w_gu, m_ffn1_w_down, m_ln1_g, m_ln1_b, m_w_in, m_conv_w, m_pool_w, m_pool_scale, m_w_out, m_ln2_g, m_ln2_b, m_ffn2_w_gu, m_ffn2_w_down, m_ln3_g, m_ln3_b, v_meta_tokens, v_ffn1_w_gu, v_ffn1_w_down, v_ln1_g, v_ln1_b, v_w_in, v_conv_w, v_pool_w, v_pool_scale, v_w_out, v_ln2_g, v_ln2_b, v_ffn2_w_gu, v_ffn2_w_down, v_ln3_g, v_ln3_b):
    names = ["meta_tokens", "ffn1_w_gu", "ffn1_w_down", "ln1_g", "ln1_b", "w_in", "conv_w", "pool_w", "pool_scale", "w_out",
             "ln2_g", "ln2_b", "ffn2_w_gu", "ffn2_w_down", "ln3_g", "ln3_b"]
    w_of = dict(zip(names, [meta_tokens, ffn1_w_gu, ffn1_w_down, ln1_g, ln1_b, w_in, conv_w, pool_w, pool_scale, w_out,
                            ln2_g, ln2_b, ffn2_w_gu, ffn2_w_down, ln3_g, ln3_b]))
    m_of = dict(zip(names, [m_meta_tokens, m_ffn1_w_gu, m_ffn1_w_down, m_ln1_g, m_ln1_b, m_w_in, m_conv_w, m_pool_w, m_pool_scale,
                            m_w_out, m_ln2_g, m_ln2_b, m_ffn2_w_gu, m_ffn2_w_down, m_ln3_g, m_ln3_b]))
    v_of = dict(zip(names, [v_meta_tokens, v_ffn1_w_gu, v_ffn1_w_down, v_ln1_g, v_ln1_b, v_w_in, v_conv_w, v_pool_w, v_pool_scale,
                            v_w_out, v_ln2_g, v_ln2_b, v_ffn2_w_gu, v_ffn2_w_down, v_ln3_g, v_ln3_b]))

    n_seq, seq, d = x.shape
    seq_len = seq + N_META
    n_rows = n_seq * seq_len
    tp = -(-n_rows // ROW_ALIGN) * ROW_ALIGN
    c_conv = conv_w.shape[2] * N_DEV
    p_pool = pool_scale.shape[1]
    pg = pool_w.shape[3]
    assert c_conv == p_pool and p_pool == N_POOL_GROUPS * pg and POOL_WINDOWS == tuple(2 << g for g in range(N_POOL_GROUPS))
    assert (N_POOL_GROUPS * pg * pg) % d == 0 and pg % LANE == 0

    xi, yi, ci = _place()
    dev_index = 4 * xi + 2 * yi + ci
    dev = jnp.reshape(dev_index, (1,)).astype(jnp.int32)
    chip = jnp.reshape(2 * xi + yi, (1,)).astype(jnp.int32)
    place = tuple(jnp.reshape(a, (1,)).astype(jnp.int32) for a in (xi, yi, ci))

    big = ["ffn1_w_gu", "ffn1_w_down", "w_in", "w_out", "ffn2_w_gu", "ffn2_w_down"]
    wcol = d // N_DEV
    conv_rows = 8
    small_local = jnp.concatenate([
        meta_tokens,
        pool_w[0].reshape(N_POOL_GROUPS * (pg // N_DEV), pg),
        jnp.pad(conv_w[0], ((0, conv_rows - CONV_K), (0, wcol - conv_w.shape[2]))),
    ], axis=0)
    lands = {n: _into_slab(f"slab_{n}", w_of[n][0], dev, BF16) for n in big}
    lands["small"] = _into_slab("slab_small", small_local, dev, F32)
    started = {}

    def start(tag, which, after=()):
        per, token = _gather_start(f"ag_start_{tag}", [lands[n] for n in which], after)
        started.update(zip(which, per))
        return token

    def gathered(n, after, then_start=()):
        land = _gather_wait(f"ag_wait_{n}", started[n], after)
        deps = (start(f"after_{n}", then_start, (land,)),) if then_start else ()
        return _gather_finish(f"ag_finish_{n}", land, deps)

    gather_token = start("first", ["small", "ffn1_w_gu"])
    small_all = gathered("small", gather_token)
    r0, r1 = N_META, N_META + N_POOL_GROUPS * (pg // N_DEV)
    meta_full = jnp.transpose(small_all[:, :r0], (1, 0, 2)).reshape(N_META, d)
    pool_w_full = jnp.transpose(small_all[:, r0:r1].reshape(N_DEV, N_POOL_GROUPS, pg // N_DEV, pg), (1, 0, 2, 3)).reshape(N_POOL_GROUPS, pg, pg)
    conv_w_full = jnp.transpose(small_all[:, r1:r1 + CONV_K, :conv_w.shape[2]], (1, 0, 2)).reshape(CONV_K, c_conv)
    pool_w_b = pool_w_full.astype(BF16)

    h0 = jnp.concatenate([jnp.broadcast_to(meta_full[None], (n_seq, N_META, d)), x], axis=1).reshape(n_rows, d)
    h0 = _pad_rows(h0, tp)
    h0_b = h0.astype(BF16)
    tgt = _pad_rows(jnp.pad(loss_target, ((0, 0), (N_META, 0), (0, 0))).reshape(n_rows, d), tp)

    early = (h0_b, tgt) + tuple(lands[n] for n in big[1:])
    early += tuple(a[n][0] for n in ("ffn1_w_gu", "ffn2_w_gu") for a in (m_of, v_of))
    wgu1 = gathered("ffn1_w_gu", early, ["ffn1_w_down", "w_in"])
    gu1, act1 = _ffn_gu("ffn1_gu", h0_b, wgu1)
    wd1 = gathered("ffn1_w_down", act1, ["w_out", "ffn2_w_gu"]).reshape(N_CHIP, -1, d)
    pre1 = _ffn_down("ffn1_down", act1, wd1, h0)
    win_all = gathered("w_in", pre1)
    h1, h1_b = _ln_fwd("ln1", pre1, ln1_g, ln1_b)

    u = _proj_in("mix_in", h1_b, win_all)
    wout_all = gathered("w_out", u)
    y_conv = _conv_fwd("mix_conv", u, conv_w_full, n_rows, seq_len)
    dpool = _pool_fwd("mix_pool", u, 3 * c_conv, p_pool, pg, n_rows, seq_len)
    ypre, y_pool = _pool_mix("mix_pool_w", dpool, pool_w_b, pool_scale)
    y_mix = jnp.concatenate([y_conv, y_pool], axis=1)
    pre2 = _proj_out("mix_out", y_mix, wout_all, h1)
    wgu2 = gathered("ffn2_w_gu", pre2, ["ffn2_w_down"])
    h2, h2_b = _ln_fwd("ln2", pre2, ln2_g, ln2_b)

    gu2, act2 = _ffn_gu("ffn2_gu", h2_b, wgu2)
    wd2 = gathered("ffn2_w_down", act2).reshape(N_CHIP, -1, d)
    pre3 = _ffn_down("ffn2_down", act2, wd2, h2)

    dpre3, dpre3_b, d_ln3_g, d_ln3_b, sq = _ln_loss_bwd("ln3_loss", pre3, tgt, ln3_g, ln3_b, n_rows, seq_len)
    loss = lax.psum(0.5 * jnp.sum(sq) / d, ("x", "y", "c"))
    in_sibling, reducing = [], {}

    def exchange(after, new=None):
        tokens = []
        while in_sibling:
            n, started = in_sibling.pop(0)
            part, from_sibling = _exchange_wait(f"rs_sibling_wait_{n}", started, _route_sibling, after)
            summed = _add_sibling(f"rs_add_{n}", part, from_sibling, place)
            started, token = _exchange_start(f"rs_chips_start_{n}", summed, _route_chips)
            reducing[n] = (part, from_sibling, started)
            tokens.append(token)
        if new is not None:
            started, token = _exchange_start(f"rs_sibling_start_{new[0]}", new[1], _route_sibling)
            in_sibling.append((new[0], started))
            tokens.append(token)
        return tuple(tokens)

    dgu2 = _ffn_bwd_dgu("ffn2_bwd_dgu", dpre3_b, wd2, gu2)
    g_wd2 = _ffn_bwd_wd("ffn2_bwd_wd", act2, dpre3_b)
    tokens = exchange(g_wd2, ("ffn2_w_down", g_wd2.reshape(N_DEV, -1, d)))
    g_wgu2 = _ffn_bwd_wgu("ffn2_bwd_wgu", h2_b, dgu2, tokens)
    tokens = exchange(g_wgu2, ("ffn2_w_gu", g_wgu2))
    dh2 = _ffn_bwd_dh("ffn2_bwd_dh", dgu2, wgu2, dpre3, tokens)
    tokens = exchange(dh2)
    dpre2, dpre2_b, d_ln2_g, d_ln2_b = _ln_bwd("ln2_bwd", pre2, dh2, ln2_g)

    dy_mix = _proj_out_bwd_y("mix_out_bwd_y", dpre2_b, wout_all, tokens)
    g_wout = _proj_out_bwd_w("mix_out_bwd_w", y_mix, dpre2_b, N_DEV)
    tokens = exchange(g_wout, ("w_out", g_wout))
    dyps, d_pool_scale = _pool_scale_bwd("mix_pool_scale_bwd", dy_mix, 1, ypre, pool_scale)
    dd = _pool_mix_bwd_in("mix_pool_w_bwd_in", dyps, pool_w_b)
    d_pool_w = _pool_mix_bwd_w("mix_pool_w_bwd_w", dpool, dyps, pg)
    du_pool = _pool_bwd("mix_pool_bwd", dd, pg, n_rows, seq_len)
    du_b, du_c, du_x, d_conv_w = _conv_bwd("mix_conv_bwd", u, conv_w_full, dy_mix, n_rows, seq_len)
    du = jnp.concatenate([du_b, du_c, du_x, du_pool], axis=1)
    g_win = _proj_in_bwd_w("mix_in_bwd_w", h1_b, du, N_DEV, tokens)
    tokens = exchange(g_win, ("w_in", g_win))
    dh1 = _proj_in_bwd_h("mix_in_bwd_h", du, win_all, dpre2, tokens)
    tokens = exchange(dh1)
    dpre1, dpre1_b, d_ln1_g, d_ln1_b = _ln_bwd("ln1_bwd", pre1, dh1, ln1_g)

    def widen(a):
        return jnp.pad(a, ((0, 0), (0, d - a.shape[1])))

    small_part = jnp.concatenate([
        d_ln1_g, d_ln1_b, d_ln2_g, d_ln2_b, d_ln3_g, d_ln3_b, widen(d_pool_scale), widen(d_conv_w), d_pool_w.reshape(-1, d)], axis=0)
    n_small_rows = small_part.shape[0]
    small_part = _pad_rows(small_part, -(-n_small_rows // 8) * 8)
    small_started, token = _gather_start("ag_start_small_grads", [_into_slab("slab_small_grads", small_part, dev, F32)])

    dgu1 = _ffn_bwd_dgu("ffn1_bwd_dgu", dpre1_b, wd1, gu1, tokens + (token,))
    g_wgu1 = _ffn_bwd_wgu("ffn1_bwd_wgu", h0_b, dgu1)
    tokens = exchange(g_wgu1, ("ffn1_w_gu", g_wgu1))
    g_wd1 = _ffn_bwd_wd("ffn1_bwd_wd", act1, dpre1_b, tokens)
    tokens = exchange(g_wd1, ("ffn1_w_down", g_wd1.reshape(N_DEV, -1, d)))
    dh0 = _ffn_bwd_dh("ffn1_bwd_dh", dgu1, wgu1, dpre1, tokens)
    tokens = exchange(dh0)

    dh0_seq = dh0[:n_rows].reshape(n_seq, seq_len, d)
    grad_x = dh0_seq[:, N_META:]
    d_meta = jnp.sum(dh0_seq[:, :N_META], axis=0)

    grads, deltas, new_m, new_v = {}, {}, {}, {}
    after = tokens[0]
    for n in ["ffn2_w_down", "ffn2_w_gu", "w_out", "w_in", "ffn1_w_gu", "ffn1_w_down"]:
        part, from_sibling, started = reducing[n]
        _, from_chips = _exchange_wait(f"rs_chips_wait_{n}", started, _route_chips, after)
        g, dl, mm, vv = _adamw_big(f"adamw_{n}", part, from_sibling, from_chips, dev, chip, w_of[n][0], m_of[n][0], v_of[n][0])
        grads[n], deltas[n], new_m[n], new_v[n] = g[None], dl[None], mm[None], vv[None]
        after = g

    small_sum = _sum_parts("small_sum", _gather_finish("ag_finish_small_grads", _gather_wait("ag_wait_small_grads", small_started[0], dh0)))
    meta_sum = _sum_parts("meta_sum", _all_gather("ag_meta_grads", _into_slab("slab_meta_grads", d_meta, dev, F32)))
    o = 7 + CONV_K
    g_small = {
        "ln1_g": small_sum[0:1], "ln1_b": small_sum[1:2], "ln2_g": small_sum[2:3], "ln2_b": small_sum[3:4],
        "ln3_g": small_sum[4:5], "ln3_b": small_sum[5:6], "pool_scale": small_sum[6:7, :p_pool],
        "conv_w": lax.dynamic_slice_in_dim(small_sum[7:o, :c_conv], dev_index * (c_conv // N_DEV), c_conv // N_DEV, axis=1)[None],
        "meta_tokens": lax.dynamic_slice_in_dim(meta_sum, dev_index * wcol, wcol, axis=1),
        "pool_w": lax.dynamic_slice_in_dim(small_sum[o:n_small_rows].reshape(N_POOL_GROUPS, pg, pg),
                                           dev_index * (pg // N_DEV), pg // N_DEV, axis=1)[None],
    }
    small = ["meta_tokens", "ln1_g", "ln1_b", "conv_w", "pool_w", "pool_scale", "ln2_g", "ln2_b", "ln3_g", "ln3_b"]

    def flat(a):
        return a.reshape(-1, a.shape[-1])

    outs = _adamw_small("adamw_small", [flat(w_of[n]) for n in small], [flat(g_small[n]) for n in small],
                        [flat(m_of[n]) for n in small], [flat(v_of[n]) for n in small])
    ns = len(small)
    for i, n in enumerate(small):
        shape = w_of[n].shape
        grads[n] = g_small[n].reshape(shape)
        deltas[n], new_m[n], new_v[n] = outs[i].reshape(shape), outs[ns + i].reshape(shape), outs[2 * ns + i].reshape(shape)

    return (loss, grad_x, *[grads[n] for n in names], *[deltas[n] for n in names],
            *[new_m[n] for n in names], *[new_v[n] for n in names])
```

```python
import functools

import jax
import jax.numpy as jnp
from jax import lax
from jax.experimental import pallas as pl
from jax.experimental.pallas import tpu as pltpu

N_DEV = 8
N_CHIP = 4
N_META = 16
CONV_K = 3
POOL_WINDOWS = (2, 4, 8, 16)
N_POOL_GROUPS = len(POOL_WINDOWS)
LN_EPS = 1e-5
DEPTH = 1
ALPHA = (2.0 * DEPTH) ** 0.25
ADAM_LR = 0.001
ADAM_B1 = 0.9
ADAM_B2 = 0.999
ADAM_EPS = 1e-08
ADAM_WD = 0.01
ADAM_STEP = 10

V7X_VMEM_BYTES = 64 * 1024 * 1024
VMEM_LIMIT = V7X_VMEM_BYTES - 6 * 1024 * 1024
LANE = 128
ROW_ALIGN = 3 * LANE
TM_BIG = 1408
TM_WIDE = 704
TM_GU = 528
TK = 512
TK_TOKENS = 1408
TN = 1024
TR_LN = 128
ELEM_BLOCK_BYTES = 1 << 20
TC_MIX = LANE
EPILOGUE_ROWS = 64

NN = (((1,), (0,)), ((), ()))
NT = (((1,), (1,)), ((), ()))
TN_DIMS = (((0,), (0,)), ((), ()))
MESH = pl.DeviceIdType.MESH
BF16 = jnp.bfloat16
F32 = jnp.float32
ANY = pl.BlockSpec(memory_space=pl.ANY)
HBM = pl.BlockSpec(memory_space=pltpu.HBM)
SEM = pl.BlockSpec(memory_space=pltpu.SEMAPHORE)
EFFECT = pltpu.SideEffectType.DATAFLOW_SIDE_EFFECTING


def _tile(n, target, mult):
    best = None
    for t in range(mult, min(n, target) + 1, mult):
        if n % t == 0:
            best = t
    return n if best is None else best


def _params(sem):
    return pltpu.CompilerParams(dimension_semantics=sem, vmem_limit_bytes=VMEM_LIMIT)


def _sds(shape, dtype):
    return jax.ShapeDtypeStruct(shape, dtype)


def _row_chunks(n_rows, fn):
    ch = _tile(n_rows, EPILOGUE_ROWS, 16)

    def step(i, carry):
        fn(pl.ds(pl.multiple_of(i * ch, ch), ch))
        return carry

    lax.fori_loop(0, n_rows // ch, step, 0)


def _mm(name, grid, dims, ab, ab_specs, extras, extra_specs, out_shape, out_specs, acc_shape, epilogue, deps=()):
    nk = grid[-1]
    n_extra = len(extras)
    n_in = 2 + n_extra + len(deps)
    n_out = len(out_shape)
    kax = len(grid) - 1

    def body(*refs):
        a_ref, b_ref = refs[0], refs[1]
        ex = refs[2:2 + n_extra]
        outs = refs[n_in:n_in + n_out]
        if nk == 1:
            epilogue(lax.dot_general(a_ref[...], b_ref[...], dims, preferred_element_type=F32), ex, outs, slice(None))
            return
        acc = refs[-1]
        k = pl.program_id(kax)

        @pl.when(k == 0)
        def _():
            acc[...] = jnp.zeros_like(acc)

        acc[...] += lax.dot_general(a_ref[...], b_ref[...], dims, preferred_element_type=F32)

        @pl.when(k == nk - 1)
        def _():
            _row_chunks(acc_shape[0], lambda rows: epilogue(acc[rows, :], ex, outs, rows))

    scratch = [] if nk == 1 else [pltpu.VMEM(acc_shape, F32)]
    sem = ("parallel",) * kax + ("arbitrary",)
    return pl.pallas_call(
        body, name=name, grid=grid, in_specs=list(ab_specs) + list(extra_specs) + [ANY] * len(deps), out_specs=list(out_specs),
        out_shape=list(out_shape), scratch_shapes=scratch, compiler_params=_params(sem),
    )(*ab, *extras, *deps)


def _silu_parts(g):
    s = 1.0 / (1.0 + jnp.exp(-g))
    return s, g * s


def _ffn_gu(name, h_b, wgu_all):
    tp, d = h_b.shape
    ns, _, ng = wgu_all.shape
    half = ns // 2
    tm, tk = _tile(tp, TM_GU, 16), _tile(d, 2 * TK, LANE)
    grid = (tp // tm, half, d // tk)
    nk = grid[-1]

    def body(h_ref, wg_ref, wu_ref, gu_ref, act_ref, acc_g, acc_u):
        k = pl.program_id(2)

        @pl.when(k == 0)
        def _():
            acc_g[...] = jnp.zeros_like(acc_g)
            acc_u[...] = jnp.zeros_like(acc_u)

        acc_g[...] += jnp.dot(h_ref[...], wg_ref[...], preferred_element_type=F32)
        acc_u[...] += jnp.dot(h_ref[...], wu_ref[...], preferred_element_type=F32)

        @pl.when(k == nk - 1)
        def _():
            def finish(rows):
                g = acc_g[rows, :]
                u = acc_u[rows, :]
                _, silu = _silu_parts(g)
                gu_ref[0, rows, :] = g.astype(BF16)
                gu_ref[1, rows, :] = u.astype(BF16)
                act_ref[rows, :] = (silu * u).astype(BF16)

            _row_chunks(tm, finish)

    return pl.pallas_call(
        body, name=name, grid=grid,
        in_specs=[pl.BlockSpec((tm, tk), lambda m, s, k: (m, k)),
                  pl.BlockSpec((None, tk, ng), lambda m, s, k: (s, k, 0)),
                  pl.BlockSpec((None, tk, ng), lambda m, s, k: (s + half, k, 0))],
        out_specs=[pl.BlockSpec((None, 2, tm, ng), lambda m, s, k: (s, 0, m, 0)),
                   pl.BlockSpec((None, tm, ng), lambda m, s, k: (s, m, 0))],
        out_shape=[_sds((half, 2, tp, ng), BF16), _sds((half, tp, ng), BF16)],
        scratch_shapes=[pltpu.VMEM((tm, ng), F32), pltpu.VMEM((tm, ng), F32)],
        compiler_params=_params(("parallel", "parallel", "arbitrary")),
    )(h_b, wgu_all, wgu_all)


def _ffn_down(name, act, wd4, h):
    ns, tp, ng = act.shape
    d = wd4.shape[2]
    tm, tn = _tile(tp, TM_WIDE, 16), _tile(d, TN, LANE)

    def epi(acc, ex, outs, rows):
        outs[0][rows, :] = ALPHA * ex[0][rows, :] + 0.5 * acc

    return _mm(name, (tp // tm, d // tn, ns), NN, (act, wd4),
               [pl.BlockSpec((None, tm, ng), lambda m, n, s: (s, m, 0)),
                pl.BlockSpec((None, ng, tn), lambda m, n, s: (s, 0, n))],
               (h,), [pl.BlockSpec((tm, tn), lambda m, n, s: (m, n))],
               [_sds((tp, d), F32)], [pl.BlockSpec((tm, tn), lambda m, n, s: (m, n))], (tm, tn), epi)[0]


def _ffn_bwd_dgu(name, dp_b, wd4, gu, deps=()):
    tp, d = dp_b.shape
    ns, ng, _ = wd4.shape
    tm, tk = _tile(tp, TM_WIDE, 16), _tile(d, 2 * TK, LANE)

    def epi(acc, ex, outs, rows):
        g = ex[0][0, rows, :].astype(F32)
        u = ex[0][1, rows, :].astype(F32)
        da = 0.5 * acc
        s, silu = _silu_parts(g)
        outs[0][0, rows, :] = (da * u * (s + silu * (1.0 - s))).astype(BF16)
        outs[0][1, rows, :] = (da * silu).astype(BF16)

    return _mm(name, (tp // tm, ns, d // tk), NT, (dp_b, wd4),
               [pl.BlockSpec((tm, tk), lambda m, s, k: (m, k)),
                pl.BlockSpec((None, ng, tk), lambda m, s, k: (s, 0, k))],
               (gu,), [pl.BlockSpec((None, 2, tm, ng), lambda m, s, k: (s, 0, m, 0))],
               [_sds((ns, 2, tp, ng), BF16)], [pl.BlockSpec((None, 2, tm, ng), lambda m, s, k: (s, 0, m, 0))],
               (tm, ng), epi, deps)[0]


def _ffn_bwd_wd(name, act, dp_b, deps=()):
    ns, tp, ng = act.shape
    d = dp_b.shape[1]
    tkt, tn = _tile(tp, TK_TOKENS, LANE), _tile(d, TN, LANE)

    def epi(acc, ex, outs, rows):
        outs[0][rows, :] = (0.5 * acc).astype(BF16)

    return _mm(name, (ns, d // tn, tp // tkt), TN_DIMS, (act, dp_b),
               [pl.BlockSpec((None, tkt, ng), lambda s, n, t: (s, t, 0)),
                pl.BlockSpec((tkt, tn), lambda s, n, t: (t, n))],
               (), [], [_sds((ns, ng, d), BF16)], [pl.BlockSpec((None, ng, tn), lambda s, n, t: (s, 0, n))],
               (ng, tn), epi, deps)[0]


def _ffn_bwd_wgu(name, h_b, dgu, deps=()):
    tp, d = h_b.shape
    ns, _, _, ng = dgu.shape
    tkt, tmd = _tile(tp, TK_TOKENS, LANE), _tile(d, TN, LANE)

    def epi(acc, ex, outs, rows):
        outs[0][rows, :] = acc.astype(BF16)

    return _mm(name, (ns, 2, d // tmd, tp // tkt), TN_DIMS, (h_b, dgu),
               [pl.BlockSpec((tkt, tmd), lambda s, j, i, t: (t, i)),
                pl.BlockSpec((None, None, tkt, ng), lambda s, j, i, t: (s, j, t, 0))],
               (), [], [_sds((2 * ns, d, ng), BF16)],
               [pl.BlockSpec((None, tmd, ng), lambda s, j, i, t: (j * ns + s, i, 0))], (tmd, ng), epi, deps)[0]


def _ffn_bwd_dh(name, dgu, wgu_all, dp, deps=()):
    ns, _, tp, ng = dgu.shape
    d = wgu_all.shape[1]
    tm, tn = _tile(tp, TM_WIDE, 16), _tile(d, TN, LANE)

    def epi(acc, ex, outs, rows):
        outs[0][rows, :] = ALPHA * ex[0][rows, :] + acc

    return _mm(name, (tp // tm, d // tn, 2 * ns), NT, (dgu, wgu_all),
               [pl.BlockSpec((None, None, tm, ng), lambda m, n, j: (j % ns, j // ns, m, 0)),
                pl.BlockSpec((None, tn, ng), lambda m, n, j: (j, n, 0))],
               (dp,), [pl.BlockSpec((tm, tn), lambda m, n, j: (m, n))],
               [_sds((tp, d), F32)], [pl.BlockSpec((tm, tn), lambda m, n, j: (m, n))], (tm, tn), epi, deps)[0]


def _ln_stats(x):
    mu = jnp.mean(x, axis=-1, keepdims=True)
    xc = x - mu
    var = jnp.mean(xc * xc, axis=-1, keepdims=True)
    rstd = lax.rsqrt(var + LN_EPS)
    return xc * rstd, rstd


def _ln_bwd_rows(dy, xhat, rstd, g):
    dxh = dy * g
    m1 = jnp.mean(dxh, axis=-1, keepdims=True)
    m2 = jnp.mean(dxh * xhat, axis=-1, keepdims=True)
    return rstd * (dxh - m1 - xhat * m2)


def _ln_fwd(name, pre, g, b):
    tp, d = pre.shape
    tr = _tile(tp, TR_LN, 16)

    def body(x_ref, g_ref, b_ref, y_ref, yb_ref):
        xhat, _ = _ln_stats(x_ref[...])
        y = xhat * g_ref[...] + b_ref[...]
        y_ref[...] = y
        yb_ref[...] = y.astype(BF16)

    row = pl.BlockSpec((tr, d), lambda i: (i, 0))
    vec = pl.BlockSpec((1, d), lambda i: (0, 0))
    return pl.pallas_call(body, name=name, grid=(tp // tr,), in_specs=[row, vec, vec], out_specs=[row, row],
                          out_shape=[_sds((tp, d), F32), _sds((tp, d), BF16)],
                          compiler_params=_params(("parallel",)))(pre, g, b)


def _accumulate(i, ref, val):
    @pl.when(i == 0)
    def _():
        ref[...] = val

    @pl.when(i > 0)
    def _():
        ref[...] += val


def _ln_bwd(name, pre, dy, g):
    tp, d = pre.shape
    tr = _tile(tp, TR_LN, 16)

    def body(x_ref, dy_ref, g_ref, dx_ref, dxb_ref, dg_ref, db_ref):
        i = pl.program_id(0)
        xhat, rstd = _ln_stats(x_ref[...])
        dy = dy_ref[...]
        dx = _ln_bwd_rows(dy, xhat, rstd, g_ref[...])
        dx_ref[...] = dx
        dxb_ref[...] = dx.astype(BF16)
        _accumulate(i, dg_ref, jnp.sum(dy * xhat, axis=0, keepdims=True))
        _accumulate(i, db_ref, jnp.sum(dy, axis=0, keepdims=True))

    row = pl.BlockSpec((tr, d), lambda i: (i, 0))
    vec = pl.BlockSpec((1, d), lambda i: (0, 0))
    return pl.pallas_call(body, name=name, grid=(tp // tr,), in_specs=[row, row, vec], out_specs=[row, row, vec, vec],
                          out_shape=[_sds((tp, d), F32), _sds((tp, d), BF16), _sds((1, d), F32), _sds((1, d), F32)],
                          compiler_params=_params(("arbitrary",)))(pre, dy, g)


def _ln_loss_bwd(name, pre, tgt, g, b, n_rows, seq_len):
    tp, d = pre.shape
    tr = _tile(tp, TR_LN, 16)
    n_seq = n_rows // seq_len

    def body(x_ref, t_ref, g_ref, b_ref, dx_ref, dxb_ref, dg_ref, db_ref, sq_ref):
        i = pl.program_id(0)
        xhat, rstd = _ln_stats(x_ref[...])
        gain = g_ref[...]
        y = xhat * gain + b_ref[...]
        r = i * tr + lax.broadcasted_iota(jnp.int32, (tr, 1), 0)
        pos = r
        for s in range(1, n_seq):
            pos = jnp.where(r >= s * seq_len, r - s * seq_len, pos)
        live = jnp.logical_and(r < n_rows, pos >= N_META)
        err = jnp.where(live, y - t_ref[...], 0.0)
        dy = err * (1.0 / d)
        dx = _ln_bwd_rows(dy, xhat, rstd, gain)
        dx_ref[...] = dx
        dxb_ref[...] = dx.astype(BF16)
        _accumulate(i, dg_ref, jnp.sum(dy * xhat, axis=0, keepdims=True))
        _accumulate(i, db_ref, jnp.sum(dy, axis=0, keepdims=True))
        _accumulate(i, sq_ref, jnp.sum(err * err, axis=0, keepdims=True))

    row = pl.BlockSpec((tr, d), lambda i: (i, 0))
    vec = pl.BlockSpec((1, d), lambda i: (0, 0))
    return pl.pallas_call(
        body, name=name, grid=(tp // tr,), in_specs=[row, row, vec, vec], out_specs=[row, row, vec, vec, vec],
        out_shape=[_sds((tp, d), F32), _sds((tp, d), BF16), _sds((1, d), F32), _sds((1, d), F32), _sds((1, d), F32)],
        compiler_params=_params(("arbitrary",)))(pre, tgt, g, b)


def _proj_in(name, h_b, win_all):
    tp, d = h_b.shape
    ns, _, ni = win_all.shape
    tm, tk = _tile(tp, TM_BIG, 16), _tile(d, 4 * TK, LANE)

    def epi(acc, ex, outs, rows):
        outs[0][rows, :] = acc

    return _mm(name, (tp // tm, ns, d // tk), NN, (h_b, win_all),
               [pl.BlockSpec((tm, tk), lambda m, j, k: (m, k)),
                pl.BlockSpec((None, tk, ni), lambda m, j, k: (j, k, 0))],
               (), [], [_sds((tp, ns * ni), F32)], [pl.BlockSpec((tm, ni), lambda m, j, k: (m, j))], (tm, ni), epi)[0]


def _positions(tp, n_rows, seq_len):
    r = lax.broadcasted_iota(jnp.int32, (tp, 1), 0)
    pos = r
    for s in range(1, n_rows // seq_len):
        pos = jnp.where(r >= s * seq_len, r - s * seq_len, pos)
    return pos


def _shift_down(x, s, pos):
    return jnp.where(pos >= s, pltpu.roll(x, s, 0), 0.0)


def _shift_up(x, s, pos, seq_len):
    return jnp.where(pos + s < seq_len, pltpu.roll(x, x.shape[0] - s, 0), 0.0)


def _conv_fwd(name, u, conv_w, n_rows, seq_len):
    tp = u.shape[0]
    c = conv_w.shape[1]
    tc = _tile(c, TC_MIX, LANE)
    nb = c // tc

    def body(gb_ref, gc_ref, xi_ref, w_ref, y_ref):
        pos = _positions(tp, n_rows, seq_len)
        v = gc_ref[...] * xi_ref[...]
        w = w_ref[...]
        y = _shift_down(v, 2, pos) * w[0:1]
        y = y + _shift_down(v, 1, pos) * w[1:2]
        y = y + v * w[2:3]
        y_ref[...] = (gb_ref[...] * y).astype(BF16)

    col = lambda off: pl.BlockSpec((tp, tc), lambda i: (0, off + i))
    return pl.pallas_call(body, name=name, grid=(nb,), in_specs=[col(0), col(nb), col(2 * nb), pl.BlockSpec((CONV_K, tc), lambda i: (0, i))],
                          out_specs=pl.BlockSpec((tp, tc), lambda i: (0, i)), out_shape=_sds((tp, c), BF16),
                          compiler_params=_params(("parallel",)))(u, u, u, conv_w)


def _conv_bwd(name, u, conv_w, dy, n_rows, seq_len):
    tp = u.shape[0]
    c = conv_w.shape[1]
    tc = _tile(c, TC_MIX, LANE)
    nb = c // tc

    def body(gb_ref, gc_ref, xi_ref, w_ref, dy_ref, dgb_ref, dgc_ref, dxi_ref, dw_ref):
        pos = _positions(tp, n_rows, seq_len)
        gc, xi = gc_ref[...], xi_ref[...]
        v = gc * xi
        w = w_ref[...]
        v2, v1 = _shift_down(v, 2, pos), _shift_down(v, 1, pos)
        conv = v2 * w[0:1]
        conv = conv + v1 * w[1:2]
        conv = conv + v * w[2:3]
        dyc = dy_ref[...]
        dgb_ref[...] = (dyc * conv).astype(BF16)
        dconv = dyc * gb_ref[...]
        dv = dconv * w[2:3] + _shift_up(dconv, 1, pos, seq_len) * w[1:2] + _shift_up(dconv, 2, pos, seq_len) * w[0:1]
        dgc_ref[...] = (dv * xi).astype(BF16)
        dxi_ref[...] = (dv * gc).astype(BF16)
        dw_ref[0:1, :] = jnp.sum(dconv * v2, axis=0, keepdims=True)
        dw_ref[1:2, :] = jnp.sum(dconv * v1, axis=0, keepdims=True)
        dw_ref[2:3, :] = jnp.sum(dconv * v, axis=0, keepdims=True)

    col = lambda off: pl.BlockSpec((tp, tc), lambda i: (0, off + i))
    wspec = pl.BlockSpec((CONV_K, tc), lambda i: (0, i))
    return pl.pallas_call(body, name=name, grid=(nb,), in_specs=[col(0), col(nb), col(2 * nb), wspec, col(0)],
                          out_specs=[col(0), col(0), col(0), wspec],
                          out_shape=[_sds((tp, c), BF16)] * 3 + [_sds((CONV_K, c), F32)],
                          compiler_params=_params(("parallel",)))(u, u, u, conv_w, dy)


def _window_select(group, parts):
    out = parts[-1]
    for gi in range(len(parts) - 2, -1, -1):
        out = jnp.where(group == gi, parts[gi], out)
    return out


def _pool_fwd(name, u, col0, p, pg, n_rows, seq_len):
    tp = u.shape[0]
    tc = _tile(pg, TC_MIX, LANE)
    per_group = pg // tc

    def body(z_ref, d_ref):
        group = pl.program_id(0) // per_group
        pos = _positions(tp, n_rows, seq_len)
        z = z_ref[...]
        sums, s, w = [], z, 1
        for _ in POOL_WINDOWS:
            s = s + _shift_down(s, w, pos)
            w *= 2
            sums.append(s)
        total = _window_select(group, sums)
        count = jnp.minimum(pos + 1, 2 << group).astype(F32)
        d_ref[...] = (total / count - z).astype(BF16)

    return pl.pallas_call(body, name=name, grid=(p // tc,), in_specs=[pl.BlockSpec((tp, tc), lambda i: (0, col0 // tc + i))],
                          out_specs=pl.BlockSpec((tp, tc), lambda i: (0, i)), out_shape=_sds((tp, p), BF16),
                          compiler_params=_params(("parallel",)))(u)


def _pool_bwd(name, dd, pg, n_rows, seq_len):
    tp, p = dd.shape
    tc = _tile(pg, TC_MIX, LANE)
    per_group = pg // tc

    def body(dd_ref, dz_ref):
        group = pl.program_id(0) // per_group
        pos = _positions(tp, n_rows, seq_len)
        dd_v = dd_ref[...]
        count = jnp.minimum(pos + 1, 2 << group).astype(F32)
        sums, s, w = [], dd_v / count, 1
        for _ in POOL_WINDOWS:
            s = s + _shift_up(s, w, pos, seq_len)
            w *= 2
            sums.append(s)
        dz_ref[...] = (_window_select(group, sums) - dd_v).astype(BF16)

    spec = pl.BlockSpec((tp, tc), lambda i: (0, i))
    return pl.pallas_call(body, name=name, grid=(p // tc,), in_specs=[spec], out_specs=spec, out_shape=_sds((tp, p), BF16),
                          compiler_params=_params(("parallel",)))(dd)


def _pool_mix(name, dpool, pool_w_b, scale):
    tp, p = dpool.shape
    ng, pg, _ = pool_w_b.shape
    tm = _tile(tp, TM_BIG, 16)

    def epi(acc, ex, outs, rows):
        outs[0][rows, :] = acc
        outs[1][rows, :] = (acc * ex[0][...]).astype(BF16)

    blk = pl.BlockSpec((tm, pg), lambda m, g, k: (m, g))
    return _mm(name, (tp // tm, ng, 1), NN, (dpool, pool_w_b), [blk, pl.BlockSpec((None, pg, pg), lambda m, g, k: (g, 0, 0))],
               (scale,), [pl.BlockSpec((1, pg), lambda m, g, k: (0, g))],
               [_sds((tp, p), F32), _sds((tp, p), BF16)], [blk, blk], None, epi)


def _pool_scale_bwd(name, dy, col_block, ypre, scale):
    tp, p = ypre.shape
    tr = _tile(tp, TR_LN, 16)

    def body(dy_ref, yp_ref, s_ref, o_ref, ds_ref):
        i = pl.program_id(0)
        dyp = dy_ref[...]
        o_ref[...] = (dyp * s_ref[...]).astype(BF16)
        _accumulate(i, ds_ref, jnp.sum(dyp * yp_ref[...], axis=0, keepdims=True))

    row = pl.BlockSpec((tr, p), lambda i: (i, 0))
    vec = pl.BlockSpec((1, p), lambda i: (0, 0))
    return pl.pallas_call(body, name=name, grid=(tp // tr,), in_specs=[pl.BlockSpec((tr, p), lambda i: (i, col_block)), row, vec],
                          out_specs=[row, vec], out_shape=[_sds((tp, p), BF16), _sds((1, p), F32)],
                          compiler_params=_params(("arbitrary",)))(dy, ypre, scale)


def _pool_mix_bwd_in(name, dyps, pool_w_b):
    tp, p = dyps.shape
    ng, pg, _ = pool_w_b.shape
    tm = _tile(tp, TM_BIG, 16)

    def epi(acc, ex, outs, rows):
        outs[0][rows, :] = acc

    blk = pl.BlockSpec((tm, pg), lambda m, g, k: (m, g))
    return _mm(name, (tp // tm, ng, 1), NT, (dyps, pool_w_b), [blk, pl.BlockSpec((None, pg, pg), lambda m, g, k: (g, 0, 0))],
               (), [], [_sds((tp, p), F32)], [blk], None, epi)[0]


def _pool_mix_bwd_w(name, dpool, dyps, pg):
    tp, p = dpool.shape
    ng = p // pg
    tkt = _tile(tp, TM_BIG, LANE)

    def epi(acc, ex, outs, rows):
        outs[0][rows, :] = acc

    blk = pl.BlockSpec((tkt, pg), lambda g, t: (t, g))
    return _mm(name, (ng, tp // tkt), TN_DIMS, (dpool, dyps), [blk, blk], (), [],
               [_sds((ng, pg, pg), F32)], [pl.BlockSpec((None, pg, pg), lambda g, t: (g, 0, 0))], (pg, pg), epi)[0]


def _proj_out(name, y, wout_all, h):
    tp, c = y.shape
    d = wout_all.shape[2]
    w = wout_all.reshape(c, d)
    tm, tn, tk = _tile(tp, TM_BIG, 16), _tile(d, TN, LANE), _tile(c, 4 * TK, LANE)

    def epi(acc, ex, outs, rows):
        outs[0][rows, :] = ALPHA * ex[0][rows, :] + acc

    mn = pl.BlockSpec((tm, tn), lambda m, n, k: (m, n))
    return _mm(name, (tp // tm, d // tn, c // tk), NN, (y, w),
               [pl.BlockSpec((tm, tk), lambda m, n, k: (m, k)), pl.BlockSpec((tk, tn), lambda m, n, k: (k, n))],
               (h,), [mn], [_sds((tp, d), F32)], [mn], (tm, tn), epi)[0]


def _proj_out_bwd_y(name, dp_b, wout_all, deps=()):
    tp, d = dp_b.shape
    ns, ro, _ = wout_all.shape
    tm, tk = _tile(tp, TM_BIG, 16), d

    def epi(acc, ex, outs, rows):
        outs[0][rows, :] = acc

    return _mm(name, (tp // tm, ns, d // tk), NT, (dp_b, wout_all),
               [pl.BlockSpec((tm, tk), lambda m, j, k: (m, k)), pl.BlockSpec((None, ro, tk), lambda m, j, k: (j, 0, k))],
               (), [], [_sds((tp, ns * ro), F32)], [pl.BlockSpec((tm, ro), lambda m, j, k: (m, j))], (tm, ro), epi, deps)[0]


def _proj_out_bwd_w(name, y, dp_b, ns):
    tp, c = y.shape
    d = dp_b.shape[1]
    ro = c // ns
    tkt, tn = _tile(tp, TM_BIG, LANE), _tile(d, 2 * TN, LANE)

    def epi(acc, ex, outs, rows):
        outs[0][rows, :] = acc.astype(BF16)

    return _mm(name, (ns, d // tn, tp // tkt), TN_DIMS, (y, dp_b),
               [pl.BlockSpec((tkt, ro), lambda j, n, t: (t, j)), pl.BlockSpec((tkt, tn), lambda j, n, t: (t, n))],
               (), [], [_sds((ns, ro, d), BF16)], [pl.BlockSpec((None, ro, tn), lambda j, n, t: (j, 0, n))], (ro, tn), epi)[0]


def _proj_in_bwd_w(name, h_b, du, ns, deps=()):
    tp, d = h_b.shape
    ni = du.shape[1] // ns
    tkt, tmd = _tile(tp, TK_TOKENS, LANE), _tile(d, 2 * TN, LANE)

    def epi(acc, ex, outs, rows):
        outs[0][rows, :] = acc.astype(BF16)

    return _mm(name, (ns, d // tmd, tp // tkt), TN_DIMS, (h_b, du),
               [pl.BlockSpec((tkt, tmd), lambda j, i, t: (t, i)), pl.BlockSpec((tkt, ni), lambda j, i, t: (t, j))],
               (), [], [_sds((ns, d, ni), BF16)], [pl.BlockSpec((None, tmd, ni), lambda j, i, t: (j, i, 0))], (tmd, ni), epi, deps)[0]


def _proj_in_bwd_h(name, du, win_all, dp, deps=()):
    tp = du.shape[0]
    ns, d, ni = win_all.shape
    tm, tn = _tile(tp, TM_BIG, 16), _tile(d, TN, LANE)

    def epi(acc, ex, outs, rows):
        outs[0][rows, :] = ALPHA * ex[0][rows, :] + acc

    mn = pl.BlockSpec((tm, tn), lambda m, n, j: (m, n))
    return _mm(name, (tp // tm, d // tn, ns), NT, (du, win_all),
               [pl.BlockSpec((tm, ni), lambda m, n, j: (m, j)), pl.BlockSpec((None, tn, ni), lambda m, n, j: (j, n, 0))],
               (dp,), [mn], [_sds((tp, d), F32)], [mn], (tm, tn), epi, deps)[0]


def _place():
    return lax.axis_index("x"), lax.axis_index("y"), lax.axis_index("c")


def _hbm(a):
    return pltpu.with_memory_space_constraint(a, pltpu.HBM)


def _token_shape():
    return _sds((8, LANE), F32)


def _gather_peers(x, y, c):
    return [(x, y, 1 - c), (1 - x, y, c), (x, 1 - y, c), (1 - x, 1 - y, c)]


def _into_slab(name, w, dev, dtype):
    r, c = w.shape
    tr = _tile(r, max(16, ELEM_BLOCK_BYTES // (4 * c)), 16 if dtype == BF16 else 8)

    def body(dev_ref, w_ref, o_ref):
        o_ref[...] = w_ref[...].astype(dtype)

    return pl.pallas_call(
        body, name=name,
        grid_spec=pltpu.PrefetchScalarGridSpec(
            num_scalar_prefetch=1, grid=(r // tr,), in_specs=[pl.BlockSpec((tr, c), lambda i, dev_ref: (i, 0))],
            out_specs=pl.BlockSpec((None, tr, c), lambda i, dev_ref: (dev_ref[0], i, 0))),
        out_shape=_sds((N_DEV, r, c), dtype), compiler_params=_params(("parallel",)),
    )(dev, w)


def _gather_start(name, lands, after=()):
    n = len(lands)
    n_peer = N_CHIP
    n_in = n + len(after)

    def body(*refs):
        land_refs = refs[:n]
        send_sems, recv_sems = refs[n_in:n_in + n], refs[n_in + n:n_in + 2 * n]
        token = refs[-1]
        x, y, c = _place()
        me = 4 * x + 2 * y + c
        for i in range(n):
            for k, peer in enumerate(_gather_peers(x, y, c)):
                pltpu.make_async_remote_copy(
                    src_ref=land_refs[i].at[me], dst_ref=land_refs[i].at[me], send_sem=send_sems[i].at[k],
                    recv_sem=recv_sems[i].at[k], device_id=peer, device_id_type=MESH).start()
        token[...] = jnp.zeros_like(token)

    sem = pltpu.SemaphoreType.DMA((n_peer,))
    out = pl.pallas_call(
        body, name=name,
        out_shape=[sem] * (2 * n) + [pltpu.HBM(l.shape, l.dtype) for l in lands] + [_token_shape()],
        in_specs=[HBM] * n + [ANY] * len(after),
        out_specs=[SEM] * (2 * n) + [HBM] * n + [pl.BlockSpec(memory_space=pltpu.VMEM)],
        input_output_aliases={i: 2 * n + i for i in range(n)},
        compiler_params=pltpu.CompilerParams(has_side_effects=EFFECT),
    )(*[_hbm(l) for l in lands], *after)
    per = [(out[i], out[n + i], out[2 * n + i]) for i in range(n)]
    return per, out[-1]


def _gather_wait(name, started, after):
    send_sems, recv_sems, land = started
    after = after if isinstance(after, tuple) else (after,)

    def body(land_ref, send_ref, recv_ref, *rest):
        x, y, c = _place()
        for k, (px, py, pc) in enumerate(_gather_peers(x, y, c)):
            cp = pltpu.make_async_remote_copy(
                src_ref=land_ref.at[4 * x + 2 * y + c], dst_ref=land_ref.at[4 * px + 2 * py + pc], send_sem=send_ref.at[k],
                recv_sem=recv_ref.at[k], device_id=(px, py, pc), device_id_type=MESH)
            cp.wait_send()
            cp.wait_recv()

    return pl.pallas_call(
        body, name=name, out_shape=pltpu.HBM(land.shape, land.dtype),
        in_specs=(HBM, SEM, SEM) + (ANY,) * len(after), out_specs=HBM, input_output_aliases={0: 0},
        compiler_params=pltpu.CompilerParams(has_side_effects=EFFECT),
    )(land, send_sems, recv_sems, *after)


def _gather_finish(name, land, deps=()):
    def body(land_ref, *rest):
        out_ref, send_sems, recv_sems = rest[len(deps):]
        x, y, c = _place()
        copies = []
        for k, (px, py) in enumerate([(1 - x, y), (x, 1 - y), (1 - x, 1 - y)]):
            slab = 4 * px + 2 * py + c
            copies.append(pltpu.make_async_remote_copy(
                src_ref=land_ref.at[slab], dst_ref=out_ref.at[slab], send_sem=send_sems.at[k], recv_sem=recv_sems.at[k],
                device_id=(x, y, 1 - c), device_id_type=MESH))
        for cp in copies:
            cp.start()
        for cp in copies:
            cp.wait()

    return pl.pallas_call(
        body, name=name, out_shape=_sds(land.shape, land.dtype), in_specs=[ANY] + [ANY] * len(deps), out_specs=ANY,
        input_output_aliases={0: 0},
        scratch_shapes=[pltpu.SemaphoreType.DMA((N_CHIP - 1,)), pltpu.SemaphoreType.DMA((N_CHIP - 1,))],
    )(land, *deps)


def _route_sibling(x, y, c):
    return [(2 * j + (1 - c), (x, y, 1 - c)) for j in range(N_CHIP)]


def _route_chips(x, y, c):
    return [(k, (px, py, c)) for k, (px, py) in enumerate([(1 - x, y), (x, 1 - y), (1 - x, 1 - y)])]


def _exchange_start(name, src, route):
    n_copy = len(route(0, 0, 0))

    def body(s_ref, land_ref, send_sems, recv_sems, s_out, land_out, token):
        for k, (slab, peer) in enumerate(route(*_place())):
            pltpu.make_async_remote_copy(
                src_ref=s_ref.at[slab], dst_ref=land_ref.at[k], send_sem=send_sems.at[k], recv_sem=recv_sems.at[k],
                device_id=peer, device_id_type=MESH).start()
        token[...] = jnp.zeros_like(token)

    land = lax.empty((n_copy,) + src.shape[1:], src.dtype)
    sem = pltpu.SemaphoreType.DMA((n_copy,))
    out = pl.pallas_call(
        body, name=name,
        out_shape=[sem, sem, pltpu.HBM(src.shape, src.dtype), pltpu.HBM(land.shape, land.dtype), _token_shape()],
        in_specs=[HBM, HBM], out_specs=[SEM, SEM, HBM, HBM, pl.BlockSpec(memory_space=pltpu.VMEM)],
        input_output_aliases={0: 2, 1: 3}, compiler_params=pltpu.CompilerParams(has_side_effects=EFFECT),
    )(_hbm(src), _hbm(land))
    return out[:4], out[4]


def _exchange_wait(name, started, route, after):
    send_sems, recv_sems, src, land = started

    def body(s_ref, land_ref, send_ref, recv_ref, after_ref, s_out, land_out):
        for k, (slab, peer) in enumerate(route(*_place())):
            cp = pltpu.make_async_remote_copy(
                src_ref=s_ref.at[slab], dst_ref=land_ref.at[k], send_sem=send_ref.at[k], recv_sem=recv_ref.at[k],
                device_id=peer, device_id_type=MESH)
            cp.wait_send()
            cp.wait_recv()

    return pl.pallas_call(
        body, name=name, out_shape=(pltpu.HBM(src.shape, src.dtype), pltpu.HBM(land.shape, land.dtype)),
        in_specs=(HBM, HBM, SEM, SEM, ANY), out_specs=(HBM, HBM), input_output_aliases={0: 0, 1: 1},
        compiler_params=pltpu.CompilerParams(has_side_effects=EFFECT),
    )(src, land, send_sems, recv_sems, after)


def _all_gather(name, land):
    def body(land_ref, out_ref, send_sems, recv_sems):
        x, y, c = _place()
        me, sibling = (x, y, c), (x, y, 1 - c)
        chips = [(1 - x, y), (x, 1 - y), (1 - x, 1 - y)]

        def copy(k, block, to):
            slab = 4 * block[0] + 2 * block[1] + block[2]
            return pltpu.make_async_remote_copy(
                src_ref=land_ref.at[slab], dst_ref=out_ref.at[slab],
                send_sem=send_sems.at[k], recv_sem=recv_sems.at[k], device_id=to, device_id_type=MESH)

        first = [copy(0, me, sibling)] + [copy(1 + j, me, (*chip, c)) for j, chip in enumerate(chips)]
        for cp in first:
            cp.start()
        passed = [copy(4 + j, (*chip, c), sibling) for j, chip in enumerate(chips)]
        for j, chip in enumerate(chips):
            copy(1 + j, (*chip, c), me).wait_recv()
            passed[j].start()
        copy(0, sibling, me).wait_recv()
        for j, chip in enumerate(chips):
            copy(4 + j, (*chip, 1 - c), me).wait_recv()
        for cp in first + passed:
            cp.wait_send()

    return pl.pallas_call(
        body, name=name, out_shape=_sds(land.shape, land.dtype), in_specs=[ANY], out_specs=ANY, input_output_aliases={0: 0},
        scratch_shapes=[pltpu.SemaphoreType.DMA((N_DEV - 1,)), pltpu.SemaphoreType.DMA((N_DEV - 1,))],
    )(land)


def _add_sibling(name, part, recv, place):
    _, r, c = part.shape
    tr = _tile(r, max(16, 2 * ELEM_BLOCK_BYTES // (2 * c)), 16)
    n_out = N_CHIP - 1

    def chip_of(k, x_ref, y_ref):
        px = jnp.where(k == 1, x_ref[0], 1 - x_ref[0])
        py = jnp.where(k == 0, y_ref[0], 1 - y_ref[0])
        return 2 * px + py

    def body(x_ref, y_ref, c_ref, p_ref, r_ref, o_ref):
        o_ref[...] = (p_ref[...].astype(F32) + r_ref[...].astype(F32)).astype(BF16)

    return pl.pallas_call(
        body, name=name,
        grid_spec=pltpu.PrefetchScalarGridSpec(
            num_scalar_prefetch=3, grid=(n_out, r // tr),
            in_specs=[pl.BlockSpec((None, tr, c), lambda k, i, x_ref, y_ref, c_ref: (2 * chip_of(k, x_ref, y_ref) + c_ref[0], i, 0)),
                      pl.BlockSpec((None, tr, c), lambda k, i, x_ref, y_ref, c_ref: (chip_of(k, x_ref, y_ref), i, 0))],
            out_specs=pl.BlockSpec((None, tr, c), lambda k, i, x_ref, y_ref, c_ref: (k, i, 0))),
        out_shape=_sds((n_out, r, c), BF16), compiler_params=_params(("parallel", "parallel")),
    )(*place, part, recv)


def _adamw_math(w, g, m, v):
    m = ADAM_B1 * m + (1.0 - ADAM_B1) * g
    v = ADAM_B2 * v + (1.0 - ADAM_B2) * (g * g)
    m_hat = m / (1.0 - ADAM_B1 ** ADAM_STEP)
    v_hat = v / (1.0 - ADAM_B2 ** ADAM_STEP)
    delta = -ADAM_LR * (m_hat / (jnp.sqrt(v_hat) + ADAM_EPS) + ADAM_WD * w)
    return delta, m, v


def _adamw_big(name, part, from_sibling, from_chips, dev, chip, w, m, v):
    r, c = w.shape
    tr = _tile(r, max(16, ELEM_BLOCK_BYTES // (4 * c)), 16)

    def body(dev_ref, chip_ref, p_ref, s_ref, r_ref, w_ref, m_ref, v_ref, g_out, d_out, m_out, v_out):
        g = p_ref[...].astype(F32) + s_ref[...].astype(F32)
        for k in range(N_CHIP - 1):
            g = g + r_ref[k].astype(F32)
        delta, m_new, v_new = _adamw_math(w_ref[...], g, m_ref[...], v_ref[...])
        g_out[...] = g
        d_out[...] = delta
        m_out[...] = m_new
        v_out[...] = v_new

    blk = pl.BlockSpec((tr, c), lambda i, dev_ref, chip_ref: (i, 0))
    return pl.pallas_call(
        body, name=name,
        grid_spec=pltpu.PrefetchScalarGridSpec(
            num_scalar_prefetch=2, grid=(r // tr,),
            in_specs=[pl.BlockSpec((None, tr, c), lambda i, dev_ref, chip_ref: (dev_ref[0], i, 0)),
                      pl.BlockSpec((None, tr, c), lambda i, dev_ref, chip_ref: (chip_ref[0], i, 0)),
                      pl.BlockSpec((N_CHIP - 1, tr, c), lambda i, dev_ref, chip_ref: (0, i, 0)), blk, blk, blk],
            out_specs=[blk, blk, blk, blk]),
        out_shape=[_sds((r, c), F32)] * 4, compiler_params=_params(("parallel",)),
    )(dev, chip, part, from_sibling, from_chips, w, m, v)


def _sum_parts(name, parts):
    n, r, c = parts.shape
    tr = _tile(r, max(8, ELEM_BLOCK_BYTES // (4 * c)), 8)

    def body(p_ref, o_ref):
        acc = p_ref[0]
        for k in range(1, n):
            acc = acc + p_ref[k]
        o_ref[...] = acc

    return pl.pallas_call(body, name=name, grid=(r // tr,), in_specs=[pl.BlockSpec((n, tr, c), lambda i: (0, i, 0))],
                          out_specs=pl.BlockSpec((tr, c), lambda i: (i, 0)), out_shape=_sds((r, c), F32),
                          compiler_params=_params(("parallel",)))(parts)


def _adamw_small(name, ws, gs, ms, vs):
    n = len(ws)

    def body(*refs):
        ins, outs = refs[:4 * n], refs[4 * n:]
        for i in range(n):
            delta, m_new, v_new = _adamw_math(ins[i][...], ins[n + i][...], ins[2 * n + i][...], ins[3 * n + i][...])
            outs[i][...] = delta
            outs[n + i][...] = m_new
            outs[2 * n + i][...] = v_new

    shapes = [_sds(w.shape, F32) for w in ws]
    return pl.pallas_call(body, name=name, out_shape=shapes * 3)(*ws, *gs, *ms, *vs)


def _pad_rows(a, rows):
    return jnp.pad(a, ((0, rows - a.shape[0]), (0, 0)))


def kernel(x, meta_tokens, ffn1_w_gu, ffn1_w_down, ln1_g, ln1_b, w_in, conv_w, pool_w, pool_scale, w_out, ln2_g, ln2_b, ffn2_w_gu, ffn2_w_down, ln3_g, ln3_b, loss_target, m_meta_tokens, m_ffn1_w_gu, m_ffn1_w_down, m_ln1_g, m_ln1_b, m_w_in, m_conv_w, m_pool_w, m_pool_scale, m_w_out, m_ln2_g, m_ln2_b, m_ffn2_w_gu, m_ffn2_w_down, m_ln3_g, m_ln3_b, v_meta_tokens, v_ffn1_w_gu, v_ffn1_w_down, v_ln1_g, v_ln1_b, v_w_in, v_conv_w, v_pool_w, v_pool_scale, v_w_out, v_ln2_g, v_ln2_b, v_ffn2_w_gu, v_ffn2_w_down, v_ln3_g, v_ln3_b):
    names = ["meta_tokens", "ffn1_w_gu", "ffn1_w_down", "ln1_g", "ln1_b", "w_in", "conv_w", "pool_w", "pool_scale", "w_out",
             "ln2_g", "ln2_b", "ffn2_w_gu", "ffn2_w_down", "ln3_g", "ln3_b"]
    w_of = dict(zip(names, [meta_tokens, ffn1_w_gu, ffn1_w_down, ln1_g, ln1_b, w_in, conv_w, pool_w, pool_scale, w_out,
                            ln2_g, ln2_b, ffn2_w_gu, ffn2_w_down, ln3_g, ln3_b]))
    m_of = dict(zip(names, [m_meta_tokens, m_ffn1_w_gu, m_ffn1_w_down, m_ln1_g, m_ln1_b, m_w_in, m_conv_w, m_pool_w, m_pool_scale,
                            m_w_out, m_ln2_g, m_ln2_b, m_ffn2_w_gu, m_ffn2_w_down, m_ln3_g, m_ln3_b]))
    v_of = dict(zip(names, [v_meta_tokens, v_ffn1_w_gu, v_ffn1_w_down, v_ln1_g, v_ln1_b, v_w_in, v_conv_w, v_pool_w, v_pool_scale,
                            v_w_out, v_ln2_g, v_ln2_b, v_ffn2_w_gu, v_ffn2_w_down, v_ln3_g, v_ln3_b]))

    n_seq, seq, d = x.shape
    seq_len = seq + N_META
    n_rows = n_seq * seq_len
    tp = -(-n_rows // ROW_ALIGN) * ROW_ALIGN
    c_conv = conv_w.shape[2] * N_DEV
    p_pool = pool_scale.shape[1]
    pg = pool_w.shape[3]
    assert c_conv == p_pool and p_pool == N_POOL_GROUPS * pg and POOL_WINDOWS == tuple(2 << g for g in range(N_POOL_GROUPS))
    assert (N_POOL_GROUPS * pg * pg) % d == 0 and pg % LANE == 0

    xi, yi, ci = _place()
    dev_index = 4 * xi + 2 * yi + ci
    dev = jnp.reshape(dev_index, (1,)).astype(jnp.int32)
    chip = jnp.reshape(2 * xi + yi, (1,)).astype(jnp.int32)
    place = tuple(jnp.reshape(a, (1,)).astype(jnp.int32) for a in (xi, yi, ci))

    big = ["ffn1_w_gu", "ffn1_w_down", "w_in", "w_out", "ffn2_w_gu", "ffn2_w_down"]
    wcol = d // N_DEV
    conv_rows = 8
    small_local = jnp.concatenate([
        meta_tokens,
        pool_w[0].reshape(N_POOL_GROUPS * (pg // N_DEV), pg),
        jnp.pad(conv_w[0], ((0, conv_rows - CONV_K), (0, wcol - conv_w.shape[2]))),
    ], axis=0)
    lands = {n: _into_slab(f"slab_{n}", w_of[n][0], dev, BF16) for n in big}
    lands["small"] = _into_slab("slab_small", small_local, dev, F32)
    started = {}

    def start(tag, which, after=()):
        per, token = _gather_start(f"ag_start_{tag}", [lands[n] for n in which], after)
        started.update(zip(which, per))
        return token

    def gathered(n, after, then_start=()):
        land = _gather_wait(f"ag_wait_{n}", started[n], after)
        deps = (start(f"after_{n}", then_start, (land,)),) if then_start else ()
        return _gather_finish(f"ag_finish_{n}", land, deps)

    gather_token = start("first", ["small", "ffn1_w_gu"])
    small_all = gathered("small", gather_token)
    r0, r1 = N_META, N_META + N_POOL_GROUPS * (pg // N_DEV)
    meta_full = jnp.transpose(small_all[:, :r0], (1, 0, 2)).reshape(N_META, d)
    pool_w_full = jnp.transpose(small_all[:, r0:r1].reshape(N_DEV, N_POOL_GROUPS, pg // N_DEV, pg), (1, 0, 2, 3)).reshape(N_POOL_GROUPS, pg, pg)
    conv_w_full = jnp.transpose(small_all[:, r1:r1 + CONV_K, :conv_w.shape[2]], (1, 0, 2)).reshape(CONV_K, c_conv)
    pool_w_b = pool_w_full.astype(BF16)

    h0 = jnp.concatenate([jnp.broadcast_to(meta_full[None], (n_seq, N_META, d)), x], axis=1).reshape(n_rows, d)
    h0 = _pad_rows(h0, tp)
    h0_b = h0.astype(BF16)
    tgt = _pad_rows(jnp.pad(loss_target, ((0, 0), (N_META, 0), (0, 0))).reshape(n_rows, d), tp)

    early = (h0_b, tgt) + tuple(lands[n] for n in big[1:])
    early += tuple(a[n][0] for n in ("ffn1_w_gu", "ffn2_w_gu") for a in (m_of, v_of))
    wgu1 = gathered("ffn1_w_gu", early, ["ffn1_w_down", "w_in"])
    gu1, act1 = _ffn_gu("ffn1_gu", h0_b, wgu1)
    wd1 = gathered("ffn1_w_down", act1, ["w_out", "ffn2_w_gu"]).reshape(N_CHIP, -1, d)
    pre1 = _ffn_down("ffn1_down", act1, wd1, h0)
    win_all = gathered("w_in", pre1)
    h1, h1_b = _ln_fwd("ln1", pre1, ln1_g, ln1_b)

    u = _proj_in("mix_in", h1_b, win_all)
    wout_all = gathered("w_out", u)
    y_conv = _conv_fwd("mix_conv", u, conv_w_full, n_rows, seq_len)
    dpool = _pool_fwd("mix_pool", u, 3 * c_conv, p_pool, pg, n_rows, seq_len)
    ypre, y_pool = _pool_mix("mix_pool_w", dpool, pool_w_b, pool_scale)
    y_mix = jnp.concatenate([y_conv, y_pool], axis=1)
    pre2 = _proj_out("mix_out", y_mix, wout_all, h1)
    wgu2 = gathered("ffn2_w_gu", pre2, ["ffn2_w_down"])
    h2, h2_b = _ln_fwd("ln2", pre2, ln2_g, ln2_b)

    gu2, act2 = _ffn_gu("ffn2_gu", h2_b, wgu2)
    wd2 = gathered("ffn2_w_down", act2).reshape(N_CHIP, -1, d)
    pre3 = _ffn_down("ffn2_down", act2, wd2, h2)

    dpre3, dpre3_b, d_ln3_g, d_ln3_b, sq = _ln_loss_bwd("ln3_loss", pre3, tgt, ln3_g, ln3_b, n_rows, seq_len)
    loss = lax.psum(0.5 * jnp.sum(sq) / d, ("x", "y", "c"))
    in_sibling, reducing = [], {}

    def exchange(after, new=None):
        tokens = []
        while in_sibling:
            n, started = in_sibling.pop(0)
            part, from_sibling = _exchange_wait(f"rs_sibling_wait_{n}", started, _route_sibling, after)
            summed = _add_sibling(f"rs_add_{n}", part, from_sibling, place)
            started, token = _exchange_start(f"rs_chips_start_{n}", summed, _route_chips)
            reducing[n] = (part, from_sibling, started)
            tokens.append(token)
        if new is not None:
            started, token = _exchange_start(f"rs_sibling_start_{new[0]}", new[1], _route_sibling)
            in_sibling.append((new[0], started))
            tokens.append(token)
        return tuple(tokens)

    dgu2 = _ffn_bwd_dgu("ffn2_bwd_dgu", dpre3_b, wd2, gu2)
    g_wd2 = _ffn_bwd_wd("ffn2_bwd_wd", act2, dpre3_b)
    tokens = exchange(g_wd2, ("ffn2_w_down", g_wd2.reshape(N_DEV, -1, d)))
    g_wgu2 = _ffn_bwd_wgu("ffn2_bwd_wgu", h2_b, dgu2, tokens)
    tokens = exchange(g_wgu2, ("ffn2_w_gu", g_wgu2))
    dh2 = _ffn_bwd_dh("ffn2_bwd_dh", dgu2, wgu2, dpre3, tokens)
    tokens = exchange(dh2)
    dpre2, dpre2_b, d_ln2_g, d_ln2_b = _ln_bwd("ln2_bwd", pre2, dh2, ln2_g)

    dy_mix = _proj_out_bwd_y("mix_out_bwd_y", dpre2_b, wout_all, tokens)
    g_wout = _proj_out_bwd_w("mix_out_bwd_w", y_mix, dpre2_b, N_DEV)
    tokens = exchange(g_wout, ("w_out", g_wout))
    dyps, d_pool_scale = _pool_scale_bwd("mix_pool_scale_bwd", dy_mix, 1, ypre, pool_scale)
    dd = _pool_mix_bwd_in("mix_pool_w_bwd_in", dyps, pool_w_b)
    d_pool_w = _pool_mix_bwd_w("mix_pool_w_bwd_w", dpool, dyps, pg)
    du_pool = _pool_bwd("mix_pool_bwd", dd, pg, n_rows, seq_len)
    du_b, du_c, du_x, d_conv_w = _conv_bwd("mix_conv_bwd", u, conv_w_full, dy_mix, n_rows, seq_len)
    du = jnp.concatenate([du_b, du_c, du_x, du_pool], axis=1)
    g_win = _proj_in_bwd_w("mix_in_bwd_w", h1_b, du, N_DEV, tokens)
    tokens = exchange(g_win, ("w_in", g_win))
    dh1 = _proj_in_bwd_h("mix_in_bwd_h", du, win_all, dpre2, tokens)
    tokens = exchange(dh1)
    dpre1, dpre1_b, d_ln1_g, d_ln1_b = _ln_bwd("ln1_bwd", pre1, dh1, ln1_g)

    def widen(a):
        return jnp.pad(a, ((0, 0), (0, d - a.shape[1])))

    small_part = jnp.concatenate([
        d_ln1_g, d_ln1_b, d_ln2_g, d_ln2_b, d_ln3_g, d_ln3_b, widen(d_pool_scale), widen(d_conv_w), d_pool_w.reshape(-1, d)], axis=0)
    n_small_rows = small_part.shape[0]
    small_part = _pad_rows(small_part, -(-n_small_rows // 8) * 8)
    small_started, token = _gather_start("ag_start_small_grads", [_into_slab("slab_small_grads", small_part, dev, F32)])

    dgu1 = _ffn_bwd_dgu("ffn1_bwd_dgu", dpre1_b, wd1, gu1, tokens + (token,))
    g_wgu1 = _ffn_bwd_wgu("ffn1_bwd_wgu", h0_b, dgu1)
    tokens = exchange(g_wgu1, ("ffn1_w_gu", g_wgu1))
    g_wd1 = _ffn_bwd_wd("ffn1_bwd_wd", act1, dpre1_b, tokens)
    tokens = exchange(g_wd1, ("ffn1_w_down", g_wd1.reshape(N_DEV, -1, d)))
    dh0 = _ffn_bwd_dh("ffn1_bwd_dh", dgu1, wgu1, dpre1, tokens)
    tokens = exchange(dh0)

    dh0_seq = dh0[:n_rows].reshape(n_seq, seq_len, d)
    grad_x = dh0_seq[:, N_META:]
    d_meta = jnp.sum(dh0_seq[:, :N_META], axis=0)

    grads, deltas, new_m, new_v = {}, {}, {}, {}
    after = tokens[0]
    for n in ["ffn2_w_down", "ffn2_w_gu", "w_out", "w_in", "ffn1_w_gu", "ffn1_w_down"]:
        part, from_sibling, started = reducing[n]
        _, from_chips = _exchange_wait(f"rs_chips_wait_{n}", started, _route_chips, after)
        g, dl, mm, vv = _adamw_big(f"adamw_{n}", part, from_sibling, from_chips, dev, chip, w_of[n][0], m_of[n][0], v_of[n][0])
        grads[n], deltas[n], new_m[n], new_v[n] = g[None], dl[None], mm[None], vv[None]
        after = g

    small_sum = _sum_parts("small_sum", _gather_finish("ag_finish_small_grads", _gather_wait("ag_wait_small_grads", small_started[0], dh0)))
    meta_sum = _sum_parts("meta_sum", _all_gather("ag_meta_grads", _into_slab("slab_meta_grads", d_meta, dev, F32)))
    o = 7 + CONV_K
    g_small = {
        "ln1_g": small_sum[0:1], "ln1_b": small_sum[1:2], "ln2_g": small_sum[2:3], "ln2_b": small_sum[3:4],
        "ln3_g": small_sum[4:5], "ln3_b": small_sum[5:6], "pool_scale": small_sum[6:7, :p_pool],
        "conv_w": lax.dynamic_slice_in_dim(small_sum[7:o, :c_conv], dev_index * (c_conv // N_DEV), c_conv // N_DEV, axis=1)[None],
        "meta_tokens": lax.dynamic_slice_in_dim(meta_sum, dev_index * wcol, wcol, axis=1),
        "pool_w": lax.dynamic_slice_in_dim(small_sum[o:n_small_rows].reshape(N_POOL_GROUPS, pg, pg),
                                           dev_index * (pg // N_DEV), pg // N_DEV, axis=1)[None],
    }
    small = ["meta_tokens", "ln1_g", "ln1_b", "conv_w", "pool_w", "pool_scale", "ln2_g", "ln2_b", "ln3_g", "ln3_b"]

    def flat(a):
        return a.reshape(-1, a.shape[-1])

    outs = _adamw_small("adamw_small", [flat(w_of[n]) for n in small], [flat(g_small[n]) for n in small],
                        [flat(m_of[n]) for n in small], [flat(v_of[n]) for n in small])
    ns = len(small)
    for i, n in enumerate(small):
        shape = w_of[n].shape
        grads[n] = g_small[n].reshape(shape)
        deltas[n], new_m[n], new_v[n] = outs[i].reshape(shape), outs[ns + i].reshape(shape), outs[2 * ns + i].reshape(shape)

    return (loss, grad_x, *[grads[n] for n in names], *[deltas[n] for n in names],
            *[new_m[n] for n in names], *[new_v[n] for n in names])
```

```python
import functools

import jax
import jax.numpy as jnp
from jax import lax
from jax.experimental import pallas as pl
from jax.experimental.pallas import tpu as pltpu

N_DEV = 8
N_CHIP = 4
N_META = 16
CONV_K = 3
POOL_WINDOWS = (2, 4, 8, 16)
N_POOL_GROUPS = len(POOL_WINDOWS)
LN_EPS = 1e-5
DEPTH = 1
ALPHA = (2.0 * DEPTH) ** 0.25
ADAM_LR = 0.001
ADAM_B1 = 0.9
ADAM_B2 = 0.999
ADAM_EPS = 1e-08
ADAM_WD = 0.01
ADAM_STEP = 10

V7X_VMEM_BYTES = 64 * 1024 * 1024
VMEM_LIMIT = V7X_VMEM_BYTES - 6 * 1024 * 1024
LANE = 128
ROW_ALIGN = 3 * LANE
TM_BIG = 1408
TM_WIDE = 704
TM_GU = 528
TK = 512
TK_TOKENS = 1408
TN = 1024
TR_LN = 128
ELEM_BLOCK_BYTES = 1 << 20
TC_MIX = LANE
EPILOGUE_ROWS = 64

NN = (((1,), (0,)), ((), ()))
NT = (((1,), (1,)), ((), ()))
TN_DIMS = (((0,), (0,)), ((), ()))
MESH = pl.DeviceIdType.MESH
BF16 = jnp.bfloat16
F32 = jnp.float32
ANY = pl.BlockSpec(memory_space=pl.ANY)
HBM = pl.BlockSpec(memory_space=pltpu.HBM)
SEM = pl.BlockSpec(memory_space=pltpu.SEMAPHORE)
EFFECT = pltpu.SideEffectType.DATAFLOW_SIDE_EFFECTING


def _tile(n, target, mult):
    best = None
    for t in range(mult, min(n, target) + 1, mult):
        if n % t == 0:
            best = t
    return n if best is None else best


def _params(sem):
    return pltpu.CompilerParams(dimension_semantics=sem, vmem_limit_bytes=VMEM_LIMIT)


def _sds(shape, dtype):
    return jax.ShapeDtypeStruct(shape, dtype)


def _row_chunks(n_rows, fn):
    ch = _tile(n_rows, EPILOGUE_ROWS, 16)

    def step(i, carry):
        fn(pl.ds(pl.multiple_of(i * ch, ch), ch))
        return carry

    lax.fori_loop(0, n_rows // ch, step, 0)


def _mm(name, grid, dims, ab, ab_specs, extras, extra_specs, out_shape, out_specs, acc_shape, epilogue, deps=()):
    nk = grid[-1]
    n_extra = len(extras)
    n_in = 2 + n_extra + len(deps)
    n_out = len(out_shape)
    kax = len(grid) - 1

    def body(*refs):
        a_ref, b_ref = refs[0], refs[1]
        ex = refs[2:2 + n_extra]
        outs = refs[n_in:n_in + n_out]
        if nk == 1:
            epilogue(lax.dot_general(a_ref[...], b_ref[...], dims, preferred_element_type=F32), ex, outs, slice(None))
            return
        acc = refs[-1]
        k = pl.program_id(kax)

        @pl.when(k == 0)
        def _():
            acc[...] = lax.dot_general(a_ref[...], b_ref[...], dims, preferred_element_type=F32)

        @pl.when(k > 0)
        def _():
            acc[...] += lax.dot_general(a_ref[...], b_ref[...], dims, preferred_element_type=F32)

        @pl.when(k == nk - 1)
        def _():
            _row_chunks(acc_shape[0], lambda rows: epilogue(acc[rows, :], ex, outs, rows))

    scratch = [] if nk == 1 else [pltpu.VMEM(acc_shape, F32)]
    sem = ("parallel",) * kax + ("arbitrary",)
    return pl.pallas_call(
        body, name=name, grid=grid, in_specs=list(ab_specs) + list(extra_specs) + [ANY] * len(deps), out_specs=list(out_specs),
        out_shape=list(out_shape), scratch_shapes=scratch, compiler_params=_params(sem),
    )(*ab, *extras, *deps)


def _silu_parts(g):
    s = 1.0 / (1.0 + jnp.exp(-g))
    return s, g * s


def _ffn_gu(name, h_b, wgu_all):
    tp, d = h_b.shape
    ns, _, ng = wgu_all.shape
    half = ns // 2
    tm, tk = _tile(tp, TM_GU, 16), _tile(d, 2 * TK, LANE)
    grid = (tp // tm, half, d // tk)
    nk = grid[-1]

    def body(h_ref, wg_ref, wu_ref, gu_ref, act_ref, acc_g, acc_u):
        k = pl.program_id(2)

        @pl.when(k == 0)
        def _():
            acc_g[...] = jnp.dot(h_ref[...], wg_ref[...], preferred_element_type=F32)
            acc_u[...] = jnp.dot(h_ref[...], wu_ref[...], preferred_element_type=F32)

        @pl.when(k > 0)
        def _():
            acc_g[...] += jnp.dot(h_ref[...], wg_ref[...], preferred_element_type=F32)
            acc_u[...] += jnp.dot(h_ref[...], wu_ref[...], preferred_element_type=F32)

        @pl.when(k == nk - 1)
        def _():
            def finish(rows):
                g = acc_g[rows, :]
                u = acc_u[rows, :]
                _, silu = _silu_parts(g)
                gu_ref[0, rows, :] = g.astype(BF16)
                gu_ref[1, rows, :] = u.astype(BF16)
                act_ref[rows, :] = (silu * u).astype(BF16)

            _row_chunks(tm, finish)

    return pl.pallas_call(
        body, name=name, grid=grid,
        in_specs=[pl.BlockSpec((tm, tk), lambda m, s, k: (m, k)),
                  pl.BlockSpec((None, tk, ng), lambda m, s, k: (s, k, 0)),
                  pl.BlockSpec((None, tk, ng), lambda m, s, k: (s + half, k, 0))],
        out_specs=[pl.BlockSpec((None, 2, tm, ng), lambda m, s, k: (s, 0, m, 0)),
                   pl.BlockSpec((None, tm, ng), lambda m, s, k: (s, m, 0))],
        out_shape=[_sds((half, 2, tp, ng), BF16), _sds((half, tp, ng), BF16)],
        scratch_shapes=[pltpu.VMEM((tm, ng), F32), pltpu.VMEM((tm, ng), F32)],
        compiler_params=_params(("parallel", "parallel", "arbitrary")),
    )(h_b, wgu_all, wgu_all)


def _ffn_down(name, act, wd4, h):
    ns, tp, ng = act.shape
    d = wd4.shape[2]
    tm, tn = _tile(tp, TM_WIDE, 16), _tile(d, TN, LANE)

    def epi(acc, ex, outs, rows):
        outs[0][rows, :] = ALPHA * ex[0][rows, :] + 0.5 * acc

    return _mm(name, (tp // tm, d // tn, ns), NN, (act, wd4),
               [pl.BlockSpec((None, tm, ng), lambda m, n, s: (s, m, 0)),
                pl.BlockSpec((None, ng, tn), lambda m, n, s: (s, 0, n))],
               (h,), [pl.BlockSpec((tm, tn), lambda m, n, s: (m, n))],
               [_sds((tp, d), F32)], [pl.BlockSpec((tm, tn), lambda m, n, s: (m, n))], (tm, tn), epi)[0]


def _ffn_bwd_dgu(name, dp_b, wd4, gu, deps=()):
    tp, d = dp_b.shape
    ns, ng, _ = wd4.shape
    tm, tk = _tile(tp, TM_WIDE, 16), _tile(d, 2 * TK, LANE)

    def epi(acc, ex, outs, rows):
        g = ex[0][0, rows, :].astype(F32)
        u = ex[0][1, rows, :].astype(F32)
        da = 0.5 * acc
        s, silu = _silu_parts(g)
        outs[0][0, rows, :] = (da * u * (s + silu * (1.0 - s))).astype(BF16)
        outs[0][1, rows, :] = (da * silu).astype(BF16)

    return _mm(name, (tp // tm, ns, d // tk), NT, (dp_b, wd4),
               [pl.BlockSpec((tm, tk), lambda m, s, k: (m, k)),
                pl.BlockSpec((None, ng, tk), lambda m, s, k: (s, 0, k))],
               (gu,), [pl.BlockSpec((None, 2, tm, ng), lambda m, s, k: (s, 0, m, 0))],
               [_sds((ns, 2, tp, ng), BF16)], [pl.BlockSpec((None, 2, tm, ng), lambda m, s, k: (s, 0, m, 0))],
               (tm, ng), epi, deps)[0]


def _ffn_bwd_wd(name, act, dp_b, deps=()):
    ns, tp, ng = act.shape
    d = dp_b.shape[1]
    tkt, tn = _tile(tp, TK_TOKENS, LANE), _tile(d, TN, LANE)

    def epi(acc, ex, outs, rows):
        outs[0][rows, :] = (0.5 * acc).astype(BF16)

    return _mm(name, (ns, d // tn, tp // tkt), TN_DIMS, (act, dp_b),
               [pl.BlockSpec((None, tkt, ng), lambda s, n, t: (s, t, 0)),
                pl.BlockSpec((tkt, tn), lambda s, n, t: (t, n))],
               (), [], [_sds((ns, ng, d), BF16)], [pl.BlockSpec((None, ng, tn), lambda s, n, t: (s, 0, n))],
               (ng, tn), epi, deps)[0]


def _ffn_bwd_wgu(name, h_b, dgu, deps=()):
    tp, d = h_b.shape
    ns, _, _, ng = dgu.shape
    tkt, tmd = _tile(tp, TK_TOKENS, LANE), _tile(d, TN, LANE)

    def epi(acc, ex, outs, rows):
        outs[0][rows, :] = acc.astype(BF16)

    return _mm(name, (ns, 2, d // tmd, tp // tkt), TN_DIMS, (h_b, dgu),
               [pl.BlockSpec((tkt, tmd), lambda s, j, i, t: (t, i)),
                pl.BlockSpec((None, None, tkt, ng), lambda s, j, i, t: (s, j, t, 0))],
               (), [], [_sds((2 * ns, d, ng), BF16)],
               [pl.BlockSpec((None, tmd, ng), lambda s, j, i, t: (j * ns + s, i, 0))], (tmd, ng), epi, deps)[0]


def _ffn_bwd_dh(name, dgu, wgu_all, dp, deps=()):
    ns, _, tp, ng = dgu.shape
    d = wgu_all.shape[1]
    tm, tn = _tile(tp, TM_WIDE, 16), _tile(d, TN, LANE)

    def epi(acc, ex, outs, rows):
        outs[0][rows, :] = ALPHA * ex[0][rows, :] + acc

    return _mm(name, (tp // tm, d // tn, 2 * ns), NT, (dgu, wgu_all),
               [pl.BlockSpec((None, None, tm, ng), lambda m, n, j: (j % ns, j // ns, m, 0)),
                pl.BlockSpec((None, tn, ng), lambda m, n, j: (j, n, 0))],
               (dp,), [pl.BlockSpec((tm, tn), lambda m, n, j: (m, n))],
               [_sds((tp, d), F32)], [pl.BlockSpec((tm, tn), lambda m, n, j: (m, n))], (tm, tn), epi, deps)[0]


def _ln_stats(x):
    mu = jnp.mean(x, axis=-1, keepdims=True)
    xc = x - mu
    var = jnp.mean(xc * xc, axis=-1, keepdims=True)
    rstd = lax.rsqrt(var + LN_EPS)
    return xc * rstd, rstd


def _ln_bwd_rows(dy, xhat, rstd, g):
    dxh = dy * g
    m1 = jnp.mean(dxh, axis=-1, keepdims=True)
    m2 = jnp.mean(dxh * xhat, axis=-1, keepdims=True)
    return rstd * (dxh - m1 - xhat * m2)


def _ln_fwd(name, pre, g, b):
    tp, d = pre.shape
    tr = _tile(tp, TR_LN, 16)

    def body(x_ref, g_ref, b_ref, y_ref, yb_ref):
        xhat, _ = _ln_stats(x_ref[...])
        y = xhat * g_ref[...] + b_ref[...]
        y_ref[...] = y
        yb_ref[...] = y.astype(BF16)

    row = pl.BlockSpec((tr, d), lambda i: (i, 0))
    vec = pl.BlockSpec((1, d), lambda i: (0, 0))
    return pl.pallas_call(body, name=name, grid=(tp // tr,), in_specs=[row, vec, vec], out_specs=[row, row],
                          out_shape=[_sds((tp, d), F32), _sds((tp, d), BF16)],
                          compiler_params=_params(("parallel",)))(pre, g, b)


def _accumulate(i, ref, val):
    @pl.when(i == 0)
    def _():
        ref[...] = val

    @pl.when(i > 0)
    def _():
        ref[...] += val


def _ln_bwd(name, pre, dy, g):
    tp, d = pre.shape
    tr = _tile(tp, TR_LN, 16)

    def body(x_ref, dy_ref, g_ref, dx_ref, dxb_ref, dg_ref, db_ref):
        i = pl.program_id(0)
        xhat, rstd = _ln_stats(x_ref[...])
        dy = dy_ref[...]
        dx = _ln_bwd_rows(dy, xhat, rstd, g_ref[...])
        dx_ref[...] = dx
        dxb_ref[...] = dx.astype(BF16)
        _accumulate(i, dg_ref, jnp.sum(dy * xhat, axis=0, keepdims=True))
        _accumulate(i, db_ref, jnp.sum(dy, axis=0, keepdims=True))

    row = pl.BlockSpec((tr, d), lambda i: (i, 0))
    vec = pl.BlockSpec((1, d), lambda i: (0, 0))
    return pl.pallas_call(body, name=name, grid=(tp // tr,), in_specs=[row, row, vec], out_specs=[row, row, vec, vec],
                          out_shape=[_sds((tp, d), F32), _sds((tp, d), BF16), _sds((1, d), F32), _sds((1, d), F32)],
                          compiler_params=_params(("arbitrary",)))(pre, dy, g)


def _ln_loss_bwd(name, pre, tgt, g, b, n_rows, seq_len):
    tp, d = pre.shape
    tr = _tile(tp, TR_LN, 16)
    n_seq = n_rows // seq_len

    def body(x_ref, t_ref, g_ref, b_ref, dx_ref, dxb_ref, dg_ref, db_ref, sq_ref):
        i = pl.program_id(0)
        xhat, rstd = _ln_stats(x_ref[...])
        gain = g_ref[...]
        y = xhat * gain + b_ref[...]
        r = i * tr + lax.broadcasted_iota(jnp.int32, (tr, 1), 0)
        pos = r
        for s in range(1, n_seq):
            pos = jnp.where(r >= s * seq_len, r - s * seq_len, pos)
        live = jnp.logical_and(r < n_rows, pos >= N_META)
        err = jnp.where(live, y - t_ref[...], 0.0)
        dy = err * (1.0 / d)
        dx = _ln_bwd_rows(dy, xhat, rstd, gain)
        dx_ref[...] = dx
        dxb_ref[...] = dx.astype(BF16)
        _accumulate(i, dg_ref, jnp.sum(dy * xhat, axis=0, keepdims=True))
        _accumulate(i, db_ref, jnp.sum(dy, axis=0, keepdims=True))
        _accumulate(i, sq_ref, jnp.sum(err * err, axis=0, keepdims=True))

    row = pl.BlockSpec((tr, d), lambda i: (i, 0))
    vec = pl.BlockSpec((1, d), lambda i: (0, 0))
    return pl.pallas_call(
        body, name=name, grid=(tp // tr,), in_specs=[row, row, vec, vec], out_specs=[row, row, vec, vec, vec],
        out_shape=[_sds((tp, d), F32), _sds((tp, d), BF16), _sds((1, d), F32), _sds((1, d), F32), _sds((1, d), F32)],
        compiler_params=_params(("arbitrary",)))(pre, tgt, g, b)


def _proj_in(name, h_b, win_all):
    tp, d = h_b.shape
    ns, _, ni = win_all.shape
    tm, tk = _tile(tp, TM_BIG, 16), _tile(d, 4 * TK, LANE)

    def epi(acc, ex, outs, rows):
        outs[0][rows, :] = acc

    return _mm(name, (tp // tm, ns, d // tk), NN, (h_b, win_all),
               [pl.BlockSpec((tm, tk), lambda m, j, k: (m, k)),
                pl.BlockSpec((None, tk, ni), lambda m, j, k: (j, k, 0))],
               (), [], [_sds((tp, ns * ni), F32)], [pl.BlockSpec((tm, ni), lambda m, j, k: (m, j))], (tm, ni), epi)[0]


def _positions(tp, n_rows, seq_len):
    r = lax.broadcasted_iota(jnp.int32, (tp, 1), 0)
    pos = r
    for s in range(1, n_rows // seq_len):
        pos = jnp.where(r >= s * seq_len, r - s * seq_len, pos)
    return pos


def _shift_down(x, s, pos):
    return jnp.where(pos >= s, pltpu.roll(x, s, 0), 0.0)


def _shift_up(x, s, pos, seq_len):
    return jnp.where(pos + s < seq_len, pltpu.roll(x, x.shape[0] - s, 0), 0.0)


def _conv_fwd(name, u, conv_w, n_rows, seq_len):
    tp = u.shape[0]
    c = conv_w.shape[1]
    tc = _tile(c, TC_MIX, LANE)
    nb = c // tc

    def body(gb_ref, gc_ref, xi_ref, w_ref, y_ref):
        pos = _positions(tp, n_rows, seq_len)
        v = gc_ref[...] * xi_ref[...]
        w = w_ref[...]
        y = _shift_down(v, 2, pos) * w[0:1]
        y = y + _shift_down(v, 1, pos) * w[1:2]
        y = y + v * w[2:3]
        y_ref[...] = (gb_ref[...] * y).astype(BF16)

    col = lambda off: pl.BlockSpec((tp, tc), lambda i: (0, off + i))
    return pl.pallas_call(body, name=name, grid=(nb,), in_specs=[col(0), col(nb), col(2 * nb), pl.BlockSpec((CONV_K, tc), lambda i: (0, i))],
                          out_specs=pl.BlockSpec((tp, tc), lambda i: (0, i)), out_shape=_sds((tp, c), BF16),
                          compiler_params=_params(("parallel",)))(u, u, u, conv_w)


def _conv_bwd(name, u, conv_w, dy, n_rows, seq_len):
    tp = u.shape[0]
    c = conv_w.shape[1]
    tc = _tile(c, TC_MIX, LANE)
    nb = c // tc

    def body(gb_ref, gc_ref, xi_ref, w_ref, dy_ref, dgb_ref, dgc_ref, dxi_ref, dw_ref):
        pos = _positions(tp, n_rows, seq_len)
        gc, xi = gc_ref[...], xi_ref[...]
        v = gc * xi
        w = w_ref[...]
        v2, v1 = _shift_down(v, 2, pos), _shift_down(v, 1, pos)
        conv = v2 * w[0:1]
        conv = conv + v1 * w[1:2]
        conv = conv + v * w[2:3]
        dyc = dy_ref[...]
        dgb_ref[...] = (dyc * conv).astype(BF16)
        dconv = dyc * gb_ref[...]
        dv = dconv * w[2:3] + _shift_up(dconv, 1, pos, seq_len) * w[1:2] + _shift_up(dconv, 2, pos, seq_len) * w[0:1]
        dgc_ref[...] = (dv * xi).astype(BF16)
        dxi_ref[...] = (dv * gc).astype(BF16)
        dw_ref[0:1, :] = jnp.sum(dconv * v2, axis=0, keepdims=True)
        dw_ref[1:2, :] = jnp.sum(dconv * v1, axis=0, keepdims=True)
        dw_ref[2:3, :] = jnp.sum(dconv * v, axis=0, keepdims=True)

    col = lambda off: pl.BlockSpec((tp, tc), lambda i: (0, off + i))
    wspec = pl.BlockSpec((CONV_K, tc), lambda i: (0, i))
    return pl.pallas_call(body, name=name, grid=(nb,), in_specs=[col(0), col(nb), col(2 * nb), wspec, col(0)],
                          out_specs=[col(0), col(0), col(0), wspec],
                          out_shape=[_sds((tp, c), BF16)] * 3 + [_sds((CONV_K, c), F32)],
                          compiler_params=_params(("parallel",)))(u, u, u, conv_w, dy)


def _window_select(group, parts):
    out = parts[-1]
    for gi in range(len(parts) - 2, -1, -1):
        out = jnp.where(group == gi, parts[gi], out)
    return out


def _pool_fwd(name, u, col0, p, pg, n_rows, seq_len):
    tp = u.shape[0]
    tc = _tile(pg, TC_MIX, LANE)
    per_group = pg // tc

    def body(z_ref, d_ref):
        group = pl.program_id(0) // per_group
        pos = _positions(tp, n_rows, seq_len)
        z = z_ref[...]
        sums, s, w = [], z, 1
        for _ in POOL_WINDOWS:
            s = s + _shift_down(s, w, pos)
            w *= 2
            sums.append(s)
        total = _window_select(group, sums)
        count = jnp.minimum(pos + 1, 2 << group).astype(F32)
        d_ref[...] = (total / count - z).astype(BF16)

    return pl.pallas_call(body, name=name, grid=(p // tc,), in_specs=[pl.BlockSpec((tp, tc), lambda i: (0, col0 // tc + i))],
                          out_specs=pl.BlockSpec((tp, tc), lambda i: (0, i)), out_shape=_sds((tp, p), BF16),
                          compiler_params=_params(("parallel",)))(u)


def _pool_bwd(name, dd, pg, n_rows, seq_len):
    tp, p = dd.shape
    tc = _tile(pg, TC_MIX, LANE)
    per_group = pg // tc

    def body(dd_ref, dz_ref):
        group = pl.program_id(0) // per_group
        pos = _positions(tp, n_rows, seq_len)
        dd_v = dd_ref[...]
        count = jnp.minimum(pos + 1, 2 << group).astype(F32)
        sums, s, w = [], dd_v / count, 1
        for _ in POOL_WINDOWS:
            s = s + _shift_up(s, w, pos, seq_len)
            w *= 2
            sums.append(s)
        dz_ref[...] = (_window_select(group, sums) - dd_v).astype(BF16)

    spec = pl.BlockSpec((tp, tc), lambda i: (0, i))
    return pl.pallas_call(body, name=name, grid=(p // tc,), in_specs=[spec], out_specs=spec, out_shape=_sds((tp, p), BF16),
                          compiler_params=_params(("parallel",)))(dd)


def _pool_mix(name, dpool, pool_w_b, scale):
    tp, p = dpool.shape
    ng, pg, _ = pool_w_b.shape
    tm = _tile(tp, TM_BIG, 16)

    def epi(acc, ex, outs, rows):
        outs[0][rows, :] = acc
        outs[1][rows, :] = (acc * ex[0][...]).astype(BF16)

    blk = pl.BlockSpec((tm, pg), lambda m, g, k: (m, g))
    return _mm(name, (tp // tm, ng, 1), NN, (dpool, pool_w_b), [blk, pl.BlockSpec((None, pg, pg), lambda m, g, k: (g, 0, 0))],
               (scale,), [pl.BlockSpec((1, pg), lambda m, g, k: (0, g))],
               [_sds((tp, p), F32), _sds((tp, p), BF16)], [blk, blk], None, epi)


def _pool_scale_bwd(name, dy, col_block, ypre, scale):
    tp, p = ypre.shape
    tr = _tile(tp, TR_LN, 16)

    def body(dy_ref, yp_ref, s_ref, o_ref, ds_ref):
        i = pl.program_id(0)
        dyp = dy_ref[...]
        o_ref[...] = (dyp * s_ref[...]).astype(BF16)
        _accumulate(i, ds_ref, jnp.sum(dyp * yp_ref[...], axis=0, keepdims=True))

    row = pl.BlockSpec((tr, p), lambda i: (i, 0))
    vec = pl.BlockSpec((1, p), lambda i: (0, 0))
    return pl.pallas_call(body, name=name, grid=(tp // tr,), in_specs=[pl.BlockSpec((tr, p), lambda i: (i, col_block)), row, vec],
                          out_specs=[row, vec], out_shape=[_sds((tp, p), BF16), _sds((1, p), F32)],
                          compiler_params=_params(("arbitrary",)))(dy, ypre, scale)


def _pool_mix_bwd_in(name, dyps, pool_w_b):
    tp, p = dyps.shape
    ng, pg, _ = pool_w_b.shape
    tm = _tile(tp, TM_BIG, 16)

    def epi(acc, ex, outs, rows):
        outs[0][rows, :] = acc

    blk = pl.BlockSpec((tm, pg), lambda m, g, k: (m, g))
    return _mm(name, (tp // tm, ng, 1), NT, (dyps, pool_w_b), [blk, pl.BlockSpec((None, pg, pg), lambda m, g, k: (g, 0, 0))],
               (), [], [_sds((tp, p), F32)], [blk], None, epi)[0]


def _pool_mix_bwd_w(name, dpool, dyps, pg):
    tp, p = dpool.shape
    ng = p // pg
    tkt = _tile(tp, TM_BIG, LANE)

    def epi(acc, ex, outs, rows):
        outs[0][rows, :] = acc

    blk = pl.BlockSpec((tkt, pg), lambda g, t: (t, g))
    return _mm(name, (ng, tp // tkt), TN_DIMS, (dpool, dyps), [blk, blk], (), [],
               [_sds((ng, pg, pg), F32)], [pl.BlockSpec((None, pg, pg), lambda g, t: (g, 0, 0))], (pg, pg), epi)[0]


def _proj_out(name, y, wout_all, h):
    tp, c = y.shape
    d = wout_all.shape[2]
    w = wout_all.reshape(c, d)
    tm, tn, tk = _tile(tp, TM_BIG, 16), _tile(d, TN, LANE), _tile(c, 4 * TK, LANE)

    def epi(acc, ex, outs, rows):
        outs[0][rows, :] = ALPHA * ex[0][rows, :] + acc

    mn = pl.BlockSpec((tm, tn), lambda m, n, k: (m, n))
    return _mm(name, (tp // tm, d // tn, c // tk), NN, (y, w),
               [pl.BlockSpec((tm, tk), lambda m, n, k: (m, k)), pl.BlockSpec((tk, tn), lambda m, n, k: (k, n))],
               (h,), [mn], [_sds((tp, d), F32)], [mn], (tm, tn), epi)[0]


def _proj_out_bwd_y(name, dp_b, wout_all, deps=()):
    tp, d = dp_b.shape
    ns, ro, _ = wout_all.shape
    tm, tk = _tile(tp, TM_BIG, 16), d

    def epi(acc, ex, outs, rows):
        outs[0][rows, :] = acc

    return _mm(name, (tp // tm, ns, d // tk), NT, (dp_b, wout_all),
               [pl.BlockSpec((tm, tk), lambda m, j, k: (m, k)), pl.BlockSpec((None, ro, tk), lambda m, j, k: (j, 0, k))],
               (), [], [_sds((tp, ns * ro), F32)], [pl.BlockSpec((tm, ro), lambda m, j, k: (m, j))], (tm, ro), epi, deps)[0]


def _proj_out_bwd_w(name, y, dp_b, ns):
    tp, c = y.shape
    d = dp_b.shape[1]
    ro = c // ns
    tkt, tn = _tile(tp, TM_BIG, LANE), _tile(d, 2 * TN, LANE)

    def epi(acc, ex, outs, rows):
        outs[0][rows, :] = acc.astype(BF16)

    return _mm(name, (ns, d // tn, tp // tkt), TN_DIMS, (y, dp_b),
               [pl.BlockSpec((tkt, ro), lambda j, n, t: (t, j)), pl.BlockSpec((tkt, tn), lambda j, n, t: (t, n))],
               (), [], [_sds((ns, ro, d), BF16)], [pl.BlockSpec((None, ro, tn), lambda j, n, t: (j, 0, n))], (ro, tn), epi)[0]


def _proj_in_bwd_w(name, h_b, du, ns, deps=()):
    tp, d = h_b.shape
    ni = du.shape[1] // ns
    tkt, tmd = _tile(tp, TK_TOKENS, LANE), _tile(d, 2 * TN, LANE)

    def epi(acc, ex, outs, rows):
        outs[0][rows, :] = acc.astype(BF16)

    return _mm(name, (ns, d // tmd, tp // tkt), TN_DIMS, (h_b, du),
               [pl.BlockSpec((tkt, tmd), lambda j, i, t: (t, i)), pl.BlockSpec((tkt, ni), lambda j, i, t: (t, j))],
               (), [], [_sds((ns, d, ni), BF16)], [pl.BlockSpec((None, tmd, ni), lambda j, i, t: (j, i, 0))], (tmd, ni), epi, deps)[0]


def _proj_in_bwd_h(name, du, win_all, dp, deps=()):
    tp = du.shape[0]
    ns, d, ni = win_all.shape
    tm, tn = _tile(tp, TM_BIG, 16), _tile(d, TN, LANE)

    def epi(acc, ex, outs, rows):
        outs[0][rows, :] = ALPHA * ex[0][rows, :] + acc

    mn = pl.BlockSpec((tm, tn), lambda m, n, j: (m, n))
    return _mm(name, (tp // tm, d // tn, ns), NT, (du, win_all),
               [pl.BlockSpec((tm, ni), lambda m, n, j: (m, j)), pl.BlockSpec((None, tn, ni), lambda m, n, j: (j, n, 0))],
               (dp,), [mn], [_sds((tp, d), F32)], [mn], (tm, tn), epi, deps)[0]


def _place():
    return lax.axis_index("x"), lax.axis_index("y"), lax.axis_index("c")


def _hbm(a):
    return pltpu.with_memory_space_constraint(a, pltpu.HBM)


def _token_shape():
    return _sds((8, LANE), F32)


def _gather_peers(x, y, c):
    return [(x, y, 1 - c), (1 - x, y, c), (x, 1 - y, c), (1 - x, 1 - y, c)]


def _into_slab(name, w, dev, dtype):
    r, c = w.shape
    tr = _tile(r, max(16, ELEM_BLOCK_BYTES // (4 * c)), 16 if dtype == BF16 else 8)

    def body(dev_ref, w_ref, o_ref):
        o_ref[...] = w_ref[...].astype(dtype)

    return pl.pallas_call(
        body, name=name,
        grid_spec=pltpu.PrefetchScalarGridSpec(
            num_scalar_prefetch=1, grid=(r // tr,), in_specs=[pl.BlockSpec((tr, c), lambda i, dev_ref: (i, 0))],
            out_specs=pl.BlockSpec((None, tr, c), lambda i, dev_ref: (dev_ref[0], i, 0))),
        out_shape=_sds((N_DEV, r, c), dtype), compiler_params=_params(("parallel",)),
    )(dev, w)


def _gather_start(name, lands, after=()):
    n = len(lands)
    n_peer = N_CHIP
    n_in = n + len(after)

    def body(*refs):
        land_refs = refs[:n]
        send_sems, recv_sems = refs[n_in:n_in + n], refs[n_in + n:n_in + 2 * n]
        token = refs[-1]
        x, y, c = _place()
        me = 4 * x + 2 * y + c
        for i in range(n):
            for k, peer in enumerate(_gather_peers(x, y, c)):
                pltpu.make_async_remote_copy(
                    src_ref=land_refs[i].at[me], dst_ref=land_refs[i].at[me], send_sem=send_sems[i].at[k],
                    recv_sem=recv_sems[i].at[k], device_id=peer, device_id_type=MESH).start()
        token[...] = jnp.zeros_like(token)

    sem = pltpu.SemaphoreType.DMA((n_peer,))
    out = pl.pallas_call(
        body, name=name,
        out_shape=[sem] * (2 * n) + [pltpu.HBM(l.shape, l.dtype) for l in lands] + [_token_shape()],
        in_specs=[HBM] * n + [ANY] * len(after),
        out_specs=[SEM] * (2 * n) + [HBM] * n + [pl.BlockSpec(memory_space=pltpu.VMEM)],
        input_output_aliases={i: 2 * n + i for i in range(n)},
        compiler_params=pltpu.CompilerParams(has_side_effects=EFFECT),
    )(*[_hbm(l) for l in lands], *after)
    per = [(out[i], out[n + i], out[2 * n + i]) for i in range(n)]
    return per, out[-1]


def _gather_wait(name, started, after):
    send_sems, recv_sems, land = started
    after = after if isinstance(after, tuple) else (after,)

    def body(land_ref, send_ref, recv_ref, *rest):
        x, y, c = _place()
        for k, (px, py, pc) in enumerate(_gather_peers(x, y, c)):
            cp = pltpu.make_async_remote_copy(
                src_ref=land_ref.at[4 * x + 2 * y + c], dst_ref=land_ref.at[4 * px + 2 * py + pc], send_sem=send_ref.at[k],
                recv_sem=recv_ref.at[k], device_id=(px, py, pc), device_id_type=MESH)
            cp.wait_send()
            cp.wait_recv()

    return pl.pallas_call(
        body, name=name, out_shape=pltpu.HBM(land.shape, land.dtype),
        in_specs=(HBM, SEM, SEM) + (ANY,) * len(after), out_specs=HBM, input_output_aliases={0: 0},
        compiler_params=pltpu.CompilerParams(has_side_effects=EFFECT),
    )(land, send_sems, recv_sems, *after)


def _gather_finish(name, land, deps=()):
    def body(land_ref, *rest):
        out_ref, send_sems, recv_sems = rest[len(deps):]
        x, y, c = _place()
        copies = []
        for k, (px, py) in enumerate([(1 - x, y), (x, 1 - y), (1 - x, 1 - y)]):
            slab = 4 * px + 2 * py + c
            copies.append(pltpu.make_async_remote_copy(
                src_ref=land_ref.at[slab], dst_ref=out_ref.at[slab], send_sem=send_sems.at[k], recv_sem=recv_sems.at[k],
                device_id=(x, y, 1 - c), device_id_type=MESH))
        for cp in copies:
            cp.start()
        for cp in copies:
            cp.wait()

    return pl.pallas_call(
        body, name=name, out_shape=_sds(land.shape, land.dtype), in_specs=[ANY] + [ANY] * len(deps), out_specs=ANY,
        input_output_aliases={0: 0},
        scratch_shapes=[pltpu.SemaphoreType.DMA((N_CHIP - 1,)), pltpu.SemaphoreType.DMA((N_CHIP - 1,))],
    )(land, *deps)


def _route_sibling(x, y, c):
    return [(2 * j + (1 - c), (x, y, 1 - c)) for j in range(N_CHIP)]


def _route_chips(x, y, c):
    return [(k, (px, py, c)) for k, (px, py) in enumerate([(1 - x, y), (x, 1 - y), (1 - x, 1 - y)])]


def _exchange_start(name, src, route):
    n_copy = len(route(0, 0, 0))

    def body(s_ref, land_ref, send_sems, recv_sems, s_out, land_out, token):
        for k, (slab, peer) in enumerate(route(*_place())):
            pltpu.make_async_remote_copy(
                src_ref=s_ref.at[slab], dst_ref=land_ref.at[k], send_sem=send_sems.at[k], recv_sem=recv_sems.at[k],
                device_id=peer, device_id_type=MESH).start()
        token[...] = jnp.zeros_like(token)

    land = lax.empty((n_copy,) + src.shape[1:], src.dtype)
    sem = pltpu.SemaphoreType.DMA((n_copy,))
    out = pl.pallas_call(
        body, name=name,
        out_shape=[sem, sem, pltpu.HBM(src.shape, src.dtype), pltpu.HBM(land.shape, land.dtype), _token_shape()],
        in_specs=[HBM, HBM], out_specs=[SEM, SEM, HBM, HBM, pl.BlockSpec(memory_space=pltpu.VMEM)],
        input_output_aliases={0: 2, 1: 3}, compiler_params=pltpu.CompilerParams(has_side_effects=EFFECT),
    )(_hbm(src), _hbm(land))
    return out[:4], out[4]


def _exchange_wait(name, started, route, after):
    send_sems, recv_sems, src, land = started

    def body(s_ref, land_ref, send_ref, recv_ref, after_ref, s_out, land_out):
        for k, (slab, peer) in enumerate(route(*_place())):
            cp = pltpu.make_async_remote_copy(
                src_ref=s_ref.at[slab], dst_ref=land_ref.at[k], send_sem=send_ref.at[k], recv_sem=recv_ref.at[k],
                device_id=peer, device_id_type=MESH)
            cp.wait_send()
            cp.wait_recv()

    return pl.pallas_call(
        body, name=name, out_shape=(pltpu.HBM(src.shape, src.dtype), pltpu.HBM(land.shape, land.dtype)),
        in_specs=(HBM, HBM, SEM, SEM, ANY), out_specs=(HBM, HBM), input_output_aliases={0: 0, 1: 1},
        compiler_params=pltpu.CompilerParams(has_side_effects=EFFECT),
    )(src, land, send_sems, recv_sems, after)


def _all_gather(name, land):
    def body(land_ref, out_ref, send_sems, recv_sems):
        x, y, c = _place()
        me, sibling = (x, y, c), (x, y, 1 - c)
        chips = [(1 - x, y), (x, 1 - y), (1 - x, 1 - y)]

        def copy(k, block, to):
            slab = 4 * block[0] + 2 * block[1] + block[2]
            return pltpu.make_async_remote_copy(
                src_ref=land_ref.at[slab], dst_ref=out_ref.at[slab],
                send_sem=send_sems.at[k], recv_sem=recv_sems.at[k], device_id=to, device_id_type=MESH)

        first = [copy(0, me, sibling)] + [copy(1 + j, me, (*chip, c)) for j, chip in enumerate(chips)]
        for cp in first:
            cp.start()
        passed = [copy(4 + j, (*chip, c), sibling) for j, chip in enumerate(chips)]
        for j, chip in enumerate(chips):
            copy(1 + j, (*chip, c), me).wait_recv()
            passed[j].start()
        copy(0, sibling, me).wait_recv()
        for j, chip in enumerate(chips):
            copy(4 + j, (*chip, 1 - c), me).wait_recv()
        for cp in first + passed:
            cp.wait_send()

    return pl.pallas_call(
        body, name=name, out_shape=_sds(land.shape, land.dtype), in_specs=[ANY], out_specs=ANY, input_output_aliases={0: 0},
        scratch_shapes=[pltpu.SemaphoreType.DMA((N_DEV - 1,)), pltpu.SemaphoreType.DMA((N_DEV - 1,))],
    )(land)


def _add_sibling(name, part, recv, place):
    _, r, c = part.shape
    tr = _tile(r, max(16, 2 * ELEM_BLOCK_BYTES // (2 * c)), 16)
    n_out = N_CHIP - 1

    def chip_of(k, x_ref, y_ref):
        px = jnp.where(k == 1, x_ref[0], 1 - x_ref[0])
        py = jnp.where(k == 0, y_ref[0], 1 - y_ref[0])
        return 2 * px + py

    def body(x_ref, y_ref, c_ref, p_ref, r_ref, o_ref):
        o_ref[...] = (p_ref[...].astype(F32) + r_ref[...].astype(F32)).astype(BF16)

    return pl.pallas_call(
        body, name=name,
        grid_spec=pltpu.PrefetchScalarGridSpec(
            num_scalar_prefetch=3, grid=(n_out, r // tr),
            in_specs=[pl.BlockSpec((None, tr, c), lambda k, i, x_ref, y_ref, c_ref: (2 * chip_of(k, x_ref, y_ref) + c_ref[0], i, 0)),
                      pl.BlockSpec((None, tr, c), lambda k, i, x_ref, y_ref, c_ref: (chip_of(k, x_ref, y_ref), i, 0))],
            out_specs=pl.BlockSpec((None, tr, c), lambda k, i, x_ref, y_ref, c_ref: (k, i, 0))),
        out_shape=_sds((n_out, r, c), BF16), compiler_params=_params(("parallel", "parallel")),
    )(*place, part, recv)


def _adamw_math(w, g, m, v):
    m = ADAM_B1 * m + (1.0 - ADAM_B1) * g
    v = ADAM_B2 * v + (1.0 - ADAM_B2) * (g * g)
    m_hat = m / (1.0 - ADAM_B1 ** ADAM_STEP)
    v_hat = v / (1.0 - ADAM_B2 ** ADAM_STEP)
    delta = -ADAM_LR * (m_hat / (jnp.sqrt(v_hat) + ADAM_EPS) + ADAM_WD * w)
    return delta, m, v


def _adamw_big(name, part, from_sibling, from_chips, dev, chip, w, m, v):
    r, c = w.shape
    tr = _tile(r, max(16, ELEM_BLOCK_BYTES // (4 * c)), 16)

    def body(dev_ref, chip_ref, p_ref, s_ref, r_ref, w_ref, m_ref, v_ref, g_out, d_out, m_out, v_out):
        g = p_ref[...].astype(F32) + s_ref[...].astype(F32)
        for k in range(N_CHIP - 1):
            g = g + r_ref[k].astype(F32)
        delta, m_new, v_new = _adamw_math(w_ref[...], g, m_ref[...], v_ref[...])
        g_out[...] = g
        d_out[...] = delta
        m_out[...] = m_new
        v_out[...] = v_new

    blk = pl.BlockSpec((tr, c), lambda i, dev_ref, chip_ref: (i, 0))
    return pl.pallas_call(
        body, name=name,
        grid_spec=pltpu.PrefetchScalarGridSpec(
            num_scalar_prefetch=2, grid=(r // tr,),
            in_specs=[pl.BlockSpec((None, tr, c), lambda i, dev_ref, chip_ref: (dev_ref[0], i, 0)),
                      pl.BlockSpec((None, tr, c), lambda i, dev_ref, chip_ref: (chip_ref[0], i, 0)),
                      pl.BlockSpec((N_CHIP - 1, tr, c), lambda i, dev_ref, chip_ref: (0, i, 0)), blk, blk, blk],
            out_specs=[blk, blk, blk, blk]),
        out_shape=[_sds((r, c), F32)] * 4, compiler_params=_params(("parallel",)),
    )(dev, chip, part, from_sibling, from_chips, w, m, v)


def _sum_parts(name, parts):
    n, r, c = parts.shape
    tr = _tile(r, max(8, ELEM_BLOCK_BYTES // (4 * c)), 8)

    def body(p_ref, o_ref):
        acc = p_ref[0]
        for k in range(1, n):
            acc = acc + p_ref[k]
        o_ref[...] = acc

    return pl.pallas_call(body, name=name, grid=(r // tr,), in_specs=[pl.BlockSpec((n, tr, c), lambda i: (0, i, 0))],
                          out_specs=pl.BlockSpec((tr, c), lambda i: (i, 0)), out_shape=_sds((r, c), F32),
                          compiler_params=_params(("parallel",)))(parts)


def _adamw_small(name, ws, gs, ms, vs):
    n = len(ws)

    def body(*refs):
        ins, outs = refs[:4 * n], refs[4 * n:]
        for i in range(n):
            delta, m_new, v_new = _adamw_math(ins[i][...], ins[n + i][...], ins[2 * n + i][...], ins[3 * n + i][...])
            outs[i][...] = delta
            outs[n + i][...] = m_new
            outs[2 * n + i][...] = v_new

    shapes = [_sds(w.shape, F32) for w in ws]
    return pl.pallas_call(body, name=name, out_shape=shapes * 3)(*ws, *gs, *ms, *vs)


def _pad_rows(a, rows):
    return jnp.pad(a, ((0, rows - a.shape[0]), (0, 0)))


def kernel(x, meta_tokens, ffn1_w_gu, ffn1_w_down, ln1_g, ln1_b, w_in, conv_w, pool_w, pool_scale, w_out, ln2_g, ln2_b, ffn2_w_gu, ffn2_w_down, ln3_g, ln3_b, loss_target, m_meta_tokens, m_ffn1_w_gu, m_ffn1_w_down, m_ln1_g, m_ln1_b, m_w_in, m_conv_w, m_pool_w, m_pool_scale, m_w_out, m_ln2_g, m_ln2_b, m_ffn2_w_gu, m_ffn2_w_down, m_ln3_g, m_ln3_b, v_meta_tokens, v_ffn1_w_gu, v_ffn1_w_down, v_ln1_g, v_ln1_b, v_w_in, v_conv_w, v_pool_w, v_pool_scale, v_w_out, v_ln2_g, v_ln2_b, v_ffn2_w_gu, v_ffn2_w_down, v_ln3_g, v_ln3_b):
    names = ["meta_tokens", "ffn1_w_gu", "ffn1_w_down", "ln1_g", "ln1_b", "w_in", "conv_w", "pool_w", "pool_scale", "w_out",
             "ln2_g", "ln2_b", "ffn2_w_gu", "ffn2_w_down", "ln3_g", "ln3_b"]
    w_of = dict(zip(names, [meta_tokens, ffn1_w_gu, ffn1_w_down, ln1_g, ln1_b, w_in, conv_w, pool_w, pool_scale, w_out,
                            ln2_g, ln2_b, ffn2_w_gu, ffn2_w_down, ln3_g, ln3_b]))
    m_of = dict(zip(names, [m_meta_tokens, m_ffn1_w_gu, m_ffn1_w_down, m_ln1_g, m_ln1_b, m_w_in, m_conv_w, m_pool_w, m_pool_scale,
                            m_w_out, m_ln2_g, m_ln2_b, m_ffn2_w_gu, m_ffn2_w_down, m_ln3_g, m_ln3_b]))
    v_of = dict(zip(names, [v_meta_tokens, v_ffn1_w_gu, v_ffn1_w_down, v_ln1_g, v_ln1_b, v_w_in, v_conv_w, v_pool_w, v_pool_scale,
                            v_w_out, v_ln2_g, v_ln2_b, v_ffn2_w_gu, v_ffn2_w_down, v_ln3_g, v_ln3_b]))

    n_seq, seq, d = x.shape
    seq_len = seq + N_META
    n_rows = n_seq * seq_len
    tp = -(-n_rows // ROW_ALIGN) * ROW_ALIGN
    c_conv = conv_w.shape[2] * N_DEV
    p_pool = pool_scale.shape[1]
    pg = pool_w.shape[3]
    assert c_conv == p_pool and p_pool == N_POOL_GROUPS * pg and POOL_WINDOWS == tuple(2 << g for g in range(N_POOL_GROUPS))
    assert (N_POOL_GROUPS * pg * pg) % d == 0 and pg % LANE == 0

    xi, yi, ci = _place()
    dev_index = 4 * xi + 2 * yi + ci
    dev = jnp.reshape(dev_index, (1,)).astype(jnp.int32)
    chip = jnp.reshape(2 * xi + yi, (1,)).astype(jnp.int32)
    place = tuple(jnp.reshape(a, (1,)).astype(jnp.int32) for a in (xi, yi, ci))

    big = ["ffn1_w_gu", "ffn1_w_down", "w_in", "w_out", "ffn2_w_gu", "ffn2_w_down"]
    wcol = d // N_DEV
    conv_rows = 8
    small_local = jnp.concatenate([
        meta_tokens,
        pool_w[0].reshape(N_POOL_GROUPS * (pg // N_DEV), pg),
        jnp.pad(conv_w[0], ((0, conv_rows - CONV_K), (0, wcol - conv_w.shape[2]))),
    ], axis=0)
    lands = {n: _into_slab(f"slab_{n}", w_of[n][0], dev, BF16) for n in big}
    lands["small"] = _into_slab("slab_small", small_local, dev, F32)
    started = {}

    def start(tag, which, after=()):
        per, token = _gather_start(f"ag_start_{tag}", [lands[n] for n in which], after)
        started.update(zip(which, per))
        return token

    def gathered(n, after, then_start=()):
        land = _gather_wait(f"ag_wait_{n}", started[n], after)
        deps = (start(f"after_{n}", then_start, (land,)),) if then_start else ()
        return _gather_finish(f"ag_finish_{n}", land, deps)

    gather_token = start("first", ["small", "ffn1_w_gu"])
    small_all = gathered("small", gather_token)
    r0, r1 = N_META, N_META + N_POOL_GROUPS * (pg // N_DEV)
    meta_full = jnp.transpose(small_all[:, :r0], (1, 0, 2)).reshape(N_META, d)
    pool_w_full = jnp.transpose(small_all[:, r0:r1].reshape(N_DEV, N_POOL_GROUPS, pg // N_DEV, pg), (1, 0, 2, 3)).reshape(N_POOL_GROUPS, pg, pg)
    conv_w_full = jnp.transpose(small_all[:, r1:r1 + CONV_K, :conv_w.shape[2]], (1, 0, 2)).reshape(CONV_K, c_conv)
    pool_w_b = pool_w_full.astype(BF16)

    h0 = jnp.concatenate([jnp.broadcast_to(meta_full[None], (n_seq, N_META, d)), x], axis=1).reshape(n_rows, d)
    h0 = _pad_rows(h0, tp)
    h0_b = h0.astype(BF16)
    tgt = _pad_rows(jnp.pad(loss_target, ((0, 0), (N_META, 0), (0, 0))).reshape(n_rows, d), tp)

    early = (h0_b, tgt) + tuple(lands[n] for n in big[1:])
    early += tuple(a[n][0] for n in ("ffn1_w_gu", "ffn2_w_gu") for a in (m_of, v_of))
    wgu1 = gathered("ffn1_w_gu", early, ["ffn1_w_down", "w_in"])
    gu1, act1 = _ffn_gu("ffn1_gu", h0_b, wgu1)
    wd1 = gathered("ffn1_w_down", act1, ["w_out", "ffn2_w_gu"]).reshape(N_CHIP, -1, d)
    pre1 = _ffn_down("ffn1_down", act1, wd1, h0)
    win_all = gathered("w_in", pre1)
    h1, h1_b = _ln_fwd("ln1", pre1, ln1_g, ln1_b)

    u = _proj_in("mix_in", h1_b, win_all)
    wout_all = gathered("w_out", u)
    y_conv = _conv_fwd("mix_conv", u, conv_w_full, n_rows, seq_len)
    dpool = _pool_fwd("mix_pool", u, 3 * c_conv, p_pool, pg, n_rows, seq_len)
    ypre, y_pool = _pool_mix("mix_pool_w", dpool, pool_w_b, pool_scale)
    y_mix = jnp.concatenate([y_conv, y_pool], axis=1)
    pre2 = _proj_out("mix_out", y_mix, wout_all, h1)
    wgu2 = gathered("ffn2_w_gu", pre2, ["ffn2_w_down"])
    h2, h2_b = _ln_fwd("ln2", pre2, ln2_g, ln2_b)

    gu2, act2 = _ffn_gu("ffn2_gu", h2_b, wgu2)
    wd2 = gathered("ffn2_w_down", act2).reshape(N_CHIP, -1, d)
    pre3 = _ffn_down("ffn2_down", act2, wd2, h2)

    dpre3, dpre3_b, d_ln3_g, d_ln3_b, sq = _ln_loss_bwd("ln3_loss", pre3, tgt, ln3_g, ln3_b, n_rows, seq_len)
    loss = lax.psum(0.5 * jnp.sum(sq) / d, ("x", "y", "c"))
    in_sibling, reducing = [], {}

    def exchange(after, new=None):
        tokens = []
        while in_sibling:
            n, started = in_sibling.pop(0)
            part, from_sibling = _exchange_wait(f"rs_sibling_wait_{n}", started, _route_sibling, after)
            summed = _add_sibling(f"rs_add_{n}", part, from_sibling, place)
            started, token = _exchange_start(f"rs_chips_start_{n}", summed, _route_chips)
            reducing[n] = (part, from_sibling, started)
            tokens.append(token)
        if new is not None:
            started, token = _exchange_start(f"rs_sibling_start_{new[0]}", new[1], _route_sibling)
            in_sibling.append((new[0], started))
            tokens.append(token)
        return tuple(tokens)

    dgu2 = _ffn_bwd_dgu("ffn2_bwd_dgu", dpre3_b, wd2, gu2)
    g_wd2 = _ffn_bwd_wd("ffn2_bwd_wd", act2, dpre3_b)
    tokens = exchange(g_wd2, ("ffn2_w_down", g_wd2.reshape(N_DEV, -1, d)))
    g_wgu2 = _ffn_bwd_wgu("ffn2_bwd_wgu", h2_b, dgu2, tokens)
    tokens = exchange(g_wgu2, ("ffn2_w_gu", g_wgu2))
    dh2 = _ffn_bwd_dh("ffn2_bwd_dh", dgu2, wgu2, dpre3, tokens)
    tokens = exchange(dh2)
    dpre2, dpre2_b, d_ln2_g, d_ln2_b = _ln_bwd("ln2_bwd", pre2, dh2, ln2_g)

    dy_mix = _proj_out_bwd_y("mix_out_bwd_y", dpre2_b, wout_all, tokens)
    g_wout = _proj_out_bwd_w("mix_out_bwd_w", y_mix, dpre2_b, N_DEV)
    tokens = exchange(g_wout, ("w_out", g_wout))
    dyps, d_pool_scale = _pool_scale_bwd("mix_pool_scale_bwd", dy_mix, 1, ypre, pool_scale)
    dd = _pool_mix_bwd_in("mix_pool_w_bwd_in", dyps, pool_w_b)
    d_pool_w = _pool_mix_bwd_w("mix_pool_w_bwd_w", dpool, dyps, pg)
    du_pool = _pool_bwd("mix_pool_bwd", dd, pg, n_rows, seq_len)
    du_b, du_c, du_x, d_conv_w = _conv_bwd("mix_conv_bwd", u, conv_w_full, dy_mix, n_rows, seq_len)
    du = jnp.concatenate([du_b, du_c, du_x, du_pool], axis=1)
    g_win = _proj_in_bwd_w("mix_in_bwd_w", h1_b, du, N_DEV, tokens)
    tokens = exchange(g_win, ("w_in", g_win))
    dh1 = _proj_in_bwd_h("mix_in_bwd_h", du, win_all, dpre2, tokens)
    tokens = exchange(dh1)
    dpre1, dpre1_b, d_ln1_g, d_ln1_b = _ln_bwd("ln1_bwd", pre1, dh1, ln1_g)

    def widen(a):
        return jnp.pad(a, ((0, 0), (0, d - a.shape[1])))

    small_part = jnp.concatenate([
        d_ln1_g, d_ln1_b, d_ln2_g, d_ln2_b, d_ln3_g, d_ln3_b, widen(d_pool_scale), widen(d_conv_w), d_pool_w.reshape(-1, d)], axis=0)
    n_small_rows = small_part.shape[0]
    small_part = _pad_rows(small_part, -(-n_small_rows // 8) * 8)
    small_started, token = _gather_start("ag_start_small_grads", [_into_slab("slab_small_grads", small_part, dev, F32)])

    dgu1 = _ffn_bwd_dgu("ffn1_bwd_dgu", dpre1_b, wd1, gu1, tokens + (token,))
    g_wgu1 = _ffn_bwd_wgu("ffn1_bwd_wgu", h0_b, dgu1)
    tokens = exchange(g_wgu1, ("ffn1_w_gu", g_wgu1))
    g_wd1 = _ffn_bwd_wd("ffn1_bwd_wd", act1, dpre1_b, tokens)
    tokens = exchange(g_wd1, ("ffn1_w_down", g_wd1.reshape(N_DEV, -1, d)))
    dh0 = _ffn_bwd_dh("ffn1_bwd_dh", dgu1, wgu1, dpre1, tokens)
    tokens = exchange(dh0)

    dh0_seq = dh0[:n_rows].reshape(n_seq, seq_len, d)
    grad_x = dh0_seq[:, N_META:]
    d_meta = jnp.sum(dh0_seq[:, :N_META], axis=0)

    grads, deltas, new_m, new_v = {}, {}, {}, {}
    after = tokens[0]
    for n in ["ffn2_w_down", "ffn2_w_gu", "w_out", "w_in", "ffn1_w_gu", "ffn1_w_down"]:
        part, from_sibling, started = reducing[n]
        _, from_chips = _exchange_wait(f"rs_chips_wait_{n}", started, _route_chips, after)
        g, dl, mm, vv = _adamw_big(f"adamw_{n}", part, from_sibling, from_chips, dev, chip, w_of[n][0], m_of[n][0], v_of[n][0])
        grads[n], deltas[n], new_m[n], new_v[n] = g[None], dl[None], mm[None], vv[None]
        after = g

    small_sum = _sum_parts("small_sum", _gather_finish("ag_finish_small_grads", _gather_wait("ag_wait_small_grads", small_started[0], dh0)))
    meta_sum = _sum_parts("meta_sum", _all_gather("ag_meta_grads", _into_slab("slab_meta_grads", d_meta, dev, F32)))
    o = 7 + CONV_K
    g_small = {
        "ln1_g": small_sum[0:1], "ln1_b": small_sum[1:2], "ln2_g": small_sum[2:3], "ln2_b": small_sum[3:4],
        "ln3_g": small_sum[4:5], "ln3_b": small_sum[5:6], "pool_scale": small_sum[6:7, :p_pool],
        "conv_w": lax.dynamic_slice_in_dim(small_sum[7:o, :c_conv], dev_index * (c_conv // N_DEV), c_conv // N_DEV, axis=1)[None],
        "meta_tokens": lax.dynamic_slice_in_dim(meta_sum, dev_index * wcol, wcol, axis=1),
        "pool_w": lax.dynamic_slice_in_dim(small_sum[o:n_small_rows].reshape(N_POOL_GROUPS, pg, pg),
                                           dev_index * (pg // N_DEV), pg // N_DEV, axis=1)[None],
    }
    small = ["meta_tokens", "ln1_g", "ln1_b", "conv_w", "pool_w", "pool_scale", "ln2_g", "ln2_b", "ln3_g", "ln3_b"]

    def flat(a):
        return a.reshape(-1, a.shape[-1])

    outs = _adamw_small("adamw_small", [flat(w_of[n]) for n in small], [flat(g_small[n]) for n in small],
                        [flat(m_of[n]) for n in small], [flat(v_of[n]) for n in small])
    ns = len(small)
    for i, n in enumerate(small):
        shape = w_of[n].shape
        grads[n] = g_small[n].reshape(shape)
        deltas[n], new_m[n], new_v[n] = outs[i].reshape(shape), outs[ns + i].reshape(shape), outs[2 * ns + i].reshape(shape)

    return (loss, grad_x, *[grads[n] for n in names], *[deltas[n] for n in names],
            *[new_m[n] for n in names], *[new_v[n] for n in names])
```

```python
import functools

import jax
import jax.numpy as jnp
from jax import lax
from jax.experimental import pallas as pl
from jax.experimental.pallas import tpu as pltpu

N_DEV = 8
N_CHIP = 4
N_META = 16
CONV_K = 3
POOL_WINDOWS = (2, 4, 8, 16)
N_POOL_GROUPS = len(POOL_WINDOWS)
LN_EPS = 1e-5
DEPTH = 1
ALPHA = (2.0 * DEPTH) ** 0.25
ADAM_LR = 0.001
ADAM_B1 = 0.9
ADAM_B2 = 0.999
ADAM_EPS = 1e-08
ADAM_WD = 0.01
ADAM_STEP = 10

V7X_VMEM_BYTES = 64 * 1024 * 1024
VMEM_LIMIT = V7X_VMEM_BYTES - 6 * 1024 * 1024
LANE = 128
ROW_ALIGN = 3 * LANE
TM_BIG = 1408
TM_WIDE = 704
TM_GU = 528
TK = 512
TK_TOKENS = 1408
TN = 1024
TR_LN = 128
ELEM_BLOCK_BYTES = 1 << 21
TC_MIX = LANE
EPILOGUE_ROWS = 64

NN = (((1,), (0,)), ((), ()))
NT = (((1,), (1,)), ((), ()))
TN_DIMS = (((0,), (0,)), ((), ()))
MESH = pl.DeviceIdType.MESH
BF16 = jnp.bfloat16
F32 = jnp.float32
ANY = pl.BlockSpec(memory_space=pl.ANY)
HBM = pl.BlockSpec(memory_space=pltpu.HBM)
SEM = pl.BlockSpec(memory_space=pltpu.SEMAPHORE)
EFFECT = pltpu.SideEffectType.DATAFLOW_SIDE_EFFECTING


def _tile(n, target, mult):
    best = None
    for t in range(mult, min(n, target) + 1, mult):
        if n % t == 0:
            best = t
    return n if best is None else best


def _params(sem):
    return pltpu.CompilerParams(dimension_semantics=sem, vmem_limit_bytes=VMEM_LIMIT)


def _sds(shape, dtype):
    return jax.ShapeDtypeStruct(shape, dtype)


def _row_chunks(n_rows, fn):
    ch = _tile(n_rows, EPILOGUE_ROWS, 16)

    def step(i, carry):
        fn(pl.ds(pl.multiple_of(i * ch, ch), ch))
        return carry

    lax.fori_loop(0, n_rows // ch, step, 0)


def _mm(name, grid, dims, ab, ab_specs, extras, extra_specs, out_shape, out_specs, acc_shape, epilogue, deps=()):
    nk = grid[-1]
    n_extra = len(extras)
    n_in = 2 + n_extra + len(deps)
    n_out = len(out_shape)
    kax = len(grid) - 1

    def body(*refs):
        a_ref, b_ref = refs[0], refs[1]
        ex = refs[2:2 + n_extra]
        outs = refs[n_in:n_in + n_out]
        if nk == 1:
            epilogue(lax.dot_general(a_ref[...], b_ref[...], dims, preferred_element_type=F32), ex, outs, slice(None))
            return
        acc = refs[-1]
        k = pl.program_id(kax)

        @pl.when(k == 0)
        def _():
            acc[...] = lax.dot_general(a_ref[...], b_ref[...], dims, preferred_element_type=F32)

        @pl.when(k > 0)
        def _():
            acc[...] += lax.dot_general(a_ref[...], b_ref[...], dims, preferred_element_type=F32)

        @pl.when(k == nk - 1)
        def _():
            _row_chunks(acc_shape[0], lambda rows: epilogue(acc[rows, :], ex, outs, rows))

    scratch = [] if nk == 1 else [pltpu.VMEM(acc_shape, F32)]
    sem = ("parallel",) * kax + ("arbitrary",)
    return pl.pallas_call(
        body, name=name, grid=grid, in_specs=list(ab_specs) + list(extra_specs) + [ANY] * len(deps), out_specs=list(out_specs),
        out_shape=list(out_shape), scratch_shapes=scratch, compiler_params=_params(sem),
    )(*ab, *extras, *deps)


def _silu_parts(g):
    s = 1.0 / (1.0 + jnp.exp(-g))
    return s, g * s


def _ffn_gu(name, h_b, wgu_all):
    tp, d = h_b.shape
    ns, _, ng = wgu_all.shape
    half = ns // 2
    tm, tk = _tile(tp, TM_GU, 16), _tile(d, 2 * TK, LANE)
    grid = (tp // tm, half, d // tk)
    nk = grid[-1]

    def body(h_ref, wg_ref, wu_ref, gu_ref, act_ref, acc_g, acc_u):
        k = pl.program_id(2)

        @pl.when(k == 0)
        def _():
            acc_g[...] = jnp.dot(h_ref[...], wg_ref[...], preferred_element_type=F32)
            acc_u[...] = jnp.dot(h_ref[...], wu_ref[...], preferred_element_type=F32)

        @pl.when(k > 0)
        def _():
            acc_g[...] += jnp.dot(h_ref[...], wg_ref[...], preferred_element_type=F32)
            acc_u[...] += jnp.dot(h_ref[...], wu_ref[...], preferred_element_type=F32)

        @pl.when(k == nk - 1)
        def _():
            def finish(rows):
                g = acc_g[rows, :]
                u = acc_u[rows, :]
                _, silu = _silu_parts(g)
                gu_ref[0, rows, :] = g.astype(BF16)
                gu_ref[1, rows, :] = u.astype(BF16)
                act_ref[rows, :] = (silu * u).astype(BF16)

            _row_chunks(tm, finish)

    return pl.pallas_call(
        body, name=name, grid=grid,
        in_specs=[pl.BlockSpec((tm, tk), lambda m, s, k: (m, k)),
                  pl.BlockSpec((None, tk, ng), lambda m, s, k: (s, k, 0)),
                  pl.BlockSpec((None, tk, ng), lambda m, s, k: (s + half, k, 0))],
        out_specs=[pl.BlockSpec((None, 2, tm, ng), lambda m, s, k: (s, 0, m, 0)),
                   pl.BlockSpec((None, tm, ng), lambda m, s, k: (s, m, 0))],
        out_shape=[_sds((half, 2, tp, ng), BF16), _sds((half, tp, ng), BF16)],
        scratch_shapes=[pltpu.VMEM((tm, ng), F32), pltpu.VMEM((tm, ng), F32)],
        compiler_params=_params(("parallel", "parallel", "arbitrary")),
    )(h_b, wgu_all, wgu_all)


def _ffn_down(name, act, wd4, h):
    ns, tp, ng = act.shape
    d = wd4.shape[2]
    tm, tn = _tile(tp, TM_WIDE, 16), _tile(d, TN, LANE)

    def epi(acc, ex, outs, rows):
        outs[0][rows, :] = ALPHA * ex[0][rows, :] + 0.5 * acc

    return _mm(name, (tp // tm, d // tn, ns), NN, (act, wd4),
               [pl.BlockSpec((None, tm, ng), lambda m, n, s: (s, m, 0)),
                pl.BlockSpec((None, ng, tn), lambda m, n, s: (s, 0, n))],
               (h,), [pl.BlockSpec((tm, tn), lambda m, n, s: (m, n))],
               [_sds((tp, d), F32)], [pl.BlockSpec((tm, tn), lambda m, n, s: (m, n))], (tm, tn), epi)[0]


def _ffn_bwd_dgu(name, dp_b, wd4, gu, deps=()):
    tp, d = dp_b.shape
    ns, ng, _ = wd4.shape
    tm, tk = _tile(tp, TM_WIDE, 16), _tile(d, 2 * TK, LANE)

    def epi(acc, ex, outs, rows):
        g = ex[0][0, rows, :].astype(F32)
        u = ex[0][1, rows, :].astype(F32)
        da = 0.5 * acc
        s, silu = _silu_parts(g)
        outs[0][0, rows, :] = (da * u * (s + silu * (1.0 - s))).astype(BF16)
        outs[0][1, rows, :] = (da * silu).astype(BF16)

    return _mm(name, (tp // tm, ns, d // tk), NT, (dp_b, wd4),
               [pl.BlockSpec((tm, tk), lambda m, s, k: (m, k)),
                pl.BlockSpec((None, ng, tk), lambda m, s, k: (s, 0, k))],
               (gu,), [pl.BlockSpec((None, 2, tm, ng), lambda m, s, k: (s, 0, m, 0))],
               [_sds((ns, 2, tp, ng), BF16)], [pl.BlockSpec((None, 2, tm, ng), lambda m, s, k: (s, 0, m, 0))],
               (tm, ng), epi, deps)[0]


def _ffn_bwd_wd(name, act, dp_b, deps=()):
    ns, tp, ng = act.shape
    d = dp_b.shape[1]
    tkt, tn = _tile(tp, TK_TOKENS, LANE), _tile(d, TN, LANE)

    def epi(acc, ex, outs, rows):
        outs[0][rows, :] = (0.5 * acc).astype(BF16)

    return _mm(name, (ns, d // tn, tp // tkt), TN_DIMS, (act, dp_b),
               [pl.BlockSpec((None, tkt, ng), lambda s, n, t: (s, t, 0)),
                pl.BlockSpec((tkt, tn), lambda s, n, t: (t, n))],
               (), [], [_sds((ns, ng, d), BF16)], [pl.BlockSpec((None, ng, tn), lambda s, n, t: (s, 0, n))],
               (ng, tn), epi, deps)[0]


def _ffn_bwd_wgu(name, h_b, dgu, deps=()):
    tp, d = h_b.shape
    ns, _, _, ng = dgu.shape
    tkt, tmd = _tile(tp, TK_TOKENS, LANE), _tile(d, TN, LANE)

    def epi(acc, ex, outs, rows):
        outs[0][rows, :] = acc.astype(BF16)

    return _mm(name, (ns, 2, d // tmd, tp // tkt), TN_DIMS, (h_b, dgu),
               [pl.BlockSpec((tkt, tmd), lambda s, j, i, t: (t, i)),
                pl.BlockSpec((None, None, tkt, ng), lambda s, j, i, t: (s, j, t, 0))],
               (), [], [_sds((2 * ns, d, ng), BF16)],
               [pl.BlockSpec((None, tmd, ng), lambda s, j, i, t: (j * ns + s, i, 0))], (tmd, ng), epi, deps)[0]


def _ffn_bwd_dh(name, dgu, wgu_all, dp, deps=()):
    ns, _, tp, ng = dgu.shape
    d = wgu_all.shape[1]
    tm, tn = _tile(tp, TM_WIDE, 16), _tile(d, TN, LANE)

    def epi(acc, ex, outs, rows):
        outs[0][rows, :] = ALPHA * ex[0][rows, :] + acc

    return _mm(name, (tp // tm, d // tn, 2 * ns), NT, (dgu, wgu_all),
               [pl.BlockSpec((None, None, tm, ng), lambda m, n, j: (j % ns, j // ns, m, 0)),
                pl.BlockSpec((None, tn, ng), lambda m, n, j: (j, n, 0))],
               (dp,), [pl.BlockSpec((tm, tn), lambda m, n, j: (m, n))],
               [_sds((tp, d), F32)], [pl.BlockSpec((tm, tn), lambda m, n, j: (m, n))], (tm, tn), epi, deps)[0]


def _ln_stats(x):
    mu = jnp.mean(x, axis=-1, keepdims=True)
    xc = x - mu
    var = jnp.mean(xc * xc, axis=-1, keepdims=True)
    rstd = lax.rsqrt(var + LN_EPS)
    return xc * rstd, rstd


def _ln_bwd_rows(dy, xhat, rstd, g):
    dxh = dy * g
    m1 = jnp.mean(dxh, axis=-1, keepdims=True)
    m2 = jnp.mean(dxh * xhat, axis=-1, keepdims=True)
    return rstd * (dxh - m1 - xhat * m2)


def _ln_fwd(name, pre, g, b):
    tp, d = pre.shape
    tr = _tile(tp, TR_LN, 16)

    def body(x_ref, g_ref, b_ref, y_ref, yb_ref):
        xhat, _ = _ln_stats(x_ref[...])
        y = xhat * g_ref[...] + b_ref[...]
        y_ref[...] = y
        yb_ref[...] = y.astype(BF16)

    row = pl.BlockSpec((tr, d), lambda i: (i, 0))
    vec = pl.BlockSpec((1, d), lambda i: (0, 0))
    return pl.pallas_call(body, name=name, grid=(tp // tr,), in_specs=[row, vec, vec], out_specs=[row, row],
                          out_shape=[_sds((tp, d), F32), _sds((tp, d), BF16)],
                          compiler_params=_params(("parallel",)))(pre, g, b)


def _accumulate(i, ref, val):
    @pl.when(i == 0)
    def _():
        ref[...] = val

    @pl.when(i > 0)
    def _():
        ref[...] += val


def _ln_bwd(name, pre, dy, g):
    tp, d = pre.shape
    tr = _tile(tp, TR_LN, 16)

    def body(x_ref, dy_ref, g_ref, dx_ref, dxb_ref, dg_ref, db_ref):
        i = pl.program_id(0)
        xhat, rstd = _ln_stats(x_ref[...])
        dy = dy_ref[...]
        dx = _ln_bwd_rows(dy, xhat, rstd, g_ref[...])
        dx_ref[...] = dx
        dxb_ref[...] = dx.astype(BF16)
        _accumulate(i, dg_ref, jnp.sum(dy * xhat, axis=0, keepdims=True))
        _accumulate(i, db_ref, jnp.sum(dy, axis=0, keepdims=True))

    row = pl.BlockSpec((tr, d), lambda i: (i, 0))
    vec = pl.BlockSpec((1, d), lambda i: (0, 0))
    return pl.pallas_call(body, name=name, grid=(tp // tr,), in_specs=[row, row, vec], out_specs=[row, row, vec, vec],
                          out_shape=[_sds((tp, d), F32), _sds((tp, d), BF16), _sds((1, d), F32), _sds((1, d), F32)],
                          compiler_params=_params(("arbitrary",)))(pre, dy, g)


def _ln_loss_bwd(name, pre, tgt, g, b, n_rows, seq_len):
    tp, d = pre.shape
    tr = _tile(tp, TR_LN, 16)
    n_seq = n_rows // seq_len

    def body(x_ref, t_ref, g_ref, b_ref, dx_ref, dxb_ref, dg_ref, db_ref, sq_ref):
        i = pl.program_id(0)
        xhat, rstd = _ln_stats(x_ref[...])
        gain = g_ref[...]
        y = xhat * gain + b_ref[...]
        r = i * tr + lax.broadcasted_iota(jnp.int32, (tr, 1), 0)
        pos = r
        for s in range(1, n_seq):
            pos = jnp.where(r >= s * seq_len, r - s * seq_len, pos)
        live = jnp.logical_and(r < n_rows, pos >= N_META)
        err = jnp.where(live, y - t_ref[...], 0.0)
        dy = err * (1.0 / d)
        dx = _ln_bwd_rows(dy, xhat, rstd, gain)
        dx_ref[...] = dx
        dxb_ref[...] = dx.astype(BF16)
        _accumulate(i, dg_ref, jnp.sum(dy * xhat, axis=0, keepdims=True))
        _accumulate(i, db_ref, jnp.sum(dy, axis=0, keepdims=True))
        _accumulate(i, sq_ref, jnp.sum(err * err, axis=0, keepdims=True))

    row = pl.BlockSpec((tr, d), lambda i: (i, 0))
    vec = pl.BlockSpec((1, d), lambda i: (0, 0))
    return pl.pallas_call(
        body, name=name, grid=(tp // tr,), in_specs=[row, row, vec, vec], out_specs=[row, row, vec, vec, vec],
        out_shape=[_sds((tp, d), F32), _sds((tp, d), BF16), _sds((1, d), F32), _sds((1, d), F32), _sds((1, d), F32)],
        compiler_params=_params(("arbitrary",)))(pre, tgt, g, b)


def _proj_in(name, h_b, win_all):
    tp, d = h_b.shape
    ns, _, ni = win_all.shape
    tm, tk = _tile(tp, TM_BIG, 16), _tile(d, 4 * TK, LANE)

    def epi(acc, ex, outs, rows):
        outs[0][rows, :] = acc

    return _mm(name, (tp // tm, ns, d // tk), NN, (h_b, win_all),
               [pl.BlockSpec((tm, tk), lambda m, j, k: (m, k)),
                pl.BlockSpec((None, tk, ni), lambda m, j, k: (j, k, 0))],
               (), [], [_sds((tp, ns * ni), F32)], [pl.BlockSpec((tm, ni), lambda m, j, k: (m, j))], (tm, ni), epi)[0]


def _positions(tp, n_rows, seq_len):
    r = lax.broadcasted_iota(jnp.int32, (tp, 1), 0)
    pos = r
    for s in range(1, n_rows // seq_len):
        pos = jnp.where(r >= s * seq_len, r - s * seq_len, pos)
    return pos


def _shift_down(x, s, pos):
    return jnp.where(pos >= s, pltpu.roll(x, s, 0), 0.0)


def _shift_up(x, s, pos, seq_len):
    return jnp.where(pos + s < seq_len, pltpu.roll(x, x.shape[0] - s, 0), 0.0)


def _conv_fwd(name, u, conv_w, n_rows, seq_len):
    tp = u.shape[0]
    c = conv_w.shape[1]
    tc = _tile(c, TC_MIX, LANE)
    nb = c // tc

    def body(gb_ref, gc_ref, xi_ref, w_ref, y_ref):
        pos = _positions(tp, n_rows, seq_len)
        v = gc_ref[...] * xi_ref[...]
        w = w_ref[...]
        y = _shift_down(v, 2, pos) * w[0:1]
        y = y + _shift_down(v, 1, pos) * w[1:2]
        y = y + v * w[2:3]
        y_ref[...] = (gb_ref[...] * y).astype(BF16)

    col = lambda off: pl.BlockSpec((tp, tc), lambda i: (0, off + i))
    return pl.pallas_call(body, name=name, grid=(nb,), in_specs=[col(0), col(nb), col(2 * nb), pl.BlockSpec((CONV_K, tc), lambda i: (0, i))],
                          out_specs=pl.BlockSpec((tp, tc), lambda i: (0, i)), out_shape=_sds((tp, c), BF16),
                          compiler_params=_params(("parallel",)))(u, u, u, conv_w)


def _conv_bwd(name, u, conv_w, dy, n_rows, seq_len):
    tp = u.shape[0]
    c = conv_w.shape[1]
    tc = _tile(c, TC_MIX, LANE)
    nb = c // tc

    def body(gb_ref, gc_ref, xi_ref, w_ref, dy_ref, dgb_ref, dgc_ref, dxi_ref, dw_ref):
        pos = _positions(tp, n_rows, seq_len)
        gc, xi = gc_ref[...], xi_ref[...]
        v = gc * xi
        w = w_ref[...]
        v2, v1 = _shift_down(v, 2, pos), _shift_down(v, 1, pos)
        conv = v2 * w[0:1]
        conv = conv + v1 * w[1:2]
        conv = conv + v * w[2:3]
        dyc = dy_ref[...]
        dgb_ref[...] = (dyc * conv).astype(BF16)
        dconv = dyc * gb_ref[...]
        dv = dconv * w[2:3] + _shift_up(dconv, 1, pos, seq_len) * w[1:2] + _shift_up(dconv, 2, pos, seq_len) * w[0:1]
        dgc_ref[...] = (dv * xi).astype(BF16)
        dxi_ref[...] = (dv * gc).astype(BF16)
        dw_ref[0:1, :] = jnp.sum(dconv * v2, axis=0, keepdims=True)
        dw_ref[1:2, :] = jnp.sum(dconv * v1, axis=0, keepdims=True)
        dw_ref[2:3, :] = jnp.sum(dconv * v, axis=0, keepdims=True)

    col = lambda off: pl.BlockSpec((tp, tc), lambda i: (0, off + i))
    wspec = pl.BlockSpec((CONV_K, tc), lambda i: (0, i))
    return pl.pallas_call(body, name=name, grid=(nb,), in_specs=[col(0), col(nb), col(2 * nb), wspec, col(0)],
                          out_specs=[col(0), col(0), col(0), wspec],
                          out_shape=[_sds((tp, c), BF16)] * 3 + [_sds((CONV_K, c), F32)],
                          compiler_params=_params(("parallel",)))(u, u, u, conv_w, dy)


def _window_select(group, parts):
    out = parts[-1]
    for gi in range(len(parts) - 2, -1, -1):
        out = jnp.where(group == gi, parts[gi], out)
    return out


def _pool_fwd(name, u, col0, p, pg, n_rows, seq_len):
    tp = u.shape[0]
    tc = _tile(pg, TC_MIX, LANE)
    per_group = pg // tc

    def body(z_ref, d_ref):
        group = pl.program_id(0) // per_group
        pos = _positions(tp, n_rows, seq_len)
        z = z_ref[...]
        sums, s, w = [], z, 1
        for _ in POOL_WINDOWS:
            s = s + _shift_down(s, w, pos)
            w *= 2
            sums.append(s)
        total = _window_select(group, sums)
        count = jnp.minimum(pos + 1, 2 << group).astype(F32)
        d_ref[...] = (total / count - z).astype(BF16)

    return pl.pallas_call(body, name=name, grid=(p // tc,), in_specs=[pl.BlockSpec((tp, tc), lambda i: (0, col0 // tc + i))],
                          out_specs=pl.BlockSpec((tp, tc), lambda i: (0, i)), out_shape=_sds((tp, p), BF16),
                          compiler_params=_params(("parallel",)))(u)


def _pool_bwd(name, dd, pg, n_rows, seq_len):
    tp, p = dd.shape
    tc = _tile(pg, TC_MIX, LANE)
    per_group = pg // tc

    def body(dd_ref, dz_ref):
        group = pl.program_id(0) // per_group
        pos = _positions(tp, n_rows, seq_len)
        dd_v = dd_ref[...]
        count = jnp.minimum(pos + 1, 2 << group).astype(F32)
        sums, s, w = [], dd_v / count, 1
        for _ in POOL_WINDOWS:
            s = s + _shift_up(s, w, pos, seq_len)
            w *= 2
            sums.append(s)
        dz_ref[...] = (_window_select(group, sums) - dd_v).astype(BF16)

    spec = pl.BlockSpec((tp, tc), lambda i: (0, i))
    return pl.pallas_call(body, name=name, grid=(p // tc,), in_specs=[spec], out_specs=spec, out_shape=_sds((tp, p), BF16),
                          compiler_params=_params(("parallel",)))(dd)


def _pool_mix(name, dpool, pool_w_b, scale):
    tp, p = dpool.shape
    ng, pg, _ = pool_w_b.shape
    tm = _tile(tp, TM_BIG, 16)

    def epi(acc, ex, outs, rows):
        outs[0][rows, :] = acc
        outs[1][rows, :] = (acc * ex[0][...]).astype(BF16)

    blk = pl.BlockSpec((tm, pg), lambda m, g, k: (m, g))
    return _mm(name, (tp // tm, ng, 1), NN, (dpool, pool_w_b), [blk, pl.BlockSpec((None, pg, pg), lambda m, g, k: (g, 0, 0))],
               (scale,), [pl.BlockSpec((1, pg), lambda m, g, k: (0, g))],
               [_sds((tp, p), F32), _sds((tp, p), BF16)], [blk, blk], None, epi)


def _pool_scale_bwd(name, dy, col_block, ypre, scale):
    tp, p = ypre.shape
    tr = _tile(tp, TR_LN, 16)

    def body(dy_ref, yp_ref, s_ref, o_ref, ds_ref):
        i = pl.program_id(0)
        dyp = dy_ref[...]
        o_ref[...] = (dyp * s_ref[...]).astype(BF16)
        _accumulate(i, ds_ref, jnp.sum(dyp * yp_ref[...], axis=0, keepdims=True))

    row = pl.BlockSpec((tr, p), lambda i: (i, 0))
    vec = pl.BlockSpec((1, p), lambda i: (0, 0))
    return pl.pallas_call(body, name=name, grid=(tp // tr,), in_specs=[pl.BlockSpec((tr, p), lambda i: (i, col_block)), row, vec],
                          out_specs=[row, vec], out_shape=[_sds((tp, p), BF16), _sds((1, p), F32)],
                          compiler_params=_params(("arbitrary",)))(dy, ypre, scale)


def _pool_mix_bwd_in(name, dyps, pool_w_b):
    tp, p = dyps.shape
    ng, pg, _ = pool_w_b.shape
    tm = _tile(tp, TM_BIG, 16)

    def epi(acc, ex, outs, rows):
        outs[0][rows, :] = acc

    blk = pl.BlockSpec((tm, pg), lambda m, g, k: (m, g))
    return _mm(name, (tp // tm, ng, 1), NT, (dyps, pool_w_b), [blk, pl.BlockSpec((None, pg, pg), lambda m, g, k: (g, 0, 0))],
               (), [], [_sds((tp, p), F32)], [blk], None, epi)[0]


def _pool_mix_bwd_w(name, dpool, dyps, pg):
    tp, p = dpool.shape
    ng = p // pg
    tkt = _tile(tp, TM_BIG, LANE)

    def epi(acc, ex, outs, rows):
        outs[0][rows, :] = acc

    blk = pl.BlockSpec((tkt, pg), lambda g, t: (t, g))
    return _mm(name, (ng, tp // tkt), TN_DIMS, (dpool, dyps), [blk, blk], (), [],
               [_sds((ng, pg, pg), F32)], [pl.BlockSpec((None, pg, pg), lambda g, t: (g, 0, 0))], (pg, pg), epi)[0]


def _proj_out(name, y, wout_all, h):
    tp, c = y.shape
    d = wout_all.shape[2]
    w = wout_all.reshape(c, d)
    tm, tn, tk = _tile(tp, TM_BIG, 16), _tile(d, TN, LANE), _tile(c, 4 * TK, LANE)

    def epi(acc, ex, outs, rows):
        outs[0][rows, :] = ALPHA * ex[0][rows, :] + acc

    mn = pl.BlockSpec((tm, tn), lambda m, n, k: (m, n))
    return _mm(name, (tp // tm, d // tn, c // tk), NN, (y, w),
               [pl.BlockSpec((tm, tk), lambda m, n, k: (m, k)), pl.BlockSpec((tk, tn), lambda m, n, k: (k, n))],
               (h,), [mn], [_sds((tp, d), F32)], [mn], (tm, tn), epi)[0]


def _proj_out_bwd_y(name, dp_b, wout_all, deps=()):
    tp, d = dp_b.shape
    ns, ro, _ = wout_all.shape
    tm, tk = _tile(tp, TM_BIG, 16), d

    def epi(acc, ex, outs, rows):
        outs[0][rows, :] = acc

    return _mm(name, (tp // tm, ns, d // tk), NT, (dp_b, wout_all),
               [pl.BlockSpec((tm, tk), lambda m, j, k: (m, k)), pl.BlockSpec((None, ro, tk), lambda m, j, k: (j, 0, k))],
               (), [], [_sds((tp, ns * ro), F32)], [pl.BlockSpec((tm, ro), lambda m, j, k: (m, j))], (tm, ro), epi, deps)[0]


def _proj_out_bwd_w(name, y, dp_b, ns):
    tp, c = y.shape
    d = dp_b.shape[1]
    ro = c // ns
    tkt, tn = _tile(tp, TM_BIG, LANE), _tile(d, 2 * TN, LANE)

    def epi(acc, ex, outs, rows):
        outs[0][rows, :] = acc.astype(BF16)

    return _mm(name, (ns, d // tn, tp // tkt), TN_DIMS, (y, dp_b),
               [pl.BlockSpec((tkt, ro), lambda j, n, t: (t, j)), pl.BlockSpec((tkt, tn), lambda j, n, t: (t, n))],
               (), [], [_sds((ns, ro, d), BF16)], [pl.BlockSpec((None, ro, tn), lambda j, n, t: (j, 0, n))], (ro, tn), epi)[0]


def _proj_in_bwd_w(name, h_b, du, ns, deps=()):
    tp, d = h_b.shape
    ni = du.shape[1] // ns
    tkt, tmd = _tile(tp, TK_TOKENS, LANE), _tile(d, 2 * TN, LANE)

    def epi(acc, ex, outs, rows):
        outs[0][rows, :] = acc.astype(BF16)

    return _mm(name, (ns, d // tmd, tp // tkt), TN_DIMS, (h_b, du),
               [pl.BlockSpec((tkt, tmd), lambda j, i, t: (t, i)), pl.BlockSpec((tkt, ni), lambda j, i, t: (t, j))],
               (), [], [_sds((ns, d, ni), BF16)], [pl.BlockSpec((None, tmd, ni), lambda j, i, t: (j, i, 0))], (tmd, ni), epi, deps)[0]


def _proj_in_bwd_h(name, du, win_all, dp, deps=()):
    tp = du.shape[0]
    ns, d, ni = win_all.shape
    tm, tn = _tile(tp, TM_BIG, 16), _tile(d, TN, LANE)

    def epi(acc, ex, outs, rows):
        outs[0][rows, :] = ALPHA * ex[0][rows, :] + acc

    mn = pl.BlockSpec((tm, tn), lambda m, n, j: (m, n))
    return _mm(name, (tp // tm, d // tn, ns), NT, (du, win_all),
               [pl.BlockSpec((tm, ni), lambda m, n, j: (m, j)), pl.BlockSpec((None, tn, ni), lambda m, n, j: (j, n, 0))],
               (dp,), [mn], [_sds((tp, d), F32)], [mn], (tm, tn), epi, deps)[0]


def _place():
    return lax.axis_index("x"), lax.axis_index("y"), lax.axis_index("c")


def _hbm(a):
    return pltpu.with_memory_space_constraint(a, pltpu.HBM)


def _token_shape():
    return _sds((8, LANE), F32)


def _gather_peers(x, y, c):
    return [(x, y, 1 - c), (1 - x, y, c), (x, 1 - y, c), (1 - x, 1 - y, c)]


def _into_slab(name, w, dev, dtype):
    r, c = w.shape
    tr = _tile(r, max(16, ELEM_BLOCK_BYTES // (4 * c)), 16 if dtype == BF16 else 8)

    def body(dev_ref, w_ref, o_ref):
        o_ref[...] = w_ref[...].astype(dtype)

    return pl.pallas_call(
        body, name=name,
        grid_spec=pltpu.PrefetchScalarGridSpec(
            num_scalar_prefetch=1, grid=(r // tr,), in_specs=[pl.BlockSpec((tr, c), lambda i, dev_ref: (i, 0))],
            out_specs=pl.BlockSpec((None, tr, c), lambda i, dev_ref: (dev_ref[0], i, 0))),
        out_shape=_sds((N_DEV, r, c), dtype), compiler_params=_params(("parallel",)),
    )(dev, w)


def _gather_start(name, lands, after=()):
    n = len(lands)
    n_peer = N_CHIP
    n_in = n + len(after)

    def body(*refs):
        land_refs = refs[:n]
        send_sems, recv_sems = refs[n_in:n_in + n], refs[n_in + n:n_in + 2 * n]
        token = refs[-1]
        x, y, c = _place()
        me = 4 * x + 2 * y + c
        for i in range(n):
            for k, peer in enumerate(_gather_peers(x, y, c)):
                pltpu.make_async_remote_copy(
                    src_ref=land_refs[i].at[me], dst_ref=land_refs[i].at[me], send_sem=send_sems[i].at[k],
                    recv_sem=recv_sems[i].at[k], device_id=peer, device_id_type=MESH).start()
        token[...] = jnp.zeros_like(token)

    sem = pltpu.SemaphoreType.DMA((n_peer,))
    out = pl.pallas_call(
        body, name=name,
        out_shape=[sem] * (2 * n) + [pltpu.HBM(l.shape, l.dtype) for l in lands] + [_token_shape()],
        in_specs=[HBM] * n + [ANY] * len(after),
        out_specs=[SEM] * (2 * n) + [HBM] * n + [pl.BlockSpec(memory_space=pltpu.VMEM)],
        input_output_aliases={i: 2 * n + i for i in range(n)},
        compiler_params=pltpu.CompilerParams(has_side_effects=EFFECT),
    )(*[_hbm(l) for l in lands], *after)
    per = [(out[i], out[n + i], out[2 * n + i]) for i in range(n)]
    return per, out[-1]


def _gather_wait(name, started, after):
    send_sems, recv_sems, land = started
    after = after if isinstance(after, tuple) else (after,)

    def body(land_ref, send_ref, recv_ref, *rest):
        x, y, c = _place()
        for k, (px, py, pc) in enumerate(_gather_peers(x, y, c)):
            cp = pltpu.make_async_remote_copy(
                src_ref=land_ref.at[4 * x + 2 * y + c], dst_ref=land_ref.at[4 * px + 2 * py + pc], send_sem=send_ref.at[k],
                recv_sem=recv_ref.at[k], device_id=(px, py, pc), device_id_type=MESH)
            cp.wait_send()
            cp.wait_recv()

    return pl.pallas_call(
        body, name=name, out_shape=pltpu.HBM(land.shape, land.dtype),
        in_specs=(HBM, SEM, SEM) + (ANY,) * len(after), out_specs=HBM, input_output_aliases={0: 0},
        compiler_params=pltpu.CompilerParams(has_side_effects=EFFECT),
    )(land, send_sems, recv_sems, *after)


def _gather_finish(name, land, deps=()):
    def body(land_ref, *rest):
        out_ref, send_sems, recv_sems = rest[len(deps):]
        x, y, c = _place()
        copies = []
        for k, (px, py) in enumerate([(1 - x, y), (x, 1 - y), (1 - x, 1 - y)]):
            slab = 4 * px + 2 * py + c
            copies.append(pltpu.make_async_remote_copy(
                src_ref=land_ref.at[slab], dst_ref=out_ref.at[slab], send_sem=send_sems.at[k], recv_sem=recv_sems.at[k],
                device_id=(x, y, 1 - c), device_id_type=MESH))
        for cp in copies:
            cp.start()
        for cp in copies:
            cp.wait()

    return pl.pallas_call(
        body, name=name, out_shape=_sds(land.shape, land.dtype), in_specs=[ANY] + [ANY] * len(deps), out_specs=ANY,
        input_output_aliases={0: 0},
        scratch_shapes=[pltpu.SemaphoreType.DMA((N_CHIP - 1,)), pltpu.SemaphoreType.DMA((N_CHIP - 1,))],
    )(land, *deps)


def _route_sibling(x, y, c):
    return [(2 * j + (1 - c), (x, y, 1 - c)) for j in range(N_CHIP)]


def _route_chips(x, y, c):
    return [(k, (px, py, c)) for k, (px, py) in enumerate([(1 - x, y), (x, 1 - y), (1 - x, 1 - y)])]


def _exchange_start(name, src, route):
    n_copy = len(route(0, 0, 0))

    def body(s_ref, land_ref, send_sems, recv_sems, s_out, land_out, token):
        for k, (slab, peer) in enumerate(route(*_place())):
            pltpu.make_async_remote_copy(
                src_ref=s_ref.at[slab], dst_ref=land_ref.at[k], send_sem=send_sems.at[k], recv_sem=recv_sems.at[k],
                device_id=peer, device_id_type=MESH).start()
        token[...] = jnp.zeros_like(token)

    land = lax.empty((n_copy,) + src.shape[1:], src.dtype)
    sem = pltpu.SemaphoreType.DMA((n_copy,))
    out = pl.pallas_call(
        body, name=name,
        out_shape=[sem, sem, pltpu.HBM(src.shape, src.dtype), pltpu.HBM(land.shape, land.dtype), _token_shape()],
        in_specs=[HBM, HBM], out_specs=[SEM, SEM, HBM, HBM, pl.BlockSpec(memory_space=pltpu.VMEM)],
        input_output_aliases={0: 2, 1: 3}, compiler_params=pltpu.CompilerParams(has_side_effects=EFFECT),
    )(_hbm(src), _hbm(land))
    return out[:4], out[4]


def _exchange_wait(name, started, route, after):
    send_sems, recv_sems, src, land = started

    def body(s_ref, land_ref, send_ref, recv_ref, after_ref, s_out, land_out):
        for k, (slab, peer) in enumerate(route(*_place())):
            cp = pltpu.make_async_remote_copy(
                src_ref=s_ref.at[slab], dst_ref=land_ref.at[k], send_sem=send_ref.at[k], recv_sem=recv_ref.at[k],
                device_id=peer, device_id_type=MESH)
            cp.wait_send()
            cp.wait_recv()

    return pl.pallas_call(
        body, name=name, out_shape=(pltpu.HBM(src.shape, src.dtype), pltpu.HBM(land.shape, land.dtype)),
        in_specs=(HBM, HBM, SEM, SEM, ANY), out_specs=(HBM, HBM), input_output_aliases={0: 0, 1: 1},
        compiler_params=pltpu.CompilerParams(has_side_effects=EFFECT),
    )(src, land, send_sems, recv_sems, after)


def _all_gather(name, land):
    def body(land_ref, out_ref, send_sems, recv_sems):
        x, y, c = _place()
        me, sibling = (x, y, c), (x, y, 1 - c)
        chips = [(1 - x, y), (x, 1 - y), (1 - x, 1 - y)]

        def copy(k, block, to):
            slab = 4 * block[0] + 2 * block[1] + block[2]
            return pltpu.make_async_remote_copy(
                src_ref=land_ref.at[slab], dst_ref=out_ref.at[slab],
                send_sem=send_sems.at[k], recv_sem=recv_sems.at[k], device_id=to, device_id_type=MESH)

        first = [copy(0, me, sibling)] + [copy(1 + j, me, (*chip, c)) for j, chip in enumerate(chips)]
        for cp in first:
            cp.start()
        passed = [copy(4 + j, (*chip, c), sibling) for j, chip in enumerate(chips)]
        for j, chip in enumerate(chips):
            copy(1 + j, (*chip, c), me).wait_recv()
            passed[j].start()
        copy(0, sibling, me).wait_recv()
        for j, chip in enumerate(chips):
            copy(4 + j, (*chip, 1 - c), me).wait_recv()
        for cp in first + passed:
            cp.wait_send()

    return pl.pallas_call(
        body, name=name, out_shape=_sds(land.shape, land.dtype), in_specs=[ANY], out_specs=ANY, input_output_aliases={0: 0},
        scratch_shapes=[pltpu.SemaphoreType.DMA((N_DEV - 1,)), pltpu.SemaphoreType.DMA((N_DEV - 1,))],
    )(land)


def _add_sibling(name, part, recv, place):
    _, r, c = part.shape
    tr = _tile(r, max(16, 2 * ELEM_BLOCK_BYTES // (2 * c)), 16)
    n_out = N_CHIP - 1

    def chip_of(k, x_ref, y_ref):
        px = jnp.where(k == 1, x_ref[0], 1 - x_ref[0])
        py = jnp.where(k == 0, y_ref[0], 1 - y_ref[0])
        return 2 * px + py

    def body(x_ref, y_ref, c_ref, p_ref, r_ref, o_ref):
        o_ref[...] = (p_ref[...].astype(F32) + r_ref[...].astype(F32)).astype(BF16)

    return pl.pallas_call(
        body, name=name,
        grid_spec=pltpu.PrefetchScalarGridSpec(
            num_scalar_prefetch=3, grid=(n_out, r // tr),
            in_specs=[pl.BlockSpec((None, tr, c), lambda k, i, x_ref, y_ref, c_ref: (2 * chip_of(k, x_ref, y_ref) + c_ref[0], i, 0)),
                      pl.BlockSpec((None, tr, c), lambda k, i, x_ref, y_ref, c_ref: (chip_of(k, x_ref, y_ref), i, 0))],
            out_specs=pl.BlockSpec((None, tr, c), lambda k, i, x_ref, y_ref, c_ref: (k, i, 0))),
        out_shape=_sds((n_out, r, c), BF16), compiler_params=_params(("parallel", "parallel")),
    )(*place, part, recv)


def _adamw_math(w, g, m, v):
    m = ADAM_B1 * m + (1.0 - ADAM_B1) * g
    v = ADAM_B2 * v + (1.0 - ADAM_B2) * (g * g)
    m_hat = m / (1.0 - ADAM_B1 ** ADAM_STEP)
    v_hat = v / (1.0 - ADAM_B2 ** ADAM_STEP)
    delta = -ADAM_LR * (m_hat / (jnp.sqrt(v_hat) + ADAM_EPS) + ADAM_WD * w)
    return delta, m, v


def _adamw_big(name, part, from_sibling, from_chips, dev, chip, w, m, v):
    r, c = w.shape
    tr = _tile(r, max(16, ELEM_BLOCK_BYTES // (4 * c)), 16)

    def body(dev_ref, chip_ref, p_ref, s_ref, r_ref, w_ref, m_ref, v_ref, g_out, d_out, m_out, v_out):
        g = p_ref[...].astype(F32) + s_ref[...].astype(F32)
        for k in range(N_CHIP - 1):
            g = g + r_ref[k].astype(F32)
        delta, m_new, v_new = _adamw_math(w_ref[...], g, m_ref[...], v_ref[...])
        g_out[...] = g
        d_out[...] = delta
        m_out[...] = m_new
        v_out[...] = v_new

    blk = pl.BlockSpec((tr, c), lambda i, dev_ref, chip_ref: (i, 0))
    return pl.pallas_call(
        body, name=name,
        grid_spec=pltpu.PrefetchScalarGridSpec(
            num_scalar_prefetch=2, grid=(r // tr,),
            in_specs=[pl.BlockSpec((None, tr, c), lambda i, dev_ref, chip_ref: (dev_ref[0], i, 0)),
                      pl.BlockSpec((None, tr, c), lambda i, dev_ref, chip_ref: (chip_ref[0], i, 0)),
                      pl.BlockSpec((N_CHIP - 1, tr, c), lambda i, dev_ref, chip_ref: (0, i, 0)), blk, blk, blk],
            out_specs=[blk, blk, blk, blk]),
        out_shape=[_sds((r, c), F32)] * 4, compiler_params=_params(("parallel",)),
    )(dev, chip, part, from_sibling, from_chips, w, m, v)


def _sum_parts(name, parts):
    n, r, c = parts.shape
    tr = _tile(r, max(8, ELEM_BLOCK_BYTES // (4 * c)), 8)

    def body(p_ref, o_ref):
        acc = p_ref[0]
        for k in range(1, n):
            acc = acc + p_ref[k]
        o_ref[...] = acc

    return pl.pallas_call(body, name=name, grid=(r // tr,), in_specs=[pl.BlockSpec((n, tr, c), lambda i: (0, i, 0))],
                          out_specs=pl.BlockSpec((tr, c), lambda i: (i, 0)), out_shape=_sds((r, c), F32),
                          compiler_params=_params(("parallel",)))(parts)


def _adamw_small(name, ws, gs, ms, vs):
    n = len(ws)

    def body(*refs):
        ins, outs = refs[:4 * n], refs[4 * n:]
        for i in range(n):
            delta, m_new, v_new = _adamw_math(ins[i][...], ins[n + i][...], ins[2 * n + i][...], ins[3 * n + i][...])
            outs[i][...] = delta
            outs[n + i][...] = m_new
            outs[2 * n + i][...] = v_new

    shapes = [_sds(w.shape, F32) for w in ws]
    return pl.pallas_call(body, name=name, out_shape=shapes * 3)(*ws, *gs, *ms, *vs)


def _pad_rows(a, rows):
    return jnp.pad(a, ((0, rows - a.shape[0]), (0, 0)))


def kernel(x, meta_tokens, ffn1_w_gu, ffn1_w_down, ln1_g, ln1_b, w_in, conv_w, pool_w, pool_scale, w_out, ln2_g, ln2_b, ffn2_w_gu, ffn2_w_down, ln3_g, ln3_b, loss_target, m_meta_tokens, m_ffn1_w_gu, m_ffn1_w_down, m_ln1_g, m_ln1_b, m_w_in, m_conv_w, m_pool_w, m_pool_scale, m_w_out, m_ln2_g, m_ln2_b, m_ffn2_w_gu, m_ffn2_w_down, m_ln3_g, m_ln3_b, v_meta_tokens, v_ffn1_w_gu, v_ffn1_w_down, v_ln1_g, v_ln1_b, v_w_in, v_conv_w, v_pool_w, v_pool_scale, v_w_out, v_ln2_g, v_ln2_b, v_ffn2_w_gu, v_ffn2_w_down, v_ln3_g, v_ln3_b):
    names = ["meta_tokens", "ffn1_w_gu", "ffn1_w_down", "ln1_g", "ln1_b", "w_in", "conv_w", "pool_w", "pool_scale", "w_out",
             "ln2_g", "ln2_b", "ffn2_w_gu", "ffn2_w_down", "ln3_g", "ln3_b"]
    w_of = dict(zip(names, [meta_tokens, ffn1_w_gu, ffn1_w_down, ln1_g, ln1_b, w_in, conv_w, pool_w, pool_scale, w_out,
                            ln2_g, ln2_b, ffn2_w_gu, ffn2_w_down, ln3_g, ln3_b]))
    m_of = dict(zip(names, [m_meta_tokens, m_ffn1_w_gu, m_ffn1_w_down, m_ln1_g, m_ln1_b, m_w_in, m_conv_w, m_pool_w, m_pool_scale,
                            m_w_out, m_ln2_g, m_ln2_b, m_ffn2_w_gu, m_ffn2_w_down, m_ln3_g, m_ln3_b]))
    v_of = dict(zip(names, [v_meta_tokens, v_ffn1_w_gu, v_ffn1_w_down, v_ln1_g, v_ln1_b, v_w_in, v_conv_w, v_pool_w, v_pool_scale,
                            v_w_out, v_ln2_g, v_ln2_b, v_ffn2_w_gu, v_ffn2_w_down, v_ln3_g, v_ln3_b]))

    n_seq, seq, d = x.shape
    seq_len = seq + N_META
    n_rows = n_seq * seq_len
    tp = -(-n_rows // ROW_ALIGN) * ROW_ALIGN
    c_conv = conv_w.shape[2] * N_DEV
    p_pool = pool_scale.shape[1]
    pg = pool_w.shape[3]
    assert c_conv == p_pool and p_pool == N_POOL_GROUPS * pg and POOL_WINDOWS == tuple(2 << g for g in range(N_POOL_GROUPS))
    assert (N_POOL_GROUPS * pg * pg) % d == 0 and pg % LANE == 0

    xi, yi, ci = _place()
    dev_index = 4 * xi + 2 * yi + ci
    dev = jnp.reshape(dev_index, (1,)).astype(jnp.int32)
    chip = jnp.reshape(2 * xi + yi, (1,)).astype(jnp.int32)
    place = tuple(jnp.reshape(a, (1,)).astype(jnp.int32) for a in (xi, yi, ci))

    big = ["ffn1_w_gu", "ffn1_w_down", "w_in", "w_out", "ffn2_w_gu", "ffn2_w_down"]
    wcol = d // N_DEV
    conv_rows = 8
    small_local = jnp.concatenate([
        meta_tokens,
        pool_w[0].reshape(N_POOL_GROUPS * (pg // N_DEV), pg),
        jnp.pad(conv_w[0], ((0, conv_rows - CONV_K), (0, wcol - conv_w.shape[2]))),
    ], axis=0)
    lands = {n: _into_slab(f"slab_{n}", w_of[n][0], dev, BF16) for n in big}
    lands["small"] = _into_slab("slab_small", small_local, dev, F32)
    started = {}

    def start(tag, which, after=()):
        per, token = _gather_start(f"ag_start_{tag}", [lands[n] for n in which], after)
        started.update(zip(which, per))
        return token

    def gathered(n, after, then_start=()):
        land = _gather_wait(f"ag_wait_{n}", started[n], after)
        deps = (start(f"after_{n}", then_start, (land,)),) if then_start else ()
        return _gather_finish(f"ag_finish_{n}", land, deps)

    gather_token = start("first", ["small", "ffn1_w_gu"])
    small_all = gathered("small", gather_token)
    r0, r1 = N_META, N_META + N_POOL_GROUPS * (pg // N_DEV)
    meta_full = jnp.transpose(small_all[:, :r0], (1, 0, 2)).reshape(N_META, d)
    pool_w_full = jnp.transpose(small_all[:, r0:r1].reshape(N_DEV, N_POOL_GROUPS, pg // N_DEV, pg), (1, 0, 2, 3)).reshape(N_POOL_GROUPS, pg, pg)
    conv_w_full = jnp.transpose(small_all[:, r1:r1 + CONV_K, :conv_w.shape[2]], (1, 0, 2)).reshape(CONV_K, c_conv)
    pool_w_b = pool_w_full.astype(BF16)

    h0 = jnp.concatenate([jnp.broadcast_to(meta_full[None], (n_seq, N_META, d)), x], axis=1).reshape(n_rows, d)
    h0 = _pad_rows(h0, tp)
    h0_b = h0.astype(BF16)
    tgt = _pad_rows(jnp.pad(loss_target, ((0, 0), (N_META, 0), (0, 0))).reshape(n_rows, d), tp)

    early = (h0_b, tgt) + tuple(lands[n] for n in big[1:])
    early += tuple(a[n][0] for n in ("ffn1_w_gu", "ffn2_w_gu") for a in (m_of, v_of))
    wgu1 = gathered("ffn1_w_gu", early, ["ffn1_w_down", "w_in"])
    gu1, act1 = _ffn_gu("ffn1_gu", h0_b, wgu1)
    wd1 = gathered("ffn1_w_down", act1, ["w_out", "ffn2_w_gu"]).reshape(N_CHIP, -1, d)
    pre1 = _ffn_down("ffn1_down", act1, wd1, h0)
    win_all = gathered("w_in", pre1)
    h1, h1_b = _ln_fwd("ln1", pre1, ln1_g, ln1_b)

    u = _proj_in("mix_in", h1_b, win_all)
    wout_all = gathered("w_out", u)
    y_conv = _conv_fwd("mix_conv", u, conv_w_full, n_rows, seq_len)
    dpool = _pool_fwd("mix_pool", u, 3 * c_conv, p_pool, pg, n_rows, seq_len)
    ypre, y_pool = _pool_mix("mix_pool_w", dpool, pool_w_b, pool_scale)
    y_mix = jnp.concatenate([y_conv, y_pool], axis=1)
    pre2 = _proj_out("mix_out", y_mix, wout_all, h1)
    wgu2 = gathered("ffn2_w_gu", pre2, ["ffn2_w_down"])
    h2, h2_b = _ln_fwd("ln2", pre2, ln2_g, ln2_b)

    gu2, act2 = _ffn_gu("ffn2_gu", h2_b, wgu2)
    wd2 = gathered("ffn2_w_down", act2).reshape(N_CHIP, -1, d)
    pre3 = _ffn_down("ffn2_down", act2, wd2, h2)

    dpre3, dpre3_b, d_ln3_g, d_ln3_b, sq = _ln_loss_bwd("ln3_loss", pre3, tgt, ln3_g, ln3_b, n_rows, seq_len)
    loss = lax.psum(0.5 * jnp.sum(sq) / d, ("x", "y", "c"))
    in_sibling, reducing = [], {}

    def exchange(after, new=None):
        tokens = []
        while in_sibling:
            n, started = in_sibling.pop(0)
            part, from_sibling = _exchange_wait(f"rs_sibling_wait_{n}", started, _route_sibling, after)
            summed = _add_sibling(f"rs_add_{n}", part, from_sibling, place)
            started, token = _exchange_start(f"rs_chips_start_{n}", summed, _route_chips)
            reducing[n] = (part, from_sibling, started)
            tokens.append(token)
        if new is not None:
            started, token = _exchange_start(f"rs_sibling_start_{new[0]}", new[1], _route_sibling)
            in_sibling.append((new[0], started))
            tokens.append(token)
        return tuple(tokens)

    dgu2 = _ffn_bwd_dgu("ffn2_bwd_dgu", dpre3_b, wd2, gu2)
    g_wd2 = _ffn_bwd_wd("ffn2_bwd_wd", act2, dpre3_b)
    tokens = exchange(g_wd2, ("ffn2_w_down", g_wd2.reshape(N_DEV, -1, d)))
    g_wgu2 = _ffn_bwd_wgu("ffn2_bwd_wgu", h2_b, dgu2, tokens)
    tokens = exchange(g_wgu2, ("ffn2_w_gu", g_wgu2))
    dh2 = _ffn_bwd_dh("ffn2_bwd_dh", dgu2, wgu2, dpre3, tokens)
    tokens = exchange(dh2)
    dpre2, dpre2_b, d_ln2_g, d_ln2_b = _ln_bwd("ln2_bwd", pre2, dh2, ln2_g)

    dy_mix = _proj_out_bwd_y("mix_out_bwd_y", dpre2_b, wout_all, tokens)
    g_wout = _proj_out_bwd_w("mix_out_bwd_w", y_mix, dpre2_b, N_DEV)
    tokens = exchange(g_wout, ("w_out", g_wout))
    dyps, d_pool_scale = _pool_scale_bwd("mix_pool_scale_bwd", dy_mix, 1, ypre, pool_scale)
    dd = _pool_mix_bwd_in("mix_pool_w_bwd_in", dyps, pool_w_b)
    d_pool_w = _pool_mix_bwd_w("mix_pool_w_bwd_w", dpool, dyps, pg)
    du_pool = _pool_bwd("mix_pool_bwd", dd, pg, n_rows, seq_len)
    du_b, du_c, du_x, d_conv_w = _conv_bwd("mix_conv_bwd", u, conv_w_full, dy_mix, n_rows, seq_len)
    du = jnp.concatenate([du_b, du_c, du_x, du_pool], axis=1)
    g_win = _proj_in_bwd_w("mix_in_bwd_w", h1_b, du, N_DEV, tokens)
    tokens = exchange(g_win, ("w_in", g_win))
    dh1 = _proj_in_bwd_h("mix_in_bwd_h", du, win_all, dpre2, tokens)
    tokens = exchange(dh1)
    dpre1, dpre1_b, d_ln1_g, d_ln1_b = _ln_bwd("ln1_bwd", pre1, dh1, ln1_g)

    def widen(a):
        return jnp.pad(a, ((0, 0), (0, d - a.shape[1])))

    small_part = jnp.concatenate([
        d_ln1_g, d_ln1_b, d_ln2_g, d_ln2_b, d_ln3_g, d_ln3_b, widen(d_pool_scale), widen(d_conv_w), d_pool_w.reshape(-1, d)], axis=0)
    n_small_rows = small_part.shape[0]
    small_part = _pad_rows(small_part, -(-n_small_rows // 8) * 8)
    small_started, token = _gather_start("ag_start_small_grads", [_into_slab("slab_small_grads", small_part, dev, F32)])

    dgu1 = _ffn_bwd_dgu("ffn1_bwd_dgu", dpre1_b, wd1, gu1, tokens + (token,))
    g_wgu1 = _ffn_bwd_wgu("ffn1_bwd_wgu", h0_b, dgu1)
    tokens = exchange(g_wgu1, ("ffn1_w_gu", g_wgu1))
    g_wd1 = _ffn_bwd_wd("ffn1_bwd_wd", act1, dpre1_b, tokens)
    tokens = exchange(g_wd1, ("ffn1_w_down", g_wd1.reshape(N_DEV, -1, d)))
    dh0 = _ffn_bwd_dh("ffn1_bwd_dh", dgu1, wgu1, dpre1, tokens)
    tokens = exchange(dh0)

    dh0_seq = dh0[:n_rows].reshape(n_seq, seq_len, d)
    grad_x = dh0_seq[:, N_META:]
    d_meta = jnp.sum(dh0_seq[:, :N_META], axis=0)

    grads, deltas, new_m, new_v = {}, {}, {}, {}
    after = tokens[0]
    for n in ["ffn2_w_down", "ffn2_w_gu", "w_out", "w_in", "ffn1_w_gu", "ffn1_w_down"]:
        part, from_sibling, started = reducing[n]
        _, from_chips = _exchange_wait(f"rs_chips_wait_{n}", started, _route_chips, after)
        g, dl, mm, vv = _adamw_big(f"adamw_{n}", part, from_sibling, from_chips, dev, chip, w_of[n][0], m_of[n][0], v_of[n][0])
        grads[n], deltas[n], new_m[n], new_v[n] = g[None], dl[None], mm[None], vv[None]
        after = g

    small_sum = _sum_parts("small_sum", _gather_finish("ag_finish_small_grads", _gather_wait("ag_wait_small_grads", small_started[0], dh0)))
    meta_sum = _sum_parts("meta_sum", _all_gather("ag_meta_grads", _into_slab("slab_meta_grads", d_meta, dev, F32)))
    o = 7 + CONV_K
    g_small = {
        "ln1_g": small_sum[0:1], "ln1_b": small_sum[1:2], "ln2_g": small_sum[2:3], "ln2_b": small_sum[3:4],
        "ln3_g": small_sum[4:5], "ln3_b": small_sum[5:6], "pool_scale": small_sum[6:7, :p_pool],
        "conv_w": lax.dynamic_slice_in_dim(small_sum[7:o, :c_conv], dev_index * (c_conv // N_DEV), c_conv // N_DEV, axis=1)[None],
        "meta_tokens": lax.dynamic_slice_in_dim(meta_sum, dev_index * wcol, wcol, axis=1),
        "pool_w": lax.dynamic_slice_in_dim(small_sum[o:n_small_rows].reshape(N_POOL_GROUPS, pg, pg),
                                           dev_index * (pg // N_DEV), pg // N_DEV, axis=1)[None],
    }
    small = ["meta_tokens", "ln1_g", "ln1_b", "conv_w", "pool_w", "pool_scale", "ln2_g", "ln2_b", "ln3_g", "ln3_b"]

    def flat(a):
        return a.reshape(-1, a.shape[-1])

    outs = _adamw_small("adamw_small", [flat(w_of[n]) for n in small], [flat(g_small[n]) for n in small],
                        [flat(m_of[n]) for n in small], [flat(v_of[n]) for n in small])
    ns = len(small)
    for i, n in enumerate(small):
        shape = w_of[n].shape
        grads[n] = g_small[n].reshape(shape)
        deltas[n], new_m[n], new_v[n] = outs[i].reshape(shape), outs[ns + i].reshape(shape), outs[2 * ns + i].reshape(shape)

    return (loss, grad_x, *[grads[n] for n in names], *[deltas[n] for n in names],
            *[new_m[n] for n in names], *[new_v[n] for n in names])
```

```python
import functools

import jax
import jax.numpy as jnp
from jax import lax
from jax.experimental import pallas as pl
from jax.experimental.pallas import tpu as pltpu

N_DEV = 8
N_CHIP = 4
N_META = 16
CONV_K = 3
POOL_WINDOWS = (2, 4, 8, 16)
N_POOL_GROUPS = len(POOL_WINDOWS)
LN_EPS = 1e-5
DEPTH = 1
ALPHA = (2.0 * DEPTH) ** 0.25
ADAM_LR = 0.001
ADAM_B1 = 0.9
ADAM_B2 = 0.999
ADAM_EPS = 1e-08
ADAM_WD = 0.01
ADAM_STEP = 10

V7X_VMEM_BYTES = 64 * 1024 * 1024
VMEM_LIMIT = V7X_VMEM_BYTES - 6 * 1024 * 1024
LANE = 128
ROW_ALIGN = 3 * LANE
TM_BIG = 1408
TM_WIDE = 704
TM_GU = 528
TK = 512
TK_TOKENS = 1408
TN = 1024
TR_LN = 128
ELEM_BLOCK_BYTES = 1 << 21
TC_MIX = LANE
EPILOGUE_ROWS = 64

NN = (((1,), (0,)), ((), ()))
NT = (((1,), (1,)), ((), ()))
TN_DIMS = (((0,), (0,)), ((), ()))
MESH = pl.DeviceIdType.MESH
BF16 = jnp.bfloat16
F32 = jnp.float32
ANY = pl.BlockSpec(memory_space=pl.ANY)
HBM = pl.BlockSpec(memory_space=pltpu.HBM)
SEM = pl.BlockSpec(memory_space=pltpu.SEMAPHORE)
EFFECT = pltpu.SideEffectType.DATAFLOW_SIDE_EFFECTING


def _tile(n, target, mult):
    best = None
    for t in range(mult, min(n, target) + 1, mult):
        if n % t == 0:
            best = t
    return n if best is None else best


def _params(sem):
    return pltpu.CompilerParams(dimension_semantics=sem, vmem_limit_bytes=VMEM_LIMIT)


def _sds(shape, dtype):
    return jax.ShapeDtypeStruct(shape, dtype)


def _row_chunks(n_rows, fn):
    ch = _tile(n_rows, EPILOGUE_ROWS, 16)

    def step(i, carry):
        fn(pl.ds(pl.multiple_of(i * ch, ch), ch))
        return carry

    lax.fori_loop(0, n_rows // ch, step, 0)


def _mm(name, grid, dims, ab, ab_specs, extras, extra_specs, out_shape, out_specs, acc_shape, epilogue, deps=()):
    nk = grid[-1]
    n_extra = len(extras)
    n_in = 2 + n_extra + len(deps)
    n_out = len(out_shape)
    kax = len(grid) - 1

    def body(*refs):
        a_ref, b_ref = refs[0], refs[1]
        ex = refs[2:2 + n_extra]
        outs = refs[n_in:n_in + n_out]
        if nk == 1:
            epilogue(lax.dot_general(a_ref[...], b_ref[...], dims, preferred_element_type=F32), ex, outs, slice(None))
            return
        acc = refs[-1]
        k = pl.program_id(kax)

        @pl.when(k == 0)
        def _():
            acc[...] = lax.dot_general(a_ref[...], b_ref[...], dims, preferred_element_type=F32)

        @pl.when(k > 0)
        def _():
            acc[...] += lax.dot_general(a_ref[...], b_ref[...], dims, preferred_element_type=F32)

        @pl.when(k == nk - 1)
        def _():
            _row_chunks(acc_shape[0], lambda rows: epilogue(acc[rows, :], ex, outs, rows))

    scratch = [] if nk == 1 else [pltpu.VMEM(acc_shape, F32)]
    sem = ("parallel",) * kax + ("arbitrary",)
    return pl.pallas_call(
        body, name=name, grid=grid, in_specs=list(ab_specs) + list(extra_specs) + [ANY] * len(deps), out_specs=list(out_specs),
        out_shape=list(out_shape), scratch_shapes=scratch, compiler_params=_params(sem),
    )(*ab, *extras, *deps)


def _silu_parts(g):
    s = 1.0 / (1.0 + jnp.exp(-g))
    return s, g * s


def _ffn_gu(name, h_b, wgu_all):
    tp, d = h_b.shape
    ns, _, ng = wgu_all.shape
    half = ns // 2
    tm, tk = _tile(tp, TM_GU, 16), _tile(d, 2 * TK, LANE)
    grid = (tp // tm, half, d // tk)
    nk = grid[-1]

    def body(h_ref, wg_ref, wu_ref, gu_ref, act_ref, acc_g, acc_u):
        k = pl.program_id(2)

        @pl.when(k == 0)
        def _():
            acc_g[...] = jnp.dot(h_ref[...], wg_ref[...], preferred_element_type=F32)
            acc_u[...] = jnp.dot(h_ref[...], wu_ref[...], preferred_element_type=F32)

        @pl.when(k > 0)
        def _():
            acc_g[...] += jnp.dot(h_ref[...], wg_ref[...], preferred_element_type=F32)
            acc_u[...] += jnp.dot(h_ref[...], wu_ref[...], preferred_element_type=F32)

        @pl.when(k == nk - 1)
        def _():
            def finish(rows):
                g = acc_g[rows, :]
                u = acc_u[rows, :]
                _, silu = _silu_parts(g)
                gu_ref[0, rows, :] = g.astype(BF16)
                gu_ref[1, rows, :] = u.astype(BF16)
                act_ref[rows, :] = (silu * u).astype(BF16)

            _row_chunks(tm, finish)

    return pl.pallas_call(
        body, name=name, grid=grid,
        in_specs=[pl.BlockSpec((tm, tk), lambda m, s, k: (m, k)),
                  pl.BlockSpec((None, tk, ng), lambda m, s, k: (s, k, 0)),
                  pl.BlockSpec((None, tk, ng), lambda m, s, k: (s + half, k, 0))],
        out_specs=[pl.BlockSpec((None, 2, tm, ng), lambda m, s, k: (s, 0, m, 0)),
                   pl.BlockSpec((None, tm, ng), lambda m, s, k: (s, m, 0))],
        out_shape=[_sds((half, 2, tp, ng), BF16), _sds((half, tp, ng), BF16)],
        scratch_shapes=[pltpu.VMEM((tm, ng), F32), pltpu.VMEM((tm, ng), F32)],
        compiler_params=_params(("parallel", "parallel", "arbitrary")),
    )(h_b, wgu_all, wgu_all)


def _ffn_down(name, act, wd4, h):
    ns, tp, ng = act.shape
    d = wd4.shape[2]
    tm, tn = _tile(tp, TM_WIDE, 16), _tile(d, TN, LANE)

    def epi(acc, ex, outs, rows):
        outs[0][rows, :] = ALPHA * ex[0][rows, :] + 0.5 * acc

    return _mm(name, (tp // tm, d // tn, ns), NN, (act, wd4),
               [pl.BlockSpec((None, tm, ng), lambda m, n, s: (s, m, 0)),
                pl.BlockSpec((None, ng, tn), lambda m, n, s: (s, 0, n))],
               (h,), [pl.BlockSpec((tm, tn), lambda m, n, s: (m, n))],
               [_sds((tp, d), F32)], [pl.BlockSpec((tm, tn), lambda m, n, s: (m, n))], (tm, tn), epi)[0]


def _ffn_bwd_dgu(name, dp_b, wd4, gu, deps=()):
    tp, d = dp_b.shape
    ns, ng, _ = wd4.shape
    tm, tk = _tile(tp, TM_WIDE, 16), _tile(d, 2 * TK, LANE)

    def epi(acc, ex, outs, rows):
        g = ex[0][0, rows, :].astype(F32)
        u = ex[0][1, rows, :].astype(F32)
        da = 0.5 * acc
        s, silu = _silu_parts(g)
        outs[0][0, rows, :] = (da * u * (s + silu * (1.0 - s))).astype(BF16)
        outs[0][1, rows, :] = (da * silu).astype(BF16)

    return _mm(name, (tp // tm, ns, d // tk), NT, (dp_b, wd4),
               [pl.BlockSpec((tm, tk), lambda m, s, k: (m, k)),
                pl.BlockSpec((None, ng, tk), lambda m, s, k: (s, 0, k))],
               (gu,), [pl.BlockSpec((None, 2, tm, ng), lambda m, s, k: (s, 0, m, 0))],
               [_sds((ns, 2, tp, ng), BF16)], [pl.BlockSpec((None, 2, tm, ng), lambda m, s, k: (s, 0, m, 0))],
               (tm, ng), epi, deps)[0]


def _ffn_bwd_wd(name, act, dp_b, deps=()):
    ns, tp, ng = act.shape
    d = dp_b.shape[1]
    tkt, tn = _tile(tp, TK_TOKENS, LANE), _tile(d, TN, LANE)

    def epi(acc, ex, outs, rows):
        outs[0][rows, :] = (0.5 * acc).astype(BF16)

    return _mm(name, (ns, d // tn, tp // tkt), TN_DIMS, (act, dp_b),
               [pl.BlockSpec((None, tkt, ng), lambda s, n, t: (s, t, 0)),
                pl.BlockSpec((tkt, tn), lambda s, n, t: (t, n))],
               (), [], [_sds((ns, ng, d), BF16)], [pl.BlockSpec((None, ng, tn), lambda s, n, t: (s, 0, n))],
               (ng, tn), epi, deps)[0]


def _ffn_bwd_wgu(name, h_b, dgu, deps=()):
    tp, d = h_b.shape
    ns, _, _, ng = dgu.shape
    tkt, tmd = _tile(tp, TK_TOKENS, LANE), _tile(d, TN, LANE)

    def epi(acc, ex, outs, rows):
        outs[0][rows, :] = acc.astype(BF16)

    return _mm(name, (ns, 2, d // tmd, tp // tkt), TN_DIMS, (h_b, dgu),
               [pl.BlockSpec((tkt, tmd), lambda s, j, i, t: (t, i)),
                pl.BlockSpec((None, None, tkt, ng), lambda s, j, i, t: (s, j, t, 0))],
               (), [], [_sds((2 * ns, d, ng), BF16)],
               [pl.BlockSpec((None, tmd, ng), lambda s, j, i, t: (j * ns + s, i, 0))], (tmd, ng), epi, deps)[0]


def _ffn_bwd_dh(name, dgu, wgu_all, dp, deps=()):
    ns, _, tp, ng = dgu.shape
    d = wgu_all.shape[1]
    tm, tn = _tile(tp, TM_WIDE, 16), _tile(d, TN, LANE)

    def epi(acc, ex, outs, rows):
        outs[0][rows, :] = ALPHA * ex[0][rows, :] + acc

    return _mm(name, (tp // tm, d // tn, 2 * ns), NT, (dgu, wgu_all),
               [pl.BlockSpec((None, None, tm, ng), lambda m, n, j: (j % ns, j // ns, m, 0)),
                pl.BlockSpec((None, tn, ng), lambda m, n, j: (j, n, 0))],
               (dp,), [pl.BlockSpec((tm, tn), lambda m, n, j: (m, n))],
               [_sds((tp, d), F32)], [pl.BlockSpec((tm, tn), lambda m, n, j: (m, n))], (tm, tn), epi, deps)[0]


def _ln_stats(x):
    mu = jnp.mean(x, axis=-1, keepdims=True)
    xc = x - mu
    var = jnp.mean(xc * xc, axis=-1, keepdims=True)
    rstd = lax.rsqrt(var + LN_EPS)
    return xc * rstd, rstd


def _ln_bwd_rows(dy, xhat, rstd, g):
    dxh = dy * g
    m1 = jnp.mean(dxh, axis=-1, keepdims=True)
    m2 = jnp.mean(dxh * xhat, axis=-1, keepdims=True)
    return rstd * (dxh - m1 - xhat * m2)


def _ln_fwd(name, pre, g, b):
    tp, d = pre.shape
    tr = _tile(tp, TR_LN, 16)

    def body(x_ref, g_ref, b_ref, y_ref, yb_ref):
        xhat, _ = _ln_stats(x_ref[...])
        y = xhat * g_ref[...] + b_ref[...]
        y_ref[...] = y
        yb_ref[...] = y.astype(BF16)

    row = pl.BlockSpec((tr, d), lambda i: (i, 0))
    vec = pl.BlockSpec((1, d), lambda i: (0, 0))
    return pl.pallas_call(body, name=name, grid=(tp // tr,), in_specs=[row, vec, vec], out_specs=[row, row],
                          out_shape=[_sds((tp, d), F32), _sds((tp, d), BF16)],
                          compiler_params=_params(("parallel",)))(pre, g, b)


def _accumulate(i, ref, val):
    @pl.when(i == 0)
    def _():
        ref[...] = val

    @pl.when(i > 0)
    def _():
        ref[...] += val


def _ln_bwd(name, pre, dy, g):
    tp, d = pre.shape
    tr = _tile(tp, TR_LN, 16)

    def body(x_ref, dy_ref, g_ref, dx_ref, dxb_ref, dg_ref, db_ref):
        i = pl.program_id(0)
        xhat, rstd = _ln_stats(x_ref[...])
        dy = dy_ref[...]
        dx = _ln_bwd_rows(dy, xhat, rstd, g_ref[...])
        dx_ref[...] = dx
        dxb_ref[...] = dx.astype(BF16)
        _accumulate(i, dg_ref, jnp.sum(dy * xhat, axis=0, keepdims=True))
        _accumulate(i, db_ref, jnp.sum(dy, axis=0, keepdims=True))

    row = pl.BlockSpec((tr, d), lambda i: (i, 0))
    vec = pl.BlockSpec((1, d), lambda i: (0, 0))
    return pl.pallas_call(body, name=name, grid=(tp // tr,), in_specs=[row, row, vec], out_specs=[row, row, vec, vec],
                          out_shape=[_sds((tp, d), F32), _sds((tp, d), BF16), _sds((1, d), F32), _sds((1, d), F32)],
                          compiler_params=_params(("arbitrary",)))(pre, dy, g)


def _ln_loss_bwd(name, pre, tgt, g, b, n_rows, seq_len):
    tp, d = pre.shape
    tr = _tile(tp, TR_LN, 16)
    n_seq = n_rows // seq_len

    def body(x_ref, t_ref, g_ref, b_ref, dx_ref, dxb_ref, dg_ref, db_ref, sq_ref):
        i = pl.program_id(0)
        xhat, rstd = _ln_stats(x_ref[...])
        gain = g_ref[...]
        y = xhat * gain + b_ref[...]
        r = i * tr + lax.broadcasted_iota(jnp.int32, (tr, 1), 0)
        pos = r
        for s in range(1, n_seq):
            pos = jnp.where(r >= s * seq_len, r - s * seq_len, pos)
        live = jnp.logical_and(r < n_rows, pos >= N_META)
        err = jnp.where(live, y - t_ref[...], 0.0)
        dy = err * (1.0 / d)
        dx = _ln_bwd_rows(dy, xhat, rstd, gain)
        dx_ref[...] = dx
        dxb_ref[...] = dx.astype(BF16)
        _accumulate(i, dg_ref, jnp.sum(dy * xhat, axis=0, keepdims=True))
        _accumulate(i, db_ref, jnp.sum(dy, axis=0, keepdims=True))
        _accumulate(i, sq_ref, jnp.sum(err * err, axis=0, keepdims=True))

    row = pl.BlockSpec((tr, d), lambda i: (i, 0))
    vec = pl.BlockSpec((1, d), lambda i: (0, 0))
    return pl.pallas_call(
        body, name=name, grid=(tp // tr,), in_specs=[row, row, vec, vec], out_specs=[row, row, vec, vec, vec],
        out_shape=[_sds((tp, d), F32), _sds((tp, d), BF16), _sds((1, d), F32), _sds((1, d), F32), _sds((1, d), F32)],
        compiler_params=_params(("arbitrary",)))(pre, tgt, g, b)


def _proj_in(name, h_b, win_all):
    tp, d = h_b.shape
    ns, _, ni = win_all.shape
    tm, tk = _tile(tp, TM_BIG, 16), _tile(d, 4 * TK, LANE)

    def epi(acc, ex, outs, rows):
        outs[0][rows, :] = acc

    return _mm(name, (tp // tm, ns, d // tk), NN, (h_b, win_all),
               [pl.BlockSpec((tm, tk), lambda m, j, k: (m, k)),
                pl.BlockSpec((None, tk, ni), lambda m, j, k: (j, k, 0))],
               (), [], [_sds((tp, ns * ni), F32)], [pl.BlockSpec((tm, ni), lambda m, j, k: (m, j))], (tm, ni), epi)[0]


def _positions(tp, n_rows, seq_len):
    r = lax.broadcasted_iota(jnp.int32, (tp, 1), 0)
    pos = r
    for s in range(1, n_rows // seq_len):
        pos = jnp.where(r >= s * seq_len, r - s * seq_len, pos)
    return pos


def _shift_down(x, s, pos):
    return jnp.where(pos >= s, pltpu.roll(x, s, 0), 0.0)


def _shift_up(x, s, pos, seq_len):
    return jnp.where(pos + s < seq_len, pltpu.roll(x, x.shape[0] - s, 0), 0.0)


def _conv_fwd(name, u, conv_w, n_rows, seq_len):
    tp = u.shape[0]
    c = conv_w.shape[1]
    tc = _tile(c, TC_MIX, LANE)
    nb = c // tc

    def body(gb_ref, gc_ref, xi_ref, w_ref, y_ref):
        pos = _positions(tp, n_rows, seq_len)
        v = gc_ref[...] * xi_ref[...]
        w = w_ref[...]
        y = _shift_down(v, 2, pos) * w[0:1]
        y = y + _shift_down(v, 1, pos) * w[1:2]
        y = y + v * w[2:3]
        y_ref[...] = (gb_ref[...] * y).astype(BF16)

    col = lambda off: pl.BlockSpec((tp, tc), lambda i: (0, off + i))
    return pl.pallas_call(body, name=name, grid=(nb,), in_specs=[col(0), col(nb), col(2 * nb), pl.BlockSpec((CONV_K, tc), lambda i: (0, i))],
                          out_specs=pl.BlockSpec((tp, tc), lambda i: (0, i)), out_shape=_sds((tp, c), BF16),
                          compiler_params=_params(("parallel",)))(u, u, u, conv_w)


def _conv_bwd(name, u, conv_w, dy, n_rows, seq_len):
    tp = u.shape[0]
    c = conv_w.shape[1]
    tc = _tile(c, TC_MIX, LANE)
    nb = c // tc

    def body(gb_ref, gc_ref, xi_ref, w_ref, dy_ref, dgb_ref, dgc_ref, dxi_ref, dw_ref):
        pos = _positions(tp, n_rows, seq_len)
        gc, xi = gc_ref[...], xi_ref[...]
        v = gc * xi
        w = w_ref[...]
        v2, v1 = _shift_down(v, 2, pos), _shift_down(v, 1, pos)
        conv = v2 * w[0:1]
        conv = conv + v1 * w[1:2]
        conv = conv + v * w[2:3]
        dyc = dy_ref[...]
        dgb_ref[...] = (dyc * conv).astype(BF16)
        dconv = dyc * gb_ref[...]
        dv = dconv * w[2:3] + _shift_up(dconv, 1, pos, seq_len) * w[1:2] + _shift_up(dconv, 2, pos, seq_len) * w[0:1]
        dgc_ref[...] = (dv * xi).astype(BF16)
        dxi_ref[...] = (dv * gc).astype(BF16)
        dw_ref[0:1, :] = jnp.sum(dconv * v2, axis=0, keepdims=True)
        dw_ref[1:2, :] = jnp.sum(dconv * v1, axis=0, keepdims=True)
        dw_ref[2:3, :] = jnp.sum(dconv * v, axis=0, keepdims=True)

    col = lambda off: pl.BlockSpec((tp, tc), lambda i: (0, off + i))
    wspec = pl.BlockSpec((CONV_K, tc), lambda i: (0, i))
    return pl.pallas_call(body, name=name, grid=(nb,), in_specs=[col(0), col(nb), col(2 * nb), wspec, col(0)],
                          out_specs=[col(0), col(0), col(0), wspec],
                          out_shape=[_sds((tp, c), BF16)] * 3 + [_sds((CONV_K, c), F32)],
                          compiler_params=_params(("parallel",)))(u, u, u, conv_w, dy)


def _window_select(group, parts):
    out = parts[-1]
    for gi in range(len(parts) - 2, -1, -1):
        out = jnp.where(group == gi, parts[gi], out)
    return out


def _pool_fwd(name, u, col0, p, pg, n_rows, seq_len):
    tp = u.shape[0]
    tc = _tile(pg, TC_MIX, LANE)
    per_group = pg // tc

    def body(z_ref, d_ref):
        group = pl.program_id(0) // per_group
        pos = _positions(tp, n_rows, seq_len)
        z = z_ref[...]
        sums, s, w = [], z, 1
        for _ in POOL_WINDOWS:
            s = s + _shift_down(s, w, pos)
            w *= 2
            sums.append(s)
        total = _window_select(group, sums)
        count = jnp.minimum(pos + 1, 2 << group).astype(F32)
        d_ref[...] = (total / count - z).astype(BF16)

    return pl.pallas_call(body, name=name, grid=(p // tc,), in_specs=[pl.BlockSpec((tp, tc), lambda i: (0, col0 // tc + i))],
                          out_specs=pl.BlockSpec((tp, tc), lambda i: (0, i)), out_shape=_sds((tp, p), BF16),
                          compiler_params=_params(("parallel",)))(u)


def _pool_bwd(name, dd, pg, n_rows, seq_len):
    tp, p = dd.shape
    tc = _tile(pg, TC_MIX, LANE)
    per_group = pg // tc

    def body(dd_ref, dz_ref):
        group = pl.program_id(0) // per_group
        pos = _positions(tp, n_rows, seq_len)
        dd_v = dd_ref[...]
        count = jnp.minimum(pos + 1, 2 << group).astype(F32)
        sums, s, w = [], dd_v / count, 1
        for _ in POOL_WINDOWS:
            s = s + _shift_up(s, w, pos, seq_len)
            w *= 2
            sums.append(s)
        dz_ref[...] = (_window_select(group, sums) - dd_v).astype(BF16)

    spec = pl.BlockSpec((tp, tc), lambda i: (0, i))
    return pl.pallas_call(body, name=name, grid=(p // tc,), in_specs=[spec], out_specs=spec, out_shape=_sds((tp, p), BF16),
                          compiler_params=_params(("parallel",)))(dd)


def _pool_mix(name, dpool, pool_w_b, scale):
    tp, p = dpool.shape
    ng, pg, _ = pool_w_b.shape
    tm = _tile(tp, TM_BIG, 16)

    def epi(acc, ex, outs, rows):
        outs[0][rows, :] = acc
        outs[1][rows, :] = (acc * ex[0][...]).astype(BF16)

    blk = pl.BlockSpec((tm, pg), lambda m, g, k: (m, g))
    return _mm(name, (tp // tm, ng, 1), NN, (dpool, pool_w_b), [blk, pl.BlockSpec((None, pg, pg), lambda m, g, k: (g, 0, 0))],
               (scale,), [pl.BlockSpec((1, pg), lambda m, g, k: (0, g))],
               [_sds((tp, p), F32), _sds((tp, p), BF16)], [blk, blk], None, epi)


def _pool_scale_bwd(name, dy, col_block, ypre, scale):
    tp, p = ypre.shape
    tr = _tile(tp, TR_LN, 16)

    def body(dy_ref, yp_ref, s_ref, o_ref, ds_ref):
        i = pl.program_id(0)
        dyp = dy_ref[...]
        o_ref[...] = (dyp * s_ref[...]).astype(BF16)
        _accumulate(i, ds_ref, jnp.sum(dyp * yp_ref[...], axis=0, keepdims=True))

    row = pl.BlockSpec((tr, p), lambda i: (i, 0))
    vec = pl.BlockSpec((1, p), lambda i: (0, 0))
    return pl.pallas_call(body, name=name, grid=(tp // tr,), in_specs=[pl.BlockSpec((tr, p), lambda i: (i, col_block)), row, vec],
                          out_specs=[row, vec], out_shape=[_sds((tp, p), BF16), _sds((1, p), F32)],
                          compiler_params=_params(("arbitrary",)))(dy, ypre, scale)


def _pool_mix_bwd_in(name, dyps, pool_w_b):
    tp, p = dyps.shape
    ng, pg, _ = pool_w_b.shape
    tm = _tile(tp, TM_BIG, 16)

    def epi(acc, ex, outs, rows):
        outs[0][rows, :] = acc

    blk = pl.BlockSpec((tm, pg), lambda m, g, k: (m, g))
    return _mm(name, (tp // tm, ng, 1), NT, (dyps, pool_w_b), [blk, pl.BlockSpec((None, pg, pg), lambda m, g, k: (g, 0, 0))],
               (), [], [_sds((tp, p), F32)], [blk], None, epi)[0]


def _pool_mix_bwd_w(name, dpool, dyps, pg):
    tp, p = dpool.shape
    ng = p // pg
    tkt = _tile(tp, TM_BIG, LANE)

    def epi(acc, ex, outs, rows):
        outs[0][rows, :] = acc

    blk = pl.BlockSpec((tkt, pg), lambda g, t: (t, g))
    return _mm(name, (ng, tp // tkt), TN_DIMS, (dpool, dyps), [blk, blk], (), [],
               [_sds((ng, pg, pg), F32)], [pl.BlockSpec((None, pg, pg), lambda g, t: (g, 0, 0))], (pg, pg), epi)[0]


def _proj_out(name, y, wout_all, h):
    tp, c = y.shape
    d = wout_all.shape[2]
    w = wout_all.reshape(c, d)
    tm, tn, tk = _tile(tp, TM_BIG, 16), _tile(d, TN, LANE), _tile(c, 4 * TK, LANE)

    def epi(acc, ex, outs, rows):
        outs[0][rows, :] = ALPHA * ex[0][rows, :] + acc

    mn = pl.BlockSpec((tm, tn), lambda m, n, k: (m, n))
    return _mm(name, (tp // tm, d // tn, c // tk), NN, (y, w),
               [pl.BlockSpec((tm, tk), lambda m, n, k: (m, k)), pl.BlockSpec((tk, tn), lambda m, n, k: (k, n))],
               (h,), [mn], [_sds((tp, d), F32)], [mn], (tm, tn), epi)[0]


def _proj_out_bwd_y(name, dp_b, wout_all, deps=()):
    tp, d = dp_b.shape
    ns, ro, _ = wout_all.shape
    tm, tk = _tile(tp, TM_BIG, 16), d

    def epi(acc, ex, outs, rows):
        outs[0][rows, :] = acc

    return _mm(name, (tp // tm, ns, d // tk), NT, (dp_b, wout_all),
               [pl.BlockSpec((tm, tk), lambda m, j, k: (m, k)), pl.BlockSpec((None, ro, tk), lambda m, j, k: (j, 0, k))],
               (), [], [_sds((tp, ns * ro), F32)], [pl.BlockSpec((tm, ro), lambda m, j, k: (m, j))], (tm, ro), epi, deps)[0]


def _proj_out_bwd_w(name, y, dp_b, ns):
    tp, c = y.shape
    d = dp_b.shape[1]
    ro = c // ns
    tkt, tn = _tile(tp, TM_BIG, LANE), _tile(d, 2 * TN, LANE)

    def epi(acc, ex, outs, rows):
        outs[0][rows, :] = acc.astype(BF16)

    return _mm(name, (ns, d // tn, tp // tkt), TN_DIMS, (y, dp_b),
               [pl.BlockSpec((tkt, ro), lambda j, n, t: (t, j)), pl.BlockSpec((tkt, tn), lambda j, n, t: (t, n))],
               (), [], [_sds((ns, ro, d), BF16)], [pl.BlockSpec((None, ro, tn), lambda j, n, t: (j, 0, n))], (ro, tn), epi)[0]


def _proj_in_bwd_w(name, h_b, du, ns, deps=()):
    tp, d = h_b.shape
    ni = du.shape[1] // ns
    tkt, tmd = _tile(tp, TK_TOKENS, LANE), _tile(d, 2 * TN, LANE)

    def epi(acc, ex, outs, rows):
        outs[0][rows, :] = acc.astype(BF16)

    return _mm(name, (ns, d // tmd, tp // tkt), TN_DIMS, (h_b, du),
               [pl.BlockSpec((tkt, tmd), lambda j, i, t: (t, i)), pl.BlockSpec((tkt, ni), lambda j, i, t: (t, j))],
               (), [], [_sds((ns, d, ni), BF16)], [pl.BlockSpec((None, tmd, ni), lambda j, i, t: (j, i, 0))], (tmd, ni), epi, deps)[0]


def _proj_in_bwd_h(name, du, win_all, dp, deps=()):
    tp = du.shape[0]
    ns, d, ni = win_all.shape
    tm, tn = _tile(tp, TM_BIG, 16), _tile(d, TN, LANE)

    def epi(acc, ex, outs, rows):
        outs[0][rows, :] = ALPHA * ex[0][rows, :] + acc

    mn = pl.BlockSpec((tm, tn), lambda m, n, j: (m, n))
    return _mm(name, (tp // tm, d // tn, ns), NT, (du, win_all),
               [pl.BlockSpec((tm, ni), lambda m, n, j: (m, j)), pl.BlockSpec((None, tn, ni), lambda m, n, j: (j, n, 0))],
               (dp,), [mn], [_sds((tp, d), F32)], [mn], (tm, tn), epi, deps)[0]


def _place():
    return lax.axis_index("x"), lax.axis_index("y"), lax.axis_index("c")


def _hbm(a):
    return pltpu.with_memory_space_constraint(a, pltpu.HBM)


def _token_shape():
    return _sds((8, LANE), F32)


def _gather_peers(x, y, c):
    return [(x, y, 1 - c), (1 - x, y, c), (x, 1 - y, c), (1 - x, 1 - y, c)]


def _into_slab(name, w, dev, dtype):
    r, c = w.shape
    tr = _tile(r, max(16, ELEM_BLOCK_BYTES // (4 * c)), 16 if dtype == BF16 else 8)

    def body(dev_ref, w_ref, o_ref):
        o_ref[...] = w_ref[...].astype(dtype)

    return pl.pallas_call(
        body, name=name,
        grid_spec=pltpu.PrefetchScalarGridSpec(
            num_scalar_prefetch=1, grid=(r // tr,), in_specs=[pl.BlockSpec((tr, c), lambda i, dev_ref: (i, 0))],
            out_specs=pl.BlockSpec((None, tr, c), lambda i, dev_ref: (dev_ref[0], i, 0))),
        out_shape=_sds((N_DEV, r, c), dtype), compiler_params=_params(("parallel",)),
    )(dev, w)


def _gather_start(name, lands, after=()):
    n = len(lands)
    n_peer = N_CHIP
    n_in = n + len(after)

    def body(*refs):
        land_refs = refs[:n]
        send_sems, recv_sems = refs[n_in:n_in + n], refs[n_in + n:n_in + 2 * n]
        token = refs[-1]
        x, y, c = _place()
        me = 4 * x + 2 * y + c
        for i in range(n):
            for k, peer in enumerate(_gather_peers(x, y, c)):
                pltpu.make_async_remote_copy(
                    src_ref=land_refs[i].at[me], dst_ref=land_refs[i].at[me], send_sem=send_sems[i].at[k],
                    recv_sem=recv_sems[i].at[k], device_id=peer, device_id_type=MESH).start()
        token[...] = jnp.zeros_like(token)

    sem = pltpu.SemaphoreType.DMA((n_peer,))
    out = pl.pallas_call(
        body, name=name,
        out_shape=[sem] * (2 * n) + [pltpu.HBM(l.shape, l.dtype) for l in lands] + [_token_shape()],
        in_specs=[HBM] * n + [ANY] * len(after),
        out_specs=[SEM] * (2 * n) + [HBM] * n + [pl.BlockSpec(memory_space=pltpu.VMEM)],
        input_output_aliases={i: 2 * n + i for i in range(n)},
        compiler_params=pltpu.CompilerParams(has_side_effects=EFFECT),
    )(*[_hbm(l) for l in lands], *after)
    per = [(out[i], out[n + i], out[2 * n + i]) for i in range(n)]
    return per, out[-1]


def _gather_wait(name, started, after):
    send_sems, recv_sems, land = started
    after = after if isinstance(after, tuple) else (after,)

    def body(land_ref, send_ref, recv_ref, *rest):
        x, y, c = _place()
        for k, (px, py, pc) in enumerate(_gather_peers(x, y, c)):
            cp = pltpu.make_async_remote_copy(
                src_ref=land_ref.at[4 * x + 2 * y + c], dst_ref=land_ref.at[4 * px + 2 * py + pc], send_sem=send_ref.at[k],
                recv_sem=recv_ref.at[k], device_id=(px, py, pc), device_id_type=MESH)
            cp.wait_send()
            cp.wait_recv()

    return pl.pallas_call(
        body, name=name, out_shape=pltpu.HBM(land.shape, land.dtype),
        in_specs=(HBM, SEM, SEM) + (ANY,) * len(after), out_specs=HBM, input_output_aliases={0: 0},
        compiler_params=pltpu.CompilerParams(has_side_effects=EFFECT),
    )(land, send_sems, recv_sems, *after)


def _gather_finish(name, land, deps=()):
    def body(land_ref, *rest):
        out_ref, send_sems, recv_sems = rest[len(deps):]
        x, y, c = _place()
        copies = []
        for k, (px, py) in enumerate([(1 - x, y), (x, 1 - y), (1 - x, 1 - y)]):
            slab = 4 * px + 2 * py + c
            copies.append(pltpu.make_async_remote_copy(
                src_ref=land_ref.at[slab], dst_ref=out_ref.at[slab], send_sem=send_sems.at[k], recv_sem=recv_sems.at[k],
                device_id=(x, y, 1 - c), device_id_type=MESH))
        for cp in copies:
            cp.start()
        for cp in copies:
            cp.wait()

    return pl.pallas_call(
        body, name=name, out_shape=_sds(land.shape, land.dtype), in_specs=[ANY] + [ANY] * len(deps), out_specs=ANY,
        input_output_aliases={0: 0},
        scratch_shapes=[pltpu.SemaphoreType.DMA((N_CHIP - 1,)), pltpu.SemaphoreType.DMA((N_CHIP - 1,))],
    )(land, *deps)


def _forward_start(name, land, after):
    n_copy = N_CHIP - 1

    def body(land_ref, after_ref, send_sems, recv_sems, land_out, token):
        x, y, c = _place()
        for k, (px, py) in enumerate([(1 - x, y), (x, 1 - y), (1 - x, 1 - y)]):
            slab = 4 * px + 2 * py + c
            pltpu.make_async_remote_copy(
                src_ref=land_ref.at[slab], dst_ref=land_ref.at[slab], send_sem=send_sems.at[k], recv_sem=recv_sems.at[k],
                device_id=(x, y, 1 - c), device_id_type=MESH).start()
        token[...] = jnp.zeros_like(token)

    sem = pltpu.SemaphoreType.DMA((n_copy,))
    out = pl.pallas_call(
        body, name=name, out_shape=[sem, sem, pltpu.HBM(land.shape, land.dtype), _token_shape()],
        in_specs=[HBM, ANY], out_specs=[SEM, SEM, HBM, pl.BlockSpec(memory_space=pltpu.VMEM)],
        input_output_aliases={0: 2}, compiler_params=pltpu.CompilerParams(has_side_effects=EFFECT),
    )(_hbm(land), after)
    return out[:3], out[3]


def _forward_wait(name, started, after):
    send_sems, recv_sems, land = started

    def body(land_ref, send_ref, recv_ref, after_ref, land_out):
        x, y, c = _place()
        for k, (px, py) in enumerate([(1 - x, y), (x, 1 - y), (1 - x, 1 - y)]):
            cp = pltpu.make_async_remote_copy(
                src_ref=land_ref.at[4 * px + 2 * py + c], dst_ref=land_ref.at[4 * px + 2 * py + (1 - c)],
                send_sem=send_ref.at[k], recv_sem=recv_ref.at[k], device_id=(x, y, 1 - c), device_id_type=MESH)
            cp.wait_send()
            cp.wait_recv()

    return pl.pallas_call(
        body, name=name, out_shape=pltpu.HBM(land.shape, land.dtype),
        in_specs=(HBM, SEM, SEM, ANY), out_specs=HBM, input_output_aliases={0: 0},
        compiler_params=pltpu.CompilerParams(has_side_effects=EFFECT),
    )(land, send_sems, recv_sems, after)


def _route_sibling(x, y, c):
    return [(2 * j + (1 - c), (x, y, 1 - c)) for j in range(N_CHIP)]


def _route_chips(x, y, c):
    return [(k, (px, py, c)) for k, (px, py) in enumerate([(1 - x, y), (x, 1 - y), (1 - x, 1 - y)])]


def _exchange_start(name, src, route):
    n_copy = len(route(0, 0, 0))

    def body(s_ref, land_ref, send_sems, recv_sems, s_out, land_out, token):
        for k, (slab, peer) in enumerate(route(*_place())):
            pltpu.make_async_remote_copy(
                src_ref=s_ref.at[slab], dst_ref=land_ref.at[k], send_sem=send_sems.at[k], recv_sem=recv_sems.at[k],
                device_id=peer, device_id_type=MESH).start()
        token[...] = jnp.zeros_like(token)

    land = lax.empty((n_copy,) + src.shape[1:], src.dtype)
    sem = pltpu.SemaphoreType.DMA((n_copy,))
    out = pl.pallas_call(
        body, name=name,
        out_shape=[sem, sem, pltpu.HBM(src.shape, src.dtype), pltpu.HBM(land.shape, land.dtype), _token_shape()],
        in_specs=[HBM, HBM], out_specs=[SEM, SEM, HBM, HBM, pl.BlockSpec(memory_space=pltpu.VMEM)],
        input_output_aliases={0: 2, 1: 3}, compiler_params=pltpu.CompilerParams(has_side_effects=EFFECT),
    )(_hbm(src), _hbm(land))
    return out[:4], out[4]


def _exchange_wait(name, started, route, after):
    send_sems, recv_sems, src, land = started

    def body(s_ref, land_ref, send_ref, recv_ref, after_ref, s_out, land_out):
        for k, (slab, peer) in enumerate(route(*_place())):
            cp = pltpu.make_async_remote_copy(
                src_ref=s_ref.at[slab], dst_ref=land_ref.at[k], send_sem=send_ref.at[k], recv_sem=recv_ref.at[k],
                device_id=peer, device_id_type=MESH)
            cp.wait_send()
            cp.wait_recv()

    return pl.pallas_call(
        body, name=name, out_shape=(pltpu.HBM(src.shape, src.dtype), pltpu.HBM(land.shape, land.dtype)),
        in_specs=(HBM, HBM, SEM, SEM, ANY), out_specs=(HBM, HBM), input_output_aliases={0: 0, 1: 1},
        compiler_params=pltpu.CompilerParams(has_side_effects=EFFECT),
    )(src, land, send_sems, recv_sems, after)


def _all_gather(name, land):
    def body(land_ref, out_ref, send_sems, recv_sems):
        x, y, c = _place()
        me, sibling = (x, y, c), (x, y, 1 - c)
        chips = [(1 - x, y), (x, 1 - y), (1 - x, 1 - y)]

        def copy(k, block, to):
            slab = 4 * block[0] + 2 * block[1] + block[2]
            return pltpu.make_async_remote_copy(
                src_ref=land_ref.at[slab], dst_ref=out_ref.at[slab],
                send_sem=send_sems.at[k], recv_sem=recv_sems.at[k], device_id=to, device_id_type=MESH)

        first = [copy(0, me, sibling)] + [copy(1 + j, me, (*chip, c)) for j, chip in enumerate(chips)]
        for cp in first:
            cp.start()
        passed = [copy(4 + j, (*chip, c), sibling) for j, chip in enumerate(chips)]
        for j, chip in enumerate(chips):
            copy(1 + j, (*chip, c), me).wait_recv()
            passed[j].start()
        copy(0, sibling, me).wait_recv()
        for j, chip in enumerate(chips):
            copy(4 + j, (*chip, 1 - c), me).wait_recv()
        for cp in first + passed:
            cp.wait_send()

    return pl.pallas_call(
        body, name=name, out_shape=_sds(land.shape, land.dtype), in_specs=[ANY], out_specs=ANY, input_output_aliases={0: 0},
        scratch_shapes=[pltpu.SemaphoreType.DMA((N_DEV - 1,)), pltpu.SemaphoreType.DMA((N_DEV - 1,))],
    )(land)


def _add_sibling(name, part, recv, place):
    _, r, c = part.shape
    tr = _tile(r, max(16, 2 * ELEM_BLOCK_BYTES // (2 * c)), 16)
    n_out = N_CHIP - 1

    def chip_of(k, x_ref, y_ref):
        px = jnp.where(k == 1, x_ref[0], 1 - x_ref[0])
        py = jnp.where(k == 0, y_ref[0], 1 - y_ref[0])
        return 2 * px + py

    def body(x_ref, y_ref, c_ref, p_ref, r_ref, o_ref):
        o_ref[...] = (p_ref[...].astype(F32) + r_ref[...].astype(F32)).astype(BF16)

    return pl.pallas_call(
        body, name=name,
        grid_spec=pltpu.PrefetchScalarGridSpec(
            num_scalar_prefetch=3, grid=(n_out, r // tr),
            in_specs=[pl.BlockSpec((None, tr, c), lambda k, i, x_ref, y_ref, c_ref: (2 * chip_of(k, x_ref, y_ref) + c_ref[0], i, 0)),
                      pl.BlockSpec((None, tr, c), lambda k, i, x_ref, y_ref, c_ref: (chip_of(k, x_ref, y_ref), i, 0))],
            out_specs=pl.BlockSpec((None, tr, c), lambda k, i, x_ref, y_ref, c_ref: (k, i, 0))),
        out_shape=_sds((n_out, r, c), BF16), compiler_params=_params(("parallel", "parallel")),
    )(*place, part, recv)


def _adamw_math(w, g, m, v):
    m = ADAM_B1 * m + (1.0 - ADAM_B1) * g
    v = ADAM_B2 * v + (1.0 - ADAM_B2) * (g * g)
    m_hat = m / (1.0 - ADAM_B1 ** ADAM_STEP)
    v_hat = v / (1.0 - ADAM_B2 ** ADAM_STEP)
    delta = -ADAM_LR * (m_hat / (jnp.sqrt(v_hat) + ADAM_EPS) + ADAM_WD * w)
    return delta, m, v


def _adamw_big(name, part, from_sibling, from_chips, dev, chip, w, m, v):
    r, c = w.shape
    tr = _tile(r, max(16, ELEM_BLOCK_BYTES // (4 * c)), 16)

    def body(dev_ref, chip_ref, p_ref, s_ref, r_ref, w_ref, m_ref, v_ref, g_out, d_out, m_out, v_out):
        g = p_ref[...].astype(F32) + s_ref[...].astype(F32)
        for k in range(N_CHIP - 1):
            g = g + r_ref[k].astype(F32)
        delta, m_new, v_new = _adamw_math(w_ref[...], g, m_ref[...], v_ref[...])
        g_out[...] = g
        d_out[...] = delta
        m_out[...] = m_new
        v_out[...] = v_new

    blk = pl.BlockSpec((tr, c), lambda i, dev_ref, chip_ref: (i, 0))
    return pl.pallas_call(
        body, name=name,
        grid_spec=pltpu.PrefetchScalarGridSpec(
            num_scalar_prefetch=2, grid=(r // tr,),
            in_specs=[pl.BlockSpec((None, tr, c), lambda i, dev_ref, chip_ref: (dev_ref[0], i, 0)),
                      pl.BlockSpec((None, tr, c), lambda i, dev_ref, chip_ref: (chip_ref[0], i, 0)),
                      pl.BlockSpec((N_CHIP - 1, tr, c), lambda i, dev_ref, chip_ref: (0, i, 0)), blk, blk, blk],
            out_specs=[blk, blk, blk, blk]),
        out_shape=[_sds((r, c), F32)] * 4, compiler_params=_params(("parallel",)),
    )(dev, chip, part, from_sibling, from_chips, w, m, v)


def _sum_parts(name, parts):
    n, r, c = parts.shape
    tr = _tile(r, max(8, ELEM_BLOCK_BYTES // (4 * c)), 8)

    def body(p_ref, o_ref):
        acc = p_ref[0]
        for k in range(1, n):
            acc = acc + p_ref[k]
        o_ref[...] = acc

    return pl.pallas_call(body, name=name, grid=(r // tr,), in_specs=[pl.BlockSpec((n, tr, c), lambda i: (0, i, 0))],
                          out_specs=pl.BlockSpec((tr, c), lambda i: (i, 0)), out_shape=_sds((r, c), F32),
                          compiler_params=_params(("parallel",)))(parts)


def _adamw_small(name, ws, gs, ms, vs):
    n = len(ws)

    def body(*refs):
        ins, outs = refs[:4 * n], refs[4 * n:]
        for i in range(n):
            delta, m_new, v_new = _adamw_math(ins[i][...], ins[n + i][...], ins[2 * n + i][...], ins[3 * n + i][...])
            outs[i][...] = delta
            outs[n + i][...] = m_new
            outs[2 * n + i][...] = v_new

    shapes = [_sds(w.shape, F32) for w in ws]
    return pl.pallas_call(body, name=name, out_shape=shapes * 3)(*ws, *gs, *ms, *vs)


def _pad_rows(a, rows):
    return jnp.pad(a, ((0, rows - a.shape[0]), (0, 0)))


def kernel(x, meta_tokens, ffn1_w_gu, ffn1_w_down, ln1_g, ln1_b, w_in, conv_w, pool_w, pool_scale, w_out, ln2_g, ln2_b, ffn2_w_gu, ffn2_w_down, ln3_g, ln3_b, loss_target, m_meta_tokens, m_ffn1_w_gu, m_ffn1_w_down, m_ln1_g, m_ln1_b, m_w_in, m_conv_w, m_pool_w, m_pool_scale, m_w_out, m_ln2_g, m_ln2_b, m_ffn2_w_gu, m_ffn2_w_down, m_ln3_g, m_ln3_b, v_meta_tokens, v_ffn1_w_gu, v_ffn1_w_down, v_ln1_g, v_ln1_b, v_w_in, v_conv_w, v_pool_w, v_pool_scale, v_w_out, v_ln2_g, v_ln2_b, v_ffn2_w_gu, v_ffn2_w_down, v_ln3_g, v_ln3_b):
    names = ["meta_tokens", "ffn1_w_gu", "ffn1_w_down", "ln1_g", "ln1_b", "w_in", "conv_w", "pool_w", "pool_scale", "w_out",
             "ln2_g", "ln2_b", "ffn2_w_gu", "ffn2_w_down", "ln3_g", "ln3_b"]
    w_of = dict(zip(names, [meta_tokens, ffn1_w_gu, ffn1_w_down, ln1_g, ln1_b, w_in, conv_w, pool_w, pool_scale, w_out,
                            ln2_g, ln2_b, ffn2_w_gu, ffn2_w_down, ln3_g, ln3_b]))
    m_of = dict(zip(names, [m_meta_tokens, m_ffn1_w_gu, m_ffn1_w_down, m_ln1_g, m_ln1_b, m_w_in, m_conv_w, m_pool_w, m_pool_scale,
                            m_w_out, m_ln2_g, m_ln2_b, m_ffn2_w_gu, m_ffn2_w_down, m_ln3_g, m_ln3_b]))
    v_of = dict(zip(names, [v_meta_tokens, v_ffn1_w_gu, v_ffn1_w_down, v_ln1_g, v_ln1_b, v_w_in, v_conv_w, v_pool_w, v_pool_scale,
                            v_w_out, v_ln2_g, v_ln2_b, v_ffn2_w_gu, v_ffn2_w_down, v_ln3_g, v_ln3_b]))

    n_seq, seq, d = x.shape
    seq_len = seq + N_META
    n_rows = n_seq * seq_len
    tp = -(-n_rows // ROW_ALIGN) * ROW_ALIGN
    c_conv = conv_w.shape[2] * N_DEV
    p_pool = pool_scale.shape[1]
    pg = pool_w.shape[3]
    assert c_conv == p_pool and p_pool == N_POOL_GROUPS * pg and POOL_WINDOWS == tuple(2 << g for g in range(N_POOL_GROUPS))
    assert (N_POOL_GROUPS * pg * pg) % d == 0 and pg % LANE == 0

    xi, yi, ci = _place()
    dev_index = 4 * xi + 2 * yi + ci
    dev = jnp.reshape(dev_index, (1,)).astype(jnp.int32)
    chip = jnp.reshape(2 * xi + yi, (1,)).astype(jnp.int32)
    place = tuple(jnp.reshape(a, (1,)).astype(jnp.int32) for a in (xi, yi, ci))

    big = ["ffn1_w_gu", "ffn1_w_down", "w_in", "w_out", "ffn2_w_gu", "ffn2_w_down"]
    wcol = d // N_DEV
    conv_rows = 8
    small_local = jnp.concatenate([
        meta_tokens,
        pool_w[0].reshape(N_POOL_GROUPS * (pg // N_DEV), pg),
        jnp.pad(conv_w[0], ((0, conv_rows - CONV_K), (0, wcol - conv_w.shape[2]))),
    ], axis=0)
    lands = {n: _into_slab(f"slab_{n}", w_of[n][0], dev, BF16) for n in big}
    lands["small"] = _into_slab("slab_small", small_local, dev, F32)
    started = {}

    def start(tag, which, after=()):
        per, token = _gather_start(f"ag_start_{tag}", [lands[n] for n in which], after)
        started.update(zip(which, per))
        return token

    def gathered(n, after, then_start=()):
        land = _gather_wait(f"ag_wait_{n}", started[n], after)
        deps = (start(f"after_{n}", then_start, (land,)),) if then_start else ()
        return _gather_finish(f"ag_finish_{n}", land, deps)

    gather_token = start("first", ["small", "ffn1_w_gu"])
    small_all = gathered("small", gather_token)
    r0, r1 = N_META, N_META + N_POOL_GROUPS * (pg // N_DEV)
    meta_full = jnp.transpose(small_all[:, :r0], (1, 0, 2)).reshape(N_META, d)
    pool_w_full = jnp.transpose(small_all[:, r0:r1].reshape(N_DEV, N_POOL_GROUPS, pg // N_DEV, pg), (1, 0, 2, 3)).reshape(N_POOL_GROUPS, pg, pg)
    conv_w_full = jnp.transpose(small_all[:, r1:r1 + CONV_K, :conv_w.shape[2]], (1, 0, 2)).reshape(CONV_K, c_conv)
    pool_w_b = pool_w_full.astype(BF16)

    h0 = jnp.concatenate([jnp.broadcast_to(meta_full[None], (n_seq, N_META, d)), x], axis=1).reshape(n_rows, d)
    h0 = _pad_rows(h0, tp)
    h0_b = h0.astype(BF16)
    tgt = _pad_rows(jnp.pad(loss_target, ((0, 0), (N_META, 0), (0, 0))).reshape(n_rows, d), tp)

    early = (h0_b, tgt) + tuple(lands[n] for n in big[1:])
    early += tuple(a[n][0] for n in ("ffn1_w_gu", "ffn2_w_gu") for a in (m_of, v_of))
    wgu1 = gathered("ffn1_w_gu", early, ["ffn1_w_down", "w_in"])
    gu1, act1 = _ffn_gu("ffn1_gu", h0_b, wgu1)
    wd1 = gathered("ffn1_w_down", act1, ["w_out", "ffn2_w_gu"]).reshape(N_CHIP, -1, d)
    pre1 = _ffn_down("ffn1_down", act1, wd1, h0)
    win_all = gathered("w_in", pre1)
    h1, h1_b = _ln_fwd("ln1", pre1, ln1_g, ln1_b)

    u = _proj_in("mix_in", h1_b, win_all)
    wout_all = gathered("w_out", u)
    y_conv = _conv_fwd("mix_conv", u, conv_w_full, n_rows, seq_len)
    dpool = _pool_fwd("mix_pool", u, 3 * c_conv, p_pool, pg, n_rows, seq_len)
    ypre, y_pool = _pool_mix("mix_pool_w", dpool, pool_w_b, pool_scale)
    y_mix = jnp.concatenate([y_conv, y_pool], axis=1)
    pre2 = _proj_out("mix_out", y_mix, wout_all, h1)
    land_gu2 = _gather_wait("ag_wait_ffn2_w_gu", started["ffn2_w_gu"], pre2)
    token = start("after_ffn2_w_gu", ["ffn2_w_down"], (land_gu2,))
    forwarding, _ = _forward_start("ag_forward_start_ffn2_w_gu", land_gu2, token)
    h2, h2_b = _ln_fwd("ln2", pre2, ln2_g, ln2_b)
    wgu2 = _forward_wait("ag_forward_wait_ffn2_w_gu", forwarding, h2_b)

    gu2, act2 = _ffn_gu("ffn2_gu", h2_b, wgu2)
    wd2 = gathered("ffn2_w_down", act2).reshape(N_CHIP, -1, d)
    pre3 = _ffn_down("ffn2_down", act2, wd2, h2)

    dpre3, dpre3_b, d_ln3_g, d_ln3_b, sq = _ln_loss_bwd("ln3_loss", pre3, tgt, ln3_g, ln3_b, n_rows, seq_len)
    loss = lax.psum(0.5 * jnp.sum(sq) / d, ("x", "y", "c"))
    in_sibling, reducing = [], {}

    def exchange(after, new=None):
        tokens = []
        while in_sibling:
            n, started = in_sibling.pop(0)
            part, from_sibling = _exchange_wait(f"rs_sibling_wait_{n}", started, _route_sibling, after)
            summed = _add_sibling(f"rs_add_{n}", part, from_sibling, place)
            started, token = _exchange_start(f"rs_chips_start_{n}", summed, _route_chips)
            reducing[n] = (part, from_sibling, started)
            tokens.append(token)
        if new is not None:
            started, token = _exchange_start(f"rs_sibling_start_{new[0]}", new[1], _route_sibling)
            in_sibling.append((new[0], started))
            tokens.append(token)
        return tuple(tokens)

    dgu2 = _ffn_bwd_dgu("ffn2_bwd_dgu", dpre3_b, wd2, gu2)
    g_wd2 = _ffn_bwd_wd("ffn2_bwd_wd", act2, dpre3_b)
    tokens = exchange(g_wd2, ("ffn2_w_down", g_wd2.reshape(N_DEV, -1, d)))
    g_wgu2 = _ffn_bwd_wgu("ffn2_bwd_wgu", h2_b, dgu2, tokens)
    tokens = exchange(g_wgu2, ("ffn2_w_gu", g_wgu2))
    dh2 = _ffn_bwd_dh("ffn2_bwd_dh", dgu2, wgu2, dpre3, tokens)
    tokens = exchange(dh2)
    dpre2, dpre2_b, d_ln2_g, d_ln2_b = _ln_bwd("ln2_bwd", pre2, dh2, ln2_g)

    dy_mix = _proj_out_bwd_y("mix_out_bwd_y", dpre2_b, wout_all, tokens)
    g_wout = _proj_out_bwd_w("mix_out_bwd_w", y_mix, dpre2_b, N_DEV)
    tokens = exchange(g_wout, ("w_out", g_wout))
    dyps, d_pool_scale = _pool_scale_bwd("mix_pool_scale_bwd", dy_mix, 1, ypre, pool_scale)
    dd = _pool_mix_bwd_in("mix_pool_w_bwd_in", dyps, pool_w_b)
    d_pool_w = _pool_mix_bwd_w("mix_pool_w_bwd_w", dpool, dyps, pg)
    du_pool = _pool_bwd("mix_pool_bwd", dd, pg, n_rows, seq_len)
    du_b, du_c, du_x, d_conv_w = _conv_bwd("mix_conv_bwd", u, conv_w_full, dy_mix, n_rows, seq_len)
    du = jnp.concatenate([du_b, du_c, du_x, du_pool], axis=1)
    g_win = _proj_in_bwd_w("mix_in_bwd_w", h1_b, du, N_DEV, tokens)
    tokens = exchange(g_win, ("w_in", g_win))
    dh1 = _proj_in_bwd_h("mix_in_bwd_h", du, win_all, dpre2, tokens)
    tokens = exchange(dh1)
    dpre1, dpre1_b, d_ln1_g, d_ln1_b = _ln_bwd("ln1_bwd", pre1, dh1, ln1_g)

    def widen(a):
        return jnp.pad(a, ((0, 0), (0, d - a.shape[1])))

    small_part = jnp.concatenate([
        d_ln1_g, d_ln1_b, d_ln2_g, d_ln2_b, d_ln3_g, d_ln3_b, widen(d_pool_scale), widen(d_conv_w), d_pool_w.reshape(-1, d)], axis=0)
    n_small_rows = small_part.shape[0]
    small_part = _pad_rows(small_part, -(-n_small_rows // 8) * 8)
    small_started, token = _gather_start("ag_start_small_grads", [_into_slab("slab_small_grads", small_part, dev, F32)])

    dgu1 = _ffn_bwd_dgu("ffn1_bwd_dgu", dpre1_b, wd1, gu1, tokens + (token,))
    g_wgu1 = _ffn_bwd_wgu("ffn1_bwd_wgu", h0_b, dgu1)
    tokens = exchange(g_wgu1, ("ffn1_w_gu", g_wgu1))
    g_wd1 = _ffn_bwd_wd("ffn1_bwd_wd", act1, dpre1_b, tokens)
    tokens = exchange(g_wd1, ("ffn1_w_down", g_wd1.reshape(N_DEV, -1, d)))
    dh0 = _ffn_bwd_dh("ffn1_bwd_dh", dgu1, wgu1, dpre1, tokens)
    tokens = exchange(dh0)

    dh0_seq = dh0[:n_rows].reshape(n_seq, seq_len, d)
    grad_x = dh0_seq[:, N_META:]
    d_meta = jnp.sum(dh0_seq[:, :N_META], axis=0)

    grads, deltas, new_m, new_v = {}, {}, {}, {}
    after = tokens[0]
    for n in ["ffn2_w_down", "ffn2_w_gu", "w_out", "w_in", "ffn1_w_gu", "ffn1_w_down"]:
        part, from_sibling, started = reducing[n]
        _, from_chips = _exchange_wait(f"rs_chips_wait_{n}", started, _route_chips, after)
        g, dl, mm, vv = _adamw_big(f"adamw_{n}", part, from_sibling, from_chips, dev, chip, w_of[n][0], m_of[n][0], v_of[n][0])
        grads[n], deltas[n], new_m[n], new_v[n] = g[None], dl[None], mm[None], vv[None]
        after = g

    small_sum = _sum_parts("small_sum", _gather_finish("ag_finish_small_grads", _gather_wait("ag_wait_small_grads", small_started[0], dh0)))
    meta_sum = _sum_parts("meta_sum", _all_gather("ag_meta_grads", _into_slab("slab_meta_grads", d_meta, dev, F32)))
    o = 7 + CONV_K
    g_small = {
        "ln1_g": small_sum[0:1], "ln1_b": small_sum[1:2], "ln2_g": small_sum[2:3], "ln2_b": small_sum[3:4],
        "ln3_g": small_sum[4:5], "ln3_b": small_sum[5:6], "pool_scale": small_sum[6:7, :p_pool],
        "conv_w": lax.dynamic_slice_in_dim(small_sum[7:o, :c_conv], dev_index * (c_conv // N_DEV), c_conv // N_DEV, axis=1)[None],
        "meta_tokens": lax.dynamic_slice_in_dim(meta_sum, dev_index * wcol, wcol, axis=1),
        "pool_w": lax.dynamic_slice_in_dim(small_sum[o:n_small_rows].reshape(N_POOL_GROUPS, pg, pg),
                                           dev_index * (pg // N_DEV), pg // N_DEV, axis=1)[None],
    }
    small = ["meta_tokens", "ln1_g", "ln1_b", "conv_w", "pool_w", "pool_scale", "ln2_g", "ln2_b", "ln3_g", "ln3_b"]

    def flat(a):
        return a.reshape(-1, a.shape[-1])

    outs = _adamw_small("adamw_small", [flat(w_of[n]) for n in small], [flat(g_small[n]) for n in small],
                        [flat(m_of[n]) for n in small], [flat(v_of[n]) for n in small])
    ns = len(small)
    for i, n in enumerate(small):
        shape = w_of[n].shape
        grads[n] = g_small[n].reshape(shape)
        deltas[n], new_m[n], new_v[n] = outs[i].reshape(shape), outs[ns + i].reshape(shape), outs[2 * ns + i].reshape(shape)

    return (loss, grad_x, *[grads[n] for n in names], *[deltas[n] for n in names],
            *[new_m[n] for n in names], *[new_v[n] for n in names])
```
